```python
import jax, jax.numpy as jnp
from jax import lax
import numpy as np

D_MODEL = 2048
BATCH = 1
SEQ = 8192
DEPTH = 1

MIX_WIDTH = D_MODEL
CHUNK = 128
SGU_HEADS = 8
SGU_WIDTH = MIX_WIDTH // 2
SGU_HEAD_DIM = SGU_WIDTH // SGU_HEADS
RET_HEADS = 8
RET_WIDTH = MIX_WIDTH - SGU_WIDTH
RET_HEAD_DIM = RET_WIDTH // RET_HEADS
IN_WIDTH = 2 * SGU_WIDTH + 4 * RET_WIDTH
D_FF = ((8 * D_MODEL // 3 + 255) // 256) * 256
ROPE_BASE = 10000.0
EPS = 1e-6

kernel_name = "hybrid_sgu_retention_block"


def _rms(x, g):
    xf = x.astype(jnp.float32)
    y = xf * lax.rsqrt(jnp.mean(xf * xf, axis=-1, keepdims=True) + EPS)
    return (y * g.astype(jnp.float32)).astype(x.dtype)


def _rotary(x, pos):
    half = x.shape[-1] // 2
    inv = 1.0 / (ROPE_BASE ** (jnp.arange(half, dtype=jnp.float32) / half))
    ang = pos.astype(jnp.float32)[:, None] * inv[None, :]
    cos = jnp.cos(ang)[None, :, None, :]
    sin = jnp.sin(ang)[None, :, None, :]
    x1, x2 = x[..., :half], x[..., half:]
    return jnp.concatenate([x1 * cos - x2 * sin, x1 * sin + x2 * cos], axis=-1)


def _spatial_gate(u, v, ln_g, ln_b, w_s, b_s):
    B, S, _ = u.shape
    nc = S // CHUNK
    u = u.reshape(B, nc, CHUNK, SGU_HEADS, SGU_HEAD_DIM).astype(jnp.float32)
    v = v.reshape(B, nc, CHUNK, SGU_HEADS, SGU_HEAD_DIM).astype(jnp.float32)
    mu = jnp.mean(v, axis=-1, keepdims=True)
    var = jnp.mean(jnp.square(v - mu), axis=-1, keepdims=True)
    vn = (v - mu) * lax.rsqrt(var + EPS) * ln_g.astype(jnp.float32) + ln_b.astype(jnp.float32)
    causal = jnp.tril(jnp.ones((CHUNK, CHUNK), dtype=jnp.float32))
    ws = w_s.astype(jnp.float32) * causal[None]
    mixed = jnp.einsum('hts,bnshd->bnthd', ws, vn) + b_s.astype(jnp.float32).T[:, :, None]
    return (u * mixed).reshape(B, S, SGU_WIDTH)


def _retention(q, k, v, g, gn_g, gn_b, pos):
    B, S, _ = q.shape
    nc = S // CHUNK
    H, Dk = RET_HEADS, RET_HEAD_DIM
    q = _rotary(q.astype(jnp.float32).reshape(B, S, H, Dk), pos)
    k = _rotary(k.astype(jnp.float32).reshape(B, S, H, Dk), pos) * (Dk ** -0.5)
    q = q.reshape(B, nc, CHUNK, H, Dk)
    k = k.reshape(B, nc, CHUNK, H, Dk)
    v = v.astype(jnp.float32).reshape(B, nc, CHUNK, H, Dk)

    log_gamma = jnp.log(1.0 - jnp.exp2(-5.0 - jnp.arange(H, dtype=jnp.float32)))
    idx = jnp.arange(CHUNK, dtype=jnp.float32)
    diff = idx[:, None] - idx[None, :]
    decay = jnp.where(diff[None] >= 0,
                      jnp.exp(jnp.maximum(diff, 0.0)[None] * log_gamma[:, None, None]), 0.0)
    zeta = jnp.exp((CHUNK - 1.0 - idx)[None, :] * log_gamma[:, None])
    xi = jnp.exp((idx + 1.0)[None, :] * log_gamma[:, None])
    gamma_c = jnp.exp(CHUNK * log_gamma)

    scores = jnp.einsum('bnihd,bnjhd->bnhij', q, k) * decay[None, None]
    intra = jnp.einsum('bnhij,bnjhe->bnihe', scores, v)

    kv_chunk = jnp.einsum('bnjhd,bnjhe,hj->nbhde', k, v, zeta)

    def step(R, kv):
        return gamma_c[None, :, None, None] * R + kv, R

    _, R_prev = lax.scan(step, jnp.zeros((B, H, Dk, Dk), jnp.float32), kv_chunk)
    inter = jnp.einsum('bnihd,nbhde,hi->bnihe', q, R_prev, xi)

    o = intra + inter
    mu = jnp.mean(o, axis=-1, keepdims=True)
    var = jnp.mean(jnp.square(o - mu), axis=-1, keepdims=True)
    o = (o - mu) * lax.rsqrt(var + EPS) * gn_g.astype(jnp.float32) + gn_b.astype(jnp.float32)
    o = o.reshape(B, S, RET_WIDTH)
    return jax.nn.silu(g.astype(jnp.float32)) * o


def setup_inputs(seed: int = 0) -> dict:
    key = jax.random.key(seed)
    ks = jax.random.split(key, 18)
    f = jnp.float32
    n = lambda k, shape, s: jax.random.normal(k, shape, f) * s
    return {
        "x": jax.random.normal(ks[0], (BATCH, SEQ, D_MODEL), f),
        "norm1_g": 1.0 + n(ks[1], (DEPTH, D_MODEL), 0.02),
        "w_in": n(ks[2], (DEPTH, D_MODEL, IN_WIDTH), D_MODEL ** -0.5),
        "sgu_ln_g": 1.0 + n(ks[3], (DEPTH, SGU_HEADS, SGU_HEAD_DIM), 0.02),
        "sgu_ln_b": n(ks[4], (DEPTH, SGU_HEADS, SGU_HEAD_DIM), 0.02),
        "w_spatial": n(ks[5], (DEPTH, SGU_HEADS, CHUNK, CHUNK), CHUNK ** -0.5),
        "b_spatial": 1.0 + n(ks[6], (DEPTH, SGU_HEADS, CHUNK), 0.02),
        "ret_gn_g": 1.0 + n(ks[7], (DEPTH, RET_HEADS, RET_HEAD_DIM), 0.02),
        "ret_gn_b": n(ks[8], (DEPTH, RET_HEADS, RET_HEAD_DIM), 0.02),
        "w_out": n(ks[9], (DEPTH, MIX_WIDTH, D_MODEL), MIX_WIDTH ** -0.5),
        "norm2_g": 1.0 + n(ks[10], (DEPTH, D_MODEL), 0.02),
        "w_gate": n(ks[11], (DEPTH, D_MODEL, D_FF), D_MODEL ** -0.5),
        "w_up": n(ks[12], (DEPTH, D_MODEL, D_FF), D_MODEL ** -0.5),
        "w_down": n(ks[13], (DEPTH, D_FF, D_MODEL), D_FF ** -0.5),
        "final_norm_g": 1.0 + n(ks[14], (D_MODEL,), 0.02),
    }


def reference(x, norm1_g, w_in, sgu_ln_g, sgu_ln_b, w_spatial, b_spatial, ret_gn_g, ret_gn_b,
              w_out, norm2_g, w_gate, w_up, w_down, final_norm_g):
    S = x.shape[1]
    pos = jnp.arange(S, dtype=jnp.int32)
    splits = [SGU_WIDTH, 2 * SGU_WIDTH, 2 * SGU_WIDTH + RET_WIDTH,
              2 * SGU_WIDTH + 2 * RET_WIDTH, 2 * SGU_WIDTH + 3 * RET_WIDTH]
    for l in range(DEPTH):
        h = _rms(x, norm1_g[l])
        proj = jnp.einsum('bsd,de->bse', h, w_in[l])
        u, v_s, q, k, v_r, g = jnp.split(proj, splits, axis=-1)
        y_sgu = _spatial_gate(u, v_s, sgu_ln_g[l], sgu_ln_b[l], w_spatial[l], b_spatial[l])
        y_ret = _retention(q, k, v_r, g, ret_gn_g[l], ret_gn_b[l], pos)
        mixed = jnp.concatenate([y_sgu, y_ret], axis=-1).astype(x.dtype)
        x = x + jnp.einsum('bse,ed->bsd', mixed, w_out[l])
        h2 = _rms(x, norm2_g[l])
        a = jax.nn.silu(jnp.einsum('bsd,df->bsf', h2, w_gate[l])) * jnp.einsum('bsd,df->bsf', h2, w_up[l])
        x = x + jnp.einsum('bsf,fd->bsd', a, w_down[l])
    return _rms(x, final_norm_g)
```

```python
import functools

import jax
import jax.numpy as jnp
from jax import lax
from jax.experimental import pallas as pl
from jax.experimental.pallas import tpu as pltpu

D_MODEL = 2048
CHUNK = 128
HEADS = 8
HEAD_DIM = 128
SGU_WIDTH = HEADS * HEAD_DIM
RET_WIDTH = HEADS * HEAD_DIM
MIX_WIDTH = SGU_WIDTH + RET_WIDTH
IN_WIDTH = 2 * SGU_WIDTH + 4 * RET_WIDTH
ROPE_BASE = 10000.0
EPS = 1e-6

OFF_U, OFF_VS, OFF_Q, OFF_K, OFF_VR, OFF_G = (i * SGU_WIDTH for i in range(6))

V7X_VMEM_LIMIT_BYTES = 56 * 1024 * 1024

F32 = jnp.float32
BF16 = jnp.bfloat16


def _rms_rows(x, g):
    ms = jnp.mean(x * x, axis=-1, keepdims=True)
    return x * lax.rsqrt(ms + EPS) * g


def _norm_rows(x, g, b):
    mu = jnp.mean(x, axis=-1, keepdims=True)
    d = x - mu
    var = jnp.mean(d * d, axis=-1, keepdims=True)
    return d * lax.rsqrt(var + EPS) * g + b


def _in_proj_kernel(x_ref, g_ref, w_ref, o_ref, h_ref):
    @pl.when(pl.program_id(1) == 0)
    def _():
        h_ref[...] = _rms_rows(x_ref[...], g_ref[...]).astype(BF16)

    o_ref[...] = jnp.dot(h_ref[...], w_ref[...], preferred_element_type=F32).astype(o_ref.dtype)


def _in_proj(x, g, w, *, tm, tn):
    s, d = x.shape
    n = w.shape[1]
    return pl.pallas_call(
        _in_proj_kernel,
        grid=(s // tm, n // tn),
        in_specs=[
            pl.BlockSpec((tm, d), lambda i, j: (i, 0)),
            pl.BlockSpec((1, d), lambda i, j: (0, 0)),
            pl.BlockSpec((d, tn), lambda i, j: (0, j)),
        ],
        out_specs=pl.BlockSpec((tm, tn), lambda i, j: (i, j)),
        out_shape=jax.ShapeDtypeStruct((s, n), BF16),
        scratch_shapes=[pltpu.VMEM((tm, d), BF16)],
        compiler_params=pltpu.CompilerParams(
            dimension_semantics=("parallel", "arbitrary"),
            vmem_limit_bytes=V7X_VMEM_LIMIT_BYTES),
        name="in_proj",
    )(x, g, w)


def _mixer_kernel(p_ref, cos_ref, sin_ref, lng_ref, lnb_ref, ws_ref, bs_ref,
                  gng_ref, gnb_ref, decay_ref, zeta_ref, xi_ref, gammac_ref,
                  o_ref, state_ref, *, chunks_per_step):
    @pl.when(pl.program_id(0) == 0)
    def _():
        state_ref[...] = jnp.zeros_like(state_ref)

    row_id = lax.broadcasted_iota(jnp.int32, (CHUNK, CHUNK), 0)
    col_id = lax.broadcasted_iota(jnp.int32, (CHUNK, CHUNK), 1)
    causal = row_id >= col_id
    k_scale = HEAD_DIM ** -0.5

    for c in range(chunks_per_step):
        rows = slice(c * CHUNK, (c + 1) * CHUNK)
        cos = cos_ref[rows, :]
        sin = sin_ref[rows, :]
        for h in range(HEADS):
            def cols(off):
                return slice(off + h * HEAD_DIM, off + (h + 1) * HEAD_DIM)

            u = p_ref[rows, cols(OFF_U)].astype(F32)
            vs = p_ref[rows, cols(OFF_VS)].astype(F32)
            vn = _norm_rows(vs, lng_ref[h:h + 1, :], lnb_ref[h:h + 1, :])
            w_causal = jnp.where(causal, ws_ref[h], 0.0).astype(BF16)
            mixed = jnp.dot(w_causal, vn.astype(BF16), preferred_element_type=F32) + bs_ref[h]
            o_ref[rows, cols(0)] = (u * mixed).astype(o_ref.dtype)

            q = p_ref[rows, cols(OFF_Q)].astype(F32)
            k = p_ref[rows, cols(OFF_K)].astype(F32)
            vr = p_ref[rows, cols(OFF_VR)]
            g = p_ref[rows, cols(OFF_G)].astype(F32)
            qr = q * cos + pltpu.roll(q, HEAD_DIM // 2, axis=1) * sin
            kr = (k * cos + pltpu.roll(k, HEAD_DIM // 2, axis=1) * sin) * k_scale
            qb = qr.astype(BF16)
            scores = lax.dot_general(qb, kr.astype(BF16), (((1,), (1,)), ((), ())),
                                     preferred_element_type=F32) * decay_ref[h]
            intra = jnp.dot(scores.astype(BF16), vr, preferred_element_type=F32)
            state = state_ref[h]
            inter = jnp.dot(qb, state.astype(BF16), preferred_element_type=F32) * xi_ref[h]
            kz = (kr * zeta_ref[h]).astype(BF16)
            kv = lax.dot_general(kz, vr, (((0,), (0,)), ((), ())), preferred_element_type=F32)
            state_ref[h] = gammac_ref[h] * state + kv
            on = _norm_rows(intra + inter, gng_ref[h:h + 1, :], gnb_ref[h:h + 1, :])
            y = g * jax.nn.sigmoid(g) * on
            o_ref[rows, cols(SGU_WIDTH)] = y.astype(o_ref.dtype)


def _mixer(proj, cos2, sin2, lng, lnb, ws, bs, gng, gnb, decay, zeta, xi, gammac, *, chunks_per_step):
    s = proj.shape[0]
    rows = chunks_per_step * CHUNK
    full = lambda shape: pl.BlockSpec(shape, lambda i: (0,) * len(shape))
    return pl.pallas_call(
        functools.partial(_mixer_kernel, chunks_per_step=chunks_per_step),
        grid=(s // rows,),
        in_specs=[
            pl.BlockSpec((rows, IN_WIDTH), lambda i: (i, 0)),
            pl.BlockSpec((rows, HEAD_DIM), lambda i: (i, 0)),
            pl.BlockSpec((rows, HEAD_DIM), lambda i: (i, 0)),
            full((HEADS, HEAD_DIM)), full((HEADS, HEAD_DIM)),
            full((HEADS, CHUNK, CHUNK)), full((HEADS, CHUNK, 1)),
            full((HEADS, HEAD_DIM)), full((HEADS, HEAD_DIM)),
            full((HEADS, CHUNK, CHUNK)), full((HEADS, CHUNK, 1)), full((HEADS, CHUNK, 1)),
            full((HEADS, 1, HEAD_DIM)),
        ],
        out_specs=pl.BlockSpec((rows, MIX_WIDTH), lambda i: (i, 0)),
        out_shape=jax.ShapeDtypeStruct((s, MIX_WIDTH), BF16),
        scratch_shapes=[pltpu.VMEM((HEADS, HEAD_DIM, HEAD_DIM), F32)],
        compiler_params=pltpu.CompilerParams(
            dimension_semantics=("arbitrary",),
            vmem_limit_bytes=V7X_VMEM_LIMIT_BYTES),
        name="mixer",
    )(proj, cos2, sin2, lng, lnb, ws, bs, gng, gnb, decay, zeta, xi, gammac)


def _out_proj_kernel(m_ref, w_ref, x_ref, o_ref):
    o_ref[...] = x_ref[...] + jnp.dot(m_ref[...], w_ref[...], preferred_element_type=F32)


def _out_proj(mixed, w, x, *, tm, tn):
    s, k = mixed.shape
    n = w.shape[1]
    return pl.pallas_call(
        _out_proj_kernel,
        grid=(s // tm, n // tn),
        in_specs=[
            pl.BlockSpec((tm, k), lambda i, j: (i, 0)),
            pl.BlockSpec((k, tn), lambda i, j: (0, j)),
            pl.BlockSpec((tm, tn), lambda i, j: (i, j)),
        ],
        out_specs=pl.BlockSpec((tm, tn), lambda i, j: (i, j)),
        out_shape=jax.ShapeDtypeStruct((s, n), F32),
        compiler_params=pltpu.CompilerParams(
            dimension_semantics=("parallel", "arbitrary"),
            vmem_limit_bytes=V7X_VMEM_LIMIT_BYTES),
        name="out_proj",
    )(mixed, w, x)


def _ffn_kernel(x_ref, g2_ref, wg_ref, wu_ref, wd_ref, gf_ref, o_ref, h_ref, *, final_norm):
    f = pl.program_id(1)

    @pl.when(f == 0)
    def _():
        x = x_ref[...]
        h_ref[...] = _rms_rows(x, g2_ref[...]).astype(BF16)
        o_ref[...] = x

    h = h_ref[...]
    gate = jnp.dot(h, wg_ref[...], preferred_element_type=F32)
    up = jnp.dot(h, wu_ref[...], preferred_element_type=F32)
    a = (gate * jax.nn.sigmoid(gate) * up).astype(BF16)
    o_ref[...] += jnp.dot(a, wd_ref[...], preferred_element_type=F32)

    if final_norm:
        @pl.when(f == pl.num_programs(1) - 1)
        def _():
            o_ref[...] = _rms_rows(o_ref[...], gf_ref[...])


def _ffn(x1, g2, wg, wu, wd, gf, *, tm, tf, final_norm):
    s, d = x1.shape
    dff = wg.shape[1]
    return pl.pallas_call(
        functools.partial(_ffn_kernel, final_norm=final_norm),
        grid=(s // tm, dff // tf),
        in_specs=[
            pl.BlockSpec((tm, d), lambda i, f: (i, 0)),
            pl.BlockSpec((1, d), lambda i, f: (0, 0)),
            pl.BlockSpec((d, tf), lambda i, f: (0, f)),
            pl.BlockSpec((d, tf), lambda i, f: (0, f)),
            pl.BlockSpec((tf, d), lambda i, f: (f, 0)),
            pl.BlockSpec((1, d), lambda i, f: (0, 0)),
        ],
        out_specs=pl.BlockSpec((tm, d), lambda i, f: (i, 0)),
        out_shape=jax.ShapeDtypeStruct((s, d), F32),
        scratch_shapes=[pltpu.VMEM((tm, d), BF16)],
        compiler_params=pltpu.CompilerParams(
            dimension_semantics=("parallel", "arbitrary"),
            vmem_limit_bytes=V7X_VMEM_LIMIT_BYTES),
        name="ffn",
    )(x1, g2, wg, wu, wd, gf)


def _rotary_tables(seq):
    half = HEAD_DIM // 2
    inv = 1.0 / (ROPE_BASE ** (jnp.arange(half, dtype=F32) / half))
    ang = jnp.arange(seq, dtype=jnp.int32).astype(F32)[:, None] * inv[None, :]
    cos, sin = jnp.cos(ang), jnp.sin(ang)
    return jnp.concatenate([cos, cos], axis=-1), jnp.concatenate([-sin, sin], axis=-1)


def _decay_tables():
    log_gamma = jnp.log(1.0 - jnp.exp2(-5.0 - jnp.arange(HEADS, dtype=F32)))
    idx = jnp.arange(CHUNK, dtype=F32)
    diff = idx[:, None] - idx[None, :]
    decay = jnp.where(diff[None] >= 0,
                      jnp.exp(jnp.maximum(diff, 0.0)[None] * log_gamma[:, None, None]), 0.0)
    zeta = jnp.exp((CHUNK - 1.0 - idx)[None, :] * log_gamma[:, None])
    xi = jnp.exp((idx + 1.0)[None, :] * log_gamma[:, None])
    gamma_c = jnp.exp(CHUNK * log_gamma)
    gamma_c = jnp.broadcast_to(gamma_c[:, None, None], (HEADS, 1, HEAD_DIM))
    return decay, zeta[:, :, None], xi[:, :, None], gamma_c


def kernel(x, norm1_g, w_in, sgu_ln_g, sgu_ln_b, w_spatial, b_spatial, ret_gn_g, ret_gn_b,
           w_out, norm2_g, w_gate, w_up, w_down, final_norm_g):
    batch, seq, d = x.shape
    depth = w_in.shape[0]
    cos2, sin2 = _rotary_tables(seq)
    decay, zeta, xi, gamma_c = _decay_tables()
    outs = []
    for b in range(batch):
        xb = x[b]
        for l in range(depth):
            proj = _in_proj(xb, norm1_g[l][None, :], w_in[l].astype(BF16), tm=1024, tn=1024)
            mixed = _mixer(proj, cos2, sin2, sgu_ln_g[l], sgu_ln_b[l], w_spatial[l],
                           b_spatial[l][:, :, None], ret_gn_g[l], ret_gn_b[l],
                           decay, zeta, xi, gamma_c, chunks_per_step=2)
            x1 = _out_proj(mixed, w_out[l].astype(BF16), xb, tm=1024, tn=1024)
            xb = _ffn(x1, norm2_g[l][None, :], w_gate[l].astype(BF16), w_up[l].astype(BF16),
                      w_down[l].astype(BF16), final_norm_g[None, :], tm=512, tf=512,
                      final_norm=(l == depth - 1))
        outs.append(xb)
    return outs[0][None] if batch == 1 else jnp.stack(outs)
```

```python
import functools

import numpy as np

import jax
import jax.numpy as jnp
from jax import lax
from jax.experimental import pallas as pl
from jax.experimental.pallas import tpu as pltpu

D_MODEL = 2048
CHUNK = 128
HEADS = 8
HEAD_DIM = 128
SGU_WIDTH = HEADS * HEAD_DIM
RET_WIDTH = HEADS * HEAD_DIM
MIX_WIDTH = SGU_WIDTH + RET_WIDTH
IN_WIDTH = 2 * SGU_WIDTH + 4 * RET_WIDTH
ROPE_BASE = 10000.0
EPS = 1e-6

OFF_U, OFF_VS, OFF_Q, OFF_K, OFF_VR, OFF_G = (i * SGU_WIDTH for i in range(6))

V7X_VMEM_LIMIT_BYTES = 60 * 1024 * 1024

F32 = jnp.float32
BF16 = jnp.bfloat16


def _rms_rows(x, g):
    ms = jnp.mean(x * x, axis=-1, keepdims=True)
    return x * lax.rsqrt(ms + EPS) * g


def _in_proj_kernel(x_ref, g_ref, w_ref, o_ref, h_ref):
    @pl.when(pl.program_id(1) == 0)
    def _():
        h_ref[...] = _rms_rows(x_ref[...], g_ref[...]).astype(BF16)

    o_ref[...] = jnp.dot(h_ref[...], w_ref[...].astype(BF16),
                         preferred_element_type=F32).astype(o_ref.dtype)


def _in_proj(x, g, w, *, tm, tn):
    s, d = x.shape
    n = w.shape[1]
    return pl.pallas_call(
        _in_proj_kernel,
        grid=(s // tm, n // tn),
        in_specs=[
            pl.BlockSpec((tm, d), lambda i, j: (i, 0)),
            pl.BlockSpec((1, d), lambda i, j: (0, 0)),
            pl.BlockSpec((d, tn), lambda i, j: (0, j)),
        ],
        out_specs=pl.BlockSpec((tm, tn), lambda i, j: (i, j)),
        out_shape=jax.ShapeDtypeStruct((s, n), BF16),
        scratch_shapes=[pltpu.VMEM((tm, d), BF16)],
        compiler_params=pltpu.CompilerParams(
            dimension_semantics=("parallel", "arbitrary"),
            vmem_limit_bytes=V7X_VMEM_LIMIT_BYTES),
        name="in_proj",
    )(x, g, w)


def _split_bf16(x):
    hi = x.astype(BF16)
    lo = (x - hi.astype(F32)).astype(BF16)
    return jnp.concatenate([hi, lo], axis=1)


def _head_rows(stacked, h):
    return stacked[h * CHUNK:(h + 1) * CHUNK]


def _row_means(tiles, ones2):
    return jnp.dot(jnp.concatenate([_split_bf16(t) for t in tiles], axis=0), ones2,
                   preferred_element_type=F32)


def _mixer_kernel(p_ref, ca_ref, sa_ref, cb_ref, sb_ref, cbs_ref, sbs_ref,
                  lng_ref, lnb_ref, ws_ref, bs_ref, gng_ref, gnb_ref,
                  decay_ref, zeta_ref, xi_ref,
                  o_ref, state_ref, wc_ref, bias_ref, *, chunks_per_step, gamma_c):
    row_id = lax.broadcasted_iota(jnp.int32, (CHUNK, CHUNK), 0)
    col_id = lax.broadcasted_iota(jnp.int32, (CHUNK, CHUNK), 1)

    @pl.when(pl.program_id(0) == 0)
    def _():
        state_ref[...] = jnp.zeros_like(state_ref)
        for h in range(HEADS):
            wc_ref[h] = jnp.where(row_id >= col_id, ws_ref[h], 0.0).astype(BF16)
            b_col = jnp.sum(jnp.where(row_id == col_id, bs_ref[h:h + 1, :], 0.0), axis=1, keepdims=True)
            bias_ref[h] = jnp.broadcast_to(b_col, (CHUNK, HEAD_DIM))

    inv_d = 1.0 / HEAD_DIM
    ones1 = jnp.full((HEAD_DIM, HEAD_DIM), inv_d, BF16)
    ones2 = jnp.full((2 * HEAD_DIM, HEAD_DIM), inv_d, BF16)
    k_scale = HEAD_DIM ** -0.5
    chunk0 = pl.program_id(0) * chunks_per_step
    heads = range(HEADS)

    def cols(off, h):
        return slice(off + h * HEAD_DIM, off + (h + 1) * HEAD_DIM)

    for c in range(chunks_per_step):
        rows = slice(c * CHUNK, (c + 1) * CHUNK)

        ca = ca_ref[pl.ds(chunk0 + c, 1), :]
        sa = sa_ref[pl.ds(chunk0 + c, 1), :]
        cos2 = ca * cb_ref[...] - sa * sb_ref[...]
        sin2 = sa * cbs_ref[...] + ca * sbs_ref[...]
        cos2k = cos2 * k_scale
        sin2k = sin2 * k_scale

        vs_b = [p_ref[rows, cols(OFF_VS, h)] for h in heads]
        mu = jnp.dot(jnp.concatenate(vs_b, axis=0), ones1, preferred_element_type=F32)
        dv = [vs_b[h].astype(F32) - _head_rows(mu, h) for h in heads]
        var = _row_means([d * d for d in dv], ones2)
        for h in heads:
            vn = dv[h] * lax.rsqrt(_head_rows(var, h) + EPS) * lng_ref[h:h + 1, :] + lnb_ref[h:h + 1, :]
            mixed = jnp.dot(wc_ref[h], vn.astype(BF16), preferred_element_type=F32) + bias_ref[h]
            u = p_ref[rows, cols(OFF_U, h)].astype(F32)
            o_ref[rows, cols(0, h)] = (u * mixed).astype(o_ref.dtype)

        outs = []
        for h in heads:
            q = p_ref[rows, cols(OFF_Q, h)].astype(F32)
            k = p_ref[rows, cols(OFF_K, h)].astype(F32)
            vr = p_ref[rows, cols(OFF_VR, h)]
            qb = (q * cos2 + pltpu.roll(q, HEAD_DIM // 2, axis=1) * sin2).astype(BF16)
            kr = k * cos2k + pltpu.roll(k, HEAD_DIM // 2, axis=1) * sin2k
            scores = lax.dot_general(qb, kr.astype(BF16), (((1,), (1,)), ((), ())),
                                     preferred_element_type=F32) * decay_ref[h]
            intra = jnp.dot(scores.astype(BF16), vr, preferred_element_type=F32)
            state = state_ref[h]
            inter = jnp.dot(qb, state.astype(BF16), preferred_element_type=F32) * xi_ref[h]
            kz = (kr * zeta_ref[h]).astype(BF16)
            kv = lax.dot_general(kz, vr, (((0,), (0,)), ((), ())), preferred_element_type=F32)
            state_ref[h] = gamma_c[h] * state + kv
            outs.append(intra + inter)

        mu = _row_means(outs, ones2)
        do = [outs[h] - _head_rows(mu, h) for h in heads]
        var = _row_means([d * d for d in do], ones2)
        for h in heads:
            on = do[h] * lax.rsqrt(_head_rows(var, h) + EPS) * gng_ref[h:h + 1, :] + gnb_ref[h:h + 1, :]
            g = p_ref[rows, cols(OFF_G, h)].astype(F32)
            o_ref[rows, cols(SGU_WIDTH, h)] = (g * jax.nn.sigmoid(g) * on).astype(o_ref.dtype)


def _mixer_constants(seq):
    half = HEAD_DIM // 2
    inv = 1.0 / (ROPE_BASE ** (np.arange(half, dtype=np.float64) / half))
    inv2 = np.concatenate([inv, inv])
    sign = np.concatenate([-np.ones(half), np.ones(half)])
    ang_a = (CHUNK * np.arange(seq // CHUNK, dtype=np.float64))[:, None] * inv2[None, :]
    ang_b = np.arange(CHUNK, dtype=np.float64)[:, None] * inv2[None, :]
    rot = [np.cos(ang_a), np.sin(ang_a), np.cos(ang_b), np.sin(ang_b),
           sign * np.cos(ang_b), sign * np.sin(ang_b)]

    log_gamma = np.log(1.0 - np.exp2(-5.0 - np.arange(HEADS, dtype=np.float64)))
    idx = np.arange(CHUNK, dtype=np.float64)
    diff = idx[:, None] - idx[None, :]
    decay = np.where(diff[None] >= 0, np.exp(np.maximum(diff, 0.0)[None] * log_gamma[:, None, None]), 0.0)
    zeta = np.exp((CHUNK - 1.0 - idx)[None, :] * log_gamma[:, None])
    xi = np.exp((idx + 1.0)[None, :] * log_gamma[:, None])
    bcast = lambda v: np.broadcast_to(v[:, :, None], (HEADS, CHUNK, HEAD_DIM))
    gamma_c = tuple(float(np.float32(v)) for v in np.exp(CHUNK * log_gamma))
    tables = [jnp.asarray(np.ascontiguousarray(t), dtype=F32) for t in rot + [decay, bcast(zeta), bcast(xi)]]
    return tables, gamma_c


def _mixer(proj, lng, lnb, ws, bs, gng, gnb, *, chunks_per_step):
    s = proj.shape[0]
    n_chunks = s // CHUNK
    rows = chunks_per_step * CHUNK
    (ca, sa, cb, sb, cbs, sbs, decay, zeta, xi), gamma_c = _mixer_constants(s)
    full = lambda shape: pl.BlockSpec(shape, lambda i: (0,) * len(shape))
    head_tiles = full((HEADS, CHUNK, HEAD_DIM))
    head_rows = full((HEADS, HEAD_DIM))
    return pl.pallas_call(
        functools.partial(_mixer_kernel, chunks_per_step=chunks_per_step, gamma_c=gamma_c),
        grid=(s // rows,),
        in_specs=[
            pl.BlockSpec((rows, IN_WIDTH), lambda i: (i, 0)),
            full((n_chunks, HEAD_DIM)), full((n_chunks, HEAD_DIM)),
            full((CHUNK, HEAD_DIM)), full((CHUNK, HEAD_DIM)), full((CHUNK, HEAD_DIM)), full((CHUNK, HEAD_DIM)),
            head_rows, head_rows, head_tiles, head_rows, head_rows, head_rows,
            head_tiles, head_tiles, head_tiles,
        ],
        out_specs=pl.BlockSpec((rows, MIX_WIDTH), lambda i: (i, 0)),
        out_shape=jax.ShapeDtypeStruct((s, MIX_WIDTH), BF16),
        scratch_shapes=[pltpu.VMEM((HEADS, HEAD_DIM, HEAD_DIM), F32),
                        pltpu.VMEM((HEADS, CHUNK, CHUNK), BF16),
                        pltpu.VMEM((HEADS, CHUNK, HEAD_DIM), F32)],
        compiler_params=pltpu.CompilerParams(
            dimension_semantics=("arbitrary",),
            vmem_limit_bytes=V7X_VMEM_LIMIT_BYTES),
        name="mixer",
    )(proj, ca, sa, cb, sb, cbs, sbs, lng, lnb, ws, bs, gng, gnb, decay, zeta, xi)


def _out_proj_kernel(m_ref, w_ref, x_ref, o_ref, wb_ref):
    @pl.when(pl.program_id(0) == 0)
    def _():
        wb_ref[...] = w_ref[...].astype(BF16)

    o_ref[...] = x_ref[...] + jnp.dot(m_ref[...], wb_ref[...], preferred_element_type=F32)


def _out_proj(mixed, w, x, *, tm):
    s, k = mixed.shape
    n = w.shape[1]
    return pl.pallas_call(
        _out_proj_kernel,
        grid=(s // tm,),
        in_specs=[
            pl.BlockSpec((tm, k), lambda i: (i, 0)),
            pl.BlockSpec((k, n), lambda i: (0, 0), pipeline_mode=pl.Buffered(1)),
            pl.BlockSpec((tm, n), lambda i: (i, 0)),
        ],
        out_specs=pl.BlockSpec((tm, n), lambda i: (i, 0)),
        out_shape=jax.ShapeDtypeStruct((s, n), F32),
        scratch_shapes=[pltpu.VMEM((k, n), BF16)],
        compiler_params=pltpu.CompilerParams(
            dimension_semantics=("arbitrary",),
            vmem_limit_bytes=V7X_VMEM_LIMIT_BYTES),
        name="out_proj",
    )(mixed, w, x)


def _ffn_kernel(x_ref, g2_ref, wg_ref, wu_ref, wd_ref, gf_ref, o_ref, h_ref, *, final_norm):
    f = pl.program_id(1)

    @pl.when(f == 0)
    def _():
        x = x_ref[...]
        h_ref[...] = _rms_rows(x, g2_ref[...]).astype(BF16)
        o_ref[...] = x

    h = h_ref[...]
    gate = jnp.dot(h, wg_ref[...].astype(BF16), preferred_element_type=F32)
    up = jnp.dot(h, wu_ref[...].astype(BF16), preferred_element_type=F32)
    a = (gate * jax.nn.sigmoid(gate) * up).astype(BF16)
    o_ref[...] += jnp.dot(a, wd_ref[...].astype(BF16), preferred_element_type=F32)

    if final_norm:
        @pl.when(f == pl.num_programs(1) - 1)
        def _():
            o_ref[...] = _rms_rows(o_ref[...], gf_ref[...])


def _ffn(x1, g2, wg, wu, wd, gf, *, tm, tf, final_norm):
    s, d = x1.shape
    dff = wg.shape[1]
    return pl.pallas_call(
        functools.partial(_ffn_kernel, final_norm=final_norm),
        grid=(s // tm, dff // tf),
        in_specs=[
            pl.BlockSpec((tm, d), lambda i, f: (i, 0), pipeline_mode=pl.Buffered(1)),
            pl.BlockSpec((1, d), lambda i, f: (0, 0)),
            pl.BlockSpec((d, tf), lambda i, f: (0, f)),
            pl.BlockSpec((d, tf), lambda i, f: (0, f)),
            pl.BlockSpec((tf, d), lambda i, f: (f, 0)),
            pl.BlockSpec((1, d), lambda i, f: (0, 0)),
        ],
        out_specs=pl.BlockSpec((tm, d), lambda i, f: (i, 0)),
        out_shape=jax.ShapeDtypeStruct((s, d), F32),
        scratch_shapes=[pltpu.VMEM((tm, d), BF16)],
        compiler_params=pltpu.CompilerParams(
            dimension_semantics=("parallel", "arbitrary"),
            vmem_limit_bytes=V7X_VMEM_LIMIT_BYTES),
        name="ffn",
    )(x1, g2, wg, wu, wd, gf)


def kernel(x, norm1_g, w_in, sgu_ln_g, sgu_ln_b, w_spatial, b_spatial, ret_gn_g, ret_gn_b,
           w_out, norm2_g, w_gate, w_up, w_down, final_norm_g):
    batch, seq, d = x.shape
    depth = w_in.shape[0]
    outs = []
    for b in range(batch):
        xb = x[b]
        for l in range(depth):
            proj = _in_proj(xb, norm1_g[l][None, :], w_in[l], tm=1024, tn=1024)
            mixed = _mixer(proj, sgu_ln_g[l], sgu_ln_b[l], w_spatial[l], b_spatial[l],
                           ret_gn_g[l], ret_gn_b[l], chunks_per_step=2)
            x1 = _out_proj(mixed, w_out[l], xb, tm=512)
            xb = _ffn(x1, norm2_g[l][None, :], w_gate[l], w_up[l], w_down[l], final_norm_g[None, :],
                      tm=1024, tf=256, final_norm=(l == depth - 1))
        outs.append(xb)
    return outs[0][None] if batch == 1 else jnp.stack(outs)
```

```python
import functools

import numpy as np

import jax
import jax.numpy as jnp
from jax import lax
from jax.experimental import pallas as pl
from jax.experimental.pallas import tpu as pltpu

D_MODEL = 2048
CHUNK = 128
HEADS = 8
HEAD_DIM = 128
SGU_WIDTH = HEADS * HEAD_DIM
RET_WIDTH = HEADS * HEAD_DIM
MIX_WIDTH = SGU_WIDTH + RET_WIDTH
IN_WIDTH = 2 * SGU_WIDTH + 4 * RET_WIDTH
ROPE_BASE = 10000.0
EPS = 1e-6

OFF_U, OFF_VS, OFF_Q, OFF_K, OFF_VR, OFF_G = (i * SGU_WIDTH for i in range(6))

V7X_VMEM_LIMIT_BYTES = 60 * 1024 * 1024

F32 = jnp.float32
BF16 = jnp.bfloat16


def _rms_rows(x, g):
    ms = jnp.mean(x * x, axis=-1, keepdims=True)
    return x * lax.rsqrt(ms + EPS) * g


def _in_proj_kernel(x_ref, g_ref, w_ref, o_ref, h_ref):
    @pl.when(pl.program_id(1) == 0)
    def _():
        h_ref[...] = _rms_rows(x_ref[...], g_ref[...]).astype(BF16)

    o_ref[...] = jnp.dot(h_ref[...], w_ref[...].astype(BF16),
                         preferred_element_type=F32).astype(o_ref.dtype)


def _in_proj(x, g, w, *, tm, tn):
    s, d = x.shape
    n = w.shape[1]
    return pl.pallas_call(
        _in_proj_kernel,
        grid=(s // tm, n // tn),
        in_specs=[
            pl.BlockSpec((tm, d), lambda i, j: (i, 0)),
            pl.BlockSpec((1, d), lambda i, j: (0, 0)),
            pl.BlockSpec((d, tn), lambda i, j: (0, j)),
        ],
        out_specs=pl.BlockSpec((tm, tn), lambda i, j: (i, j)),
        out_shape=jax.ShapeDtypeStruct((s, n), BF16),
        scratch_shapes=[pltpu.VMEM((tm, d), BF16)],
        compiler_params=pltpu.CompilerParams(
            dimension_semantics=("parallel", "arbitrary"),
            vmem_limit_bytes=V7X_VMEM_LIMIT_BYTES),
        name="in_proj",
    )(x, g, w)


def _split_bf16(x):
    hi = x.astype(BF16)
    lo = (x - hi.astype(F32)).astype(BF16)
    return jnp.concatenate([hi, lo], axis=1)


def _head_rows(stacked, h):
    return stacked[h * CHUNK:(h + 1) * CHUNK]


def _row_means(tiles, ones2):
    return jnp.dot(jnp.concatenate([_split_bf16(t) for t in tiles], axis=0), ones2,
                   preferred_element_type=F32)


def _mixer_kernel(p_ref, ca_ref, sa_ref, cb_ref, sb_ref, cbs_ref, sbs_ref,
                  lng_ref, lnb_ref, ws_ref, bs_ref, gng_ref, gnb_ref,
                  decay_ref, zeta_ref, xi_ref, wout_ref, x_ref, wg_ref, wu_ref, wd_ref,
                  x1_ref, wgb_ref, wub_ref, wdb_ref,
                  state_ref, wc_ref, bias_ref, woutb_ref, o_ref, prev_ref, *, chunks_per_step, gamma_c):
    step = pl.program_id(0)
    wgb_ref[...] = wg_ref[...].astype(BF16)
    wub_ref[...] = wu_ref[...].astype(BF16)
    wdb_ref[...] = wd_ref[...].astype(BF16)
    row_id = lax.broadcasted_iota(jnp.int32, (CHUNK, CHUNK), 0)
    col_id = lax.broadcasted_iota(jnp.int32, (CHUNK, CHUNK), 1)

    @pl.when(step == 0)
    def _():
        state_ref[...] = jnp.zeros_like(state_ref)
        woutb_ref[...] = wout_ref[...].astype(BF16)
        o_ref[...] = jnp.zeros_like(o_ref)
        for h in range(HEADS):
            wc_ref[h] = jnp.where(row_id >= col_id, ws_ref[h], 0.0).astype(BF16)
            b_col = jnp.sum(jnp.where(row_id == col_id, bs_ref[h:h + 1, :], 0.0), axis=1, keepdims=True)
            bias_ref[h] = jnp.broadcast_to(b_col, (CHUNK, HEAD_DIM))

    inv_d = 1.0 / HEAD_DIM
    ones1 = jnp.full((HEAD_DIM, HEAD_DIM), inv_d, BF16)
    ones2 = jnp.full((2 * HEAD_DIM, HEAD_DIM), inv_d, BF16)
    k_scale = HEAD_DIM ** -0.5
    chunk0 = jnp.minimum(step, pl.num_programs(0) - 2) * chunks_per_step
    heads = range(HEADS)

    prev_ref[...] = o_ref[...]
    d_model = x1_ref.shape[1]
    piece_cols = d_model // HEADS

    def project_piece(c, h):
        r = slice(c * CHUNK, (c + 1) * CHUNK)
        n = slice(h * piece_cols, (h + 1) * piece_cols)
        x1_ref[r, n] = x_ref[r, n] + jnp.dot(prev_ref[r, :], woutb_ref[:, n], preferred_element_type=F32)

    def cols(off, h):
        return slice(off + h * HEAD_DIM, off + (h + 1) * HEAD_DIM)

    for c in range(chunks_per_step):
        rows = slice(c * CHUNK, (c + 1) * CHUNK)

        ca = ca_ref[pl.ds(chunk0 + c, 1), :]
        sa = sa_ref[pl.ds(chunk0 + c, 1), :]
        cos2 = ca * cb_ref[...] - sa * sb_ref[...]
        sin2 = sa * cbs_ref[...] + ca * sbs_ref[...]
        cos2k = cos2 * k_scale
        sin2k = sin2 * k_scale

        vs_b = [p_ref[rows, cols(OFF_VS, h)] for h in heads]
        mu = jnp.dot(jnp.concatenate(vs_b, axis=0), ones1, preferred_element_type=F32)
        dv = [vs_b[h].astype(F32) - _head_rows(mu, h) for h in heads]
        var = _row_means([d * d for d in dv], ones2)
        for h in heads:
            vn = dv[h] * lax.rsqrt(_head_rows(var, h) + EPS) * lng_ref[h:h + 1, :] + lnb_ref[h:h + 1, :]
            mixed = jnp.dot(wc_ref[h], vn.astype(BF16), preferred_element_type=F32) + bias_ref[h]
            u = p_ref[rows, cols(OFF_U, h)].astype(F32)
            o_ref[rows, cols(0, h)] = (u * mixed).astype(o_ref.dtype)

        outs = []
        for h in heads:
            q = p_ref[rows, cols(OFF_Q, h)].astype(F32)
            k = p_ref[rows, cols(OFF_K, h)].astype(F32)
            vr = p_ref[rows, cols(OFF_VR, h)]
            qb = (q * cos2 + pltpu.roll(q, HEAD_DIM // 2, axis=1) * sin2).astype(BF16)
            kr = k * cos2k + pltpu.roll(k, HEAD_DIM // 2, axis=1) * sin2k
            scores = lax.dot_general(qb, kr.astype(BF16), (((1,), (1,)), ((), ())),
                                     preferred_element_type=F32) * decay_ref[h]
            intra = jnp.dot(scores.astype(BF16), vr, preferred_element_type=F32)
            state = state_ref[h]
            inter = jnp.dot(qb, state.astype(BF16), preferred_element_type=F32) * xi_ref[h]
            kz = (kr * zeta_ref[h]).astype(BF16)
            kv = lax.dot_general(kz, vr, (((0,), (0,)), ((), ())), preferred_element_type=F32)
            state_ref[h] = gamma_c[h] * state + kv
            outs.append(intra + inter)
            project_piece(c, h)

        mu = _row_means(outs, ones2)
        do = [outs[h] - _head_rows(mu, h) for h in heads]
        var = _row_means([d * d for d in do], ones2)
        for h in heads:
            on = do[h] * lax.rsqrt(_head_rows(var, h) + EPS) * gng_ref[h:h + 1, :] + gnb_ref[h:h + 1, :]
            g = p_ref[rows, cols(OFF_G, h)].astype(F32)
            o_ref[rows, cols(SGU_WIDTH, h)] = (g * jax.nn.sigmoid(g) * on).astype(o_ref.dtype)


def _mixer_constants(seq):
    half = HEAD_DIM // 2
    inv = 1.0 / (ROPE_BASE ** (np.arange(half, dtype=np.float64) / half))
    inv2 = np.concatenate([inv, inv])
    sign = np.concatenate([-np.ones(half), np.ones(half)])
    ang_a = (CHUNK * np.arange(seq // CHUNK, dtype=np.float64))[:, None] * inv2[None, :]
    ang_b = np.arange(CHUNK, dtype=np.float64)[:, None] * inv2[None, :]
    rot = [np.cos(ang_a), np.sin(ang_a), np.cos(ang_b), np.sin(ang_b),
           sign * np.cos(ang_b), sign * np.sin(ang_b)]

    log_gamma = np.log(1.0 - np.exp2(-5.0 - np.arange(HEADS, dtype=np.float64)))
    idx = np.arange(CHUNK, dtype=np.float64)
    diff = idx[:, None] - idx[None, :]
    decay = np.where(diff[None] >= 0, np.exp(np.maximum(diff, 0.0)[None] * log_gamma[:, None, None]), 0.0)
    zeta = np.exp((CHUNK - 1.0 - idx)[None, :] * log_gamma[:, None])
    xi = np.exp((idx + 1.0)[None, :] * log_gamma[:, None])
    bcast = lambda v: np.broadcast_to(v[:, :, None], (HEADS, CHUNK, HEAD_DIM))
    gamma_c = tuple(float(np.float32(v)) for v in np.exp(CHUNK * log_gamma))
    tables = [jnp.asarray(np.ascontiguousarray(t), dtype=F32) for t in rot + [decay, bcast(zeta), bcast(xi)]]
    return tables, gamma_c


def _mixer_out_proj(proj, lng, lnb, ws, bs, gng, gnb, w_out, x, w_gate, w_up, w_down, *, chunks_per_step):
    s, d = x.shape
    dff = w_gate.shape[1]
    n_chunks = s // CHUNK
    rows = chunks_per_step * CHUNK
    n_tiles = s // rows
    gate_rows, down_rows = d // n_tiles, dff // n_tiles
    cast_block = lambda i: (jnp.minimum(i, n_tiles - 1), 0)
    (ca, sa, cb, sb, cbs, sbs, decay, zeta, xi), gamma_c = _mixer_constants(s)
    full = lambda shape, **kw: pl.BlockSpec(shape, lambda i: (0,) * len(shape), **kw)
    head_tiles = full((HEADS, CHUNK, HEAD_DIM))
    head_rows = full((HEADS, HEAD_DIM))
    mixed_tile = lambda i: (jnp.minimum(i, n_tiles - 1), 0)
    projected_tile = lambda i: (jnp.maximum(i - 1, 0), 0)
    return pl.pallas_call(
        functools.partial(_mixer_kernel, chunks_per_step=chunks_per_step, gamma_c=gamma_c),
        grid=(n_tiles + 1,),
        in_specs=[
            pl.BlockSpec((rows, IN_WIDTH), mixed_tile),
            full((n_chunks, HEAD_DIM)), full((n_chunks, HEAD_DIM)),
            full((CHUNK, HEAD_DIM)), full((CHUNK, HEAD_DIM)), full((CHUNK, HEAD_DIM)), full((CHUNK, HEAD_DIM)),
            head_rows, head_rows, head_tiles, head_rows, head_rows, head_rows,
            head_tiles, head_tiles, head_tiles,
            full((MIX_WIDTH, d), pipeline_mode=pl.Buffered(1)),
            pl.BlockSpec((rows, d), projected_tile),
            pl.BlockSpec((gate_rows, dff), cast_block),
            pl.BlockSpec((gate_rows, dff), cast_block),
            pl.BlockSpec((down_rows, d), cast_block),
        ],
        out_specs=[pl.BlockSpec((rows, d), projected_tile),
                   pl.BlockSpec((gate_rows, dff), cast_block),
                   pl.BlockSpec((gate_rows, dff), cast_block),
                   pl.BlockSpec((down_rows, d), cast_block)],
        out_shape=[jax.ShapeDtypeStruct((s, d), F32),
                   jax.ShapeDtypeStruct(w_gate.shape, BF16),
                   jax.ShapeDtypeStruct(w_up.shape, BF16),
                   jax.ShapeDtypeStruct(w_down.shape, BF16)],
        scratch_shapes=[pltpu.VMEM((HEADS, HEAD_DIM, HEAD_DIM), F32),
                        pltpu.VMEM((HEADS, CHUNK, CHUNK), BF16),
                        pltpu.VMEM((HEADS, CHUNK, HEAD_DIM), F32),
                        pltpu.VMEM((MIX_WIDTH, d), BF16),
                        pltpu.VMEM((rows, MIX_WIDTH), BF16),
                        pltpu.VMEM((rows, MIX_WIDTH), BF16)],
        compiler_params=pltpu.CompilerParams(
            dimension_semantics=("arbitrary",),
            vmem_limit_bytes=V7X_VMEM_LIMIT_BYTES),
        name="mixer_out_proj",
    )(proj, ca, sa, cb, sb, cbs, sbs, lng, lnb, ws, bs, gng, gnb, decay, zeta, xi, w_out, x,
      w_gate, w_up, w_down)


def _ffn_kernel(x_hbm, g2_ref, wg_ref, wu_ref, wd_ref, gf_ref, o_ref, h_ref, x_ref, x_sem, *, final_norm):
    i, f = pl.program_id(0), pl.program_id(1)
    tm = x_ref.shape[0]

    def x_copy(tile):
        start = pl.multiple_of(tile * tm, tm)
        return pltpu.make_async_copy(x_hbm.at[pl.ds(start, tm), :], x_ref, x_sem)

    @pl.when((i == 0) & (f == 0))
    def _():
        x_copy(0).start()

    @pl.when(f == 0)
    def _():
        x_copy(i).wait()
        x = x_ref[...]
        h_ref[...] = _rms_rows(x, g2_ref[...]).astype(BF16)
        o_ref[...] = x

    @pl.when((f == 1) & (i + 1 < pl.num_programs(0)))
    def _():
        x_copy(i + 1).start()

    h = h_ref[...]
    gate = jnp.dot(h, wg_ref[...], preferred_element_type=F32)
    up = jnp.dot(h, wu_ref[...], preferred_element_type=F32)
    a = (gate * jax.nn.sigmoid(gate) * up).astype(BF16)
    o_ref[...] += jnp.dot(a, wd_ref[...], preferred_element_type=F32)

    if final_norm:
        @pl.when(f == pl.num_programs(1) - 1)
        def _():
            o_ref[...] = _rms_rows(o_ref[...], gf_ref[...])


def _ffn(x1, g2, wg, wu, wd, gf, *, tm, tf, final_norm):
    s, d = x1.shape
    dff = wg.shape[1]
    return pl.pallas_call(
        functools.partial(_ffn_kernel, final_norm=final_norm),
        grid=(s // tm, dff // tf),
        in_specs=[
            pl.BlockSpec(memory_space=pl.ANY),
            pl.BlockSpec((1, d), lambda i, f: (0, 0)),
            pl.BlockSpec((d, tf), lambda i, f: (0, f)),
            pl.BlockSpec((d, tf), lambda i, f: (0, f)),
            pl.BlockSpec((tf, d), lambda i, f: (f, 0)),
            pl.BlockSpec((1, d), lambda i, f: (0, 0)),
        ],
        out_specs=pl.BlockSpec((tm, d), lambda i, f: (i, 0)),
        out_shape=jax.ShapeDtypeStruct((s, d), F32),
        scratch_shapes=[pltpu.VMEM((tm, d), BF16), pltpu.VMEM((tm, d), F32), pltpu.SemaphoreType.DMA(())],
        compiler_params=pltpu.CompilerParams(
            dimension_semantics=("arbitrary", "arbitrary"),
            vmem_limit_bytes=V7X_VMEM_LIMIT_BYTES),
        name="ffn",
    )(x1, g2, wg, wu, wd, gf)


def kernel(x, norm1_g, w_in, sgu_ln_g, sgu_ln_b, w_spatial, b_spatial, ret_gn_g, ret_gn_b,
           w_out, norm2_g, w_gate, w_up, w_down, final_norm_g):
    batch, seq, d = x.shape
    depth = w_in.shape[0]
    outs = []
    for b in range(batch):
        xb = x[b]
        for l in range(depth):
            proj = _in_proj(xb, norm1_g[l][None, :], w_in[l], tm=1024, tn=1024)
            x1, wg, wu, wd = _mixer_out_proj(proj, sgu_ln_g[l], sgu_ln_b[l], w_spatial[l], b_spatial[l],
                                             ret_gn_g[l], ret_gn_b[l], w_out[l], xb,
                                             w_gate[l], w_up[l], w_down[l], chunks_per_step=2)
            xb = _ffn(x1, norm2_g[l][None, :], wg, wu, wd, final_norm_g[None, :],
                      tm=1024, tf=512, final_norm=(l == depth - 1))
        outs.append(xb)
    return outs[0][None] if batch == 1 else jnp.stack(outs)
```

```python
import functools

import numpy as np

import jax
import jax.numpy as jnp
from jax import lax
from jax.experimental import pallas as pl
from jax.experimental.pallas import tpu as pltpu

D_MODEL = 2048
CHUNK = 128
HEADS = 8
HEAD_DIM = 128
SGU_WIDTH = HEADS * HEAD_DIM
RET_WIDTH = HEADS * HEAD_DIM
MIX_WIDTH = SGU_WIDTH + RET_WIDTH
IN_WIDTH = 2 * SGU_WIDTH + 4 * RET_WIDTH
ROPE_BASE = 10000.0
EPS = 1e-6

OFF_U, OFF_VS, OFF_Q, OFF_K, OFF_VR, OFF_G = (i * SGU_WIDTH for i in range(6))

V7X_VMEM_LIMIT_BYTES = 60 * 1024 * 1024

N_PROJ_PIECES = 8
W_OUT_STAGE_ROWS = 256

F32 = jnp.float32
BF16 = jnp.bfloat16


def _rms_rows(x, g):
    ms = jnp.mean(x * x, axis=-1, keepdims=True)
    return x * lax.rsqrt(ms + EPS) * g


def _in_proj_kernel(x_ref, g_ref, w_ref, o_ref, h_ref):
    @pl.when(pl.program_id(1) == 0)
    def _():
        h_ref[...] = _rms_rows(x_ref[...], g_ref[...]).astype(BF16)

    o_ref[...] = jnp.dot(h_ref[...], w_ref[...].astype(BF16),
                         preferred_element_type=F32).astype(o_ref.dtype)


def _in_proj(x, g, w, *, tm, tn):
    s, d = x.shape
    n = w.shape[1]
    return pl.pallas_call(
        _in_proj_kernel,
        grid=(s // tm, n // tn),
        in_specs=[
            pl.BlockSpec((tm, d), lambda i, j: (i, 0)),
            pl.BlockSpec((1, d), lambda i, j: (0, 0)),
            pl.BlockSpec((d, tn), lambda i, j: (0, j)),
        ],
        out_specs=pl.BlockSpec((tm, tn), lambda i, j: (i, j)),
        out_shape=jax.ShapeDtypeStruct((s, n), BF16),
        scratch_shapes=[pltpu.VMEM((tm, d), BF16)],
        compiler_params=pltpu.CompilerParams(
            dimension_semantics=("parallel", "arbitrary"),
            vmem_limit_bytes=V7X_VMEM_LIMIT_BYTES),
        name="in_proj",
    )(x, g, w)


def _norm_rows(x, g, b):
    mu = jnp.mean(x, axis=-1, keepdims=True)
    d = x - mu
    var = jnp.mean(d * d, axis=-1, keepdims=True)
    return d * lax.rsqrt(var + EPS) * g + b


def _mixer_kernel(p_ref, ca_ref, sa_ref, cb_ref, sb_ref, cbs_ref, sbs_ref,
                  lng_ref, lnb_ref, ws_ref, bs_ref, gng_ref, gnb_ref,
                  decay_ref, zeta_ref, xi_ref, wout_hbm, x_ref, wg_ref, wu_ref, wd_ref,
                  x1_ref, wgb_ref, wub_ref, wdb_ref,
                  state_ref, wc_ref, bias_ref, woutb_ref, o_ref, prev_ref, stage_ref, stage_sem,
                  *, chunks_per_step, gamma_c):
    step = pl.program_id(0)
    wgb_ref[...] = wg_ref[...].astype(BF16)
    wub_ref[...] = wu_ref[...].astype(BF16)
    wdb_ref[...] = wd_ref[...].astype(BF16)
    row_id = lax.broadcasted_iota(jnp.int32, (CHUNK, CHUNK), 0)
    col_id = lax.broadcasted_iota(jnp.int32, (CHUNK, CHUNK), 1)

    @pl.when(step == 0)
    def _():
        state_ref[...] = jnp.zeros_like(state_ref)
        o_ref[...] = jnp.zeros_like(o_ref)
        stage_rows = stage_ref.shape[1]
        n_stage = woutb_ref.shape[0] // stage_rows

        def stage_copy(r):
            return pltpu.make_async_copy(wout_hbm.at[pl.ds(r * stage_rows, stage_rows), :],
                                         stage_ref.at[r % 2], stage_sem.at[r % 2])

        stage_copy(0).start()
        for r in range(n_stage):
            if r + 1 < n_stage:
                stage_copy(r + 1).start()
            stage_copy(r).wait()
            woutb_ref[r * stage_rows:(r + 1) * stage_rows, :] = stage_ref[r % 2].astype(BF16)
        for h in range(HEADS):
            wc_ref[h] = jnp.where(row_id >= col_id, ws_ref[h], 0.0).astype(BF16)
            b_col = jnp.sum(jnp.where(row_id == col_id, bs_ref[h:h + 1, :], 0.0), axis=1, keepdims=True)
            bias_ref[h] = jnp.broadcast_to(b_col, (CHUNK, HEAD_DIM))

    k_scale = HEAD_DIM ** -0.5
    chunk0 = jnp.minimum(step, pl.num_programs(0) - 2) * chunks_per_step
    heads = range(HEADS)

    prev_ref[...] = o_ref[...]
    piece_cols = x1_ref.shape[1] // N_PROJ_PIECES
    heads_per_piece = chunks_per_step * HEADS // N_PROJ_PIECES

    def project_piece(c, h):
        done = c * HEADS + h + 1
        if done % heads_per_piece == 0:
            n = slice((done // heads_per_piece - 1) * piece_cols, (done // heads_per_piece) * piece_cols)
            x1_ref[:, n] = x_ref[:, n] + jnp.dot(prev_ref[...], woutb_ref[:, n], preferred_element_type=F32)

    def cols(off, h):
        return slice(off + h * HEAD_DIM, off + (h + 1) * HEAD_DIM)

    for c in range(chunks_per_step):
        rows = slice(c * CHUNK, (c + 1) * CHUNK)

        ca = ca_ref[pl.ds(chunk0 + c, 1), :]
        sa = sa_ref[pl.ds(chunk0 + c, 1), :]
        cos2 = ca * cb_ref[...] - sa * sb_ref[...]
        sin2 = sa * cbs_ref[...] + ca * sbs_ref[...]
        cos2k = cos2 * k_scale
        sin2k = sin2 * k_scale

        for h in heads:
            vn = _norm_rows(p_ref[rows, cols(OFF_VS, h)].astype(F32), lng_ref[h:h + 1, :], lnb_ref[h:h + 1, :])
            mixed = jnp.dot(wc_ref[h], vn.astype(BF16), preferred_element_type=F32) + bias_ref[h]
            u = p_ref[rows, cols(OFF_U, h)].astype(F32)
            o_ref[rows, cols(0, h)] = (u * mixed).astype(o_ref.dtype)

        outs = []
        for h in heads:
            q = p_ref[rows, cols(OFF_Q, h)].astype(F32)
            k = p_ref[rows, cols(OFF_K, h)].astype(F32)
            vr = p_ref[rows, cols(OFF_VR, h)]
            qb = (q * cos2 + pltpu.roll(q, HEAD_DIM // 2, axis=1) * sin2).astype(BF16)
            kr = k * cos2k + pltpu.roll(k, HEAD_DIM // 2, axis=1) * sin2k
            scores = lax.dot_general(qb, kr.astype(BF16), (((1,), (1,)), ((), ())),
                                     preferred_element_type=F32) * decay_ref[h]
            intra = jnp.dot(scores.astype(BF16), vr, preferred_element_type=F32)
            state = state_ref[h]
            inter = jnp.dot(qb, state.astype(BF16), preferred_element_type=F32) * xi_ref[h]
            kz = (kr * zeta_ref[h]).astype(BF16)
            kv = lax.dot_general(kz, vr, (((0,), (0,)), ((), ())), preferred_element_type=F32)
            state_ref[h] = gamma_c[h] * state + kv
            outs.append(intra + inter)
            project_piece(c, h)

        for h in heads:
            on = _norm_rows(outs[h], gng_ref[h:h + 1, :], gnb_ref[h:h + 1, :])
            g = p_ref[rows, cols(OFF_G, h)].astype(F32)
            o_ref[rows, cols(SGU_WIDTH, h)] = (g * jax.nn.sigmoid(g) * on).astype(o_ref.dtype)


def _mixer_constants(seq):
    half = HEAD_DIM // 2
    inv = 1.0 / (ROPE_BASE ** (np.arange(half, dtype=np.float64) / half))
    inv2 = np.concatenate([inv, inv])
    sign = np.concatenate([-np.ones(half), np.ones(half)])
    ang_a = (CHUNK * np.arange(seq // CHUNK, dtype=np.float64))[:, None] * inv2[None, :]
    ang_b = np.arange(CHUNK, dtype=np.float64)[:, None] * inv2[None, :]
    rot = [np.cos(ang_a), np.sin(ang_a), np.cos(ang_b), np.sin(ang_b),
           sign * np.cos(ang_b), sign * np.sin(ang_b)]

    log_gamma = np.log(1.0 - np.exp2(-5.0 - np.arange(HEADS, dtype=np.float64)))
    idx = np.arange(CHUNK, dtype=np.float64)
    diff = idx[:, None] - idx[None, :]
    decay = np.where(diff[None] >= 0, np.exp(np.maximum(diff, 0.0)[None] * log_gamma[:, None, None]), 0.0)
    zeta = np.exp((CHUNK - 1.0 - idx)[None, :] * log_gamma[:, None])
    xi = np.exp((idx + 1.0)[None, :] * log_gamma[:, None])
    bcast = lambda v: np.broadcast_to(v[:, :, None], (HEADS, CHUNK, HEAD_DIM))
    gamma_c = tuple(float(np.float32(v)) for v in np.exp(CHUNK * log_gamma))
    tables = [jnp.asarray(np.ascontiguousarray(t), dtype=F32) for t in rot + [decay, bcast(zeta), bcast(xi)]]
    return tables, gamma_c


def _mixer_out_proj(proj, lng, lnb, ws, bs, gng, gnb, w_out, x, w_gate, w_up, w_down, *, chunks_per_step):
    s, d = x.shape
    dff = w_gate.shape[1]
    n_chunks = s // CHUNK
    rows = chunks_per_step * CHUNK
    n_tiles = s // rows
    gate_rows, down_rows = d // n_tiles, dff // n_tiles
    cast_block = lambda i: (jnp.minimum(i, n_tiles - 1), 0)
    (ca, sa, cb, sb, cbs, sbs, decay, zeta, xi), gamma_c = _mixer_constants(s)
    full = lambda shape, **kw: pl.BlockSpec(shape, lambda i: (0,) * len(shape), **kw)
    head_tiles = full((HEADS, CHUNK, HEAD_DIM))
    head_rows = full((HEADS, HEAD_DIM))
    mixed_tile = lambda i: (jnp.minimum(i, n_tiles - 1), 0)
    projected_tile = lambda i: (jnp.maximum(i - 1, 0), 0)
    return pl.pallas_call(
        functools.partial(_mixer_kernel, chunks_per_step=chunks_per_step, gamma_c=gamma_c),
        grid=(n_tiles + 1,),
        in_specs=[
            pl.BlockSpec((rows, IN_WIDTH), mixed_tile),
            full((n_chunks, HEAD_DIM)), full((n_chunks, HEAD_DIM)),
            full((CHUNK, HEAD_DIM)), full((CHUNK, HEAD_DIM)), full((CHUNK, HEAD_DIM)), full((CHUNK, HEAD_DIM)),
            head_rows, head_rows, head_tiles, head_rows, head_rows, head_rows,
            head_tiles, head_tiles, head_tiles,
            pl.BlockSpec(memory_space=pl.ANY),
            pl.BlockSpec((rows, d), projected_tile),
            pl.BlockSpec((gate_rows, dff), cast_block),
            pl.BlockSpec((gate_rows, dff), cast_block),
            pl.BlockSpec((down_rows, d), cast_block),
        ],
        out_specs=[pl.BlockSpec((rows, d), projected_tile),
                   pl.BlockSpec((gate_rows, dff), cast_block),
                   pl.BlockSpec((gate_rows, dff), cast_block),
                   pl.BlockSpec((down_rows, d), cast_block)],
        out_shape=[jax.ShapeDtypeStruct((s, d), F32),
                   jax.ShapeDtypeStruct(w_gate.shape, BF16),
                   jax.ShapeDtypeStruct(w_up.shape, BF16),
                   jax.ShapeDtypeStruct(w_down.shape, BF16)],
        scratch_shapes=[pltpu.VMEM((HEADS, HEAD_DIM, HEAD_DIM), F32),
                        pltpu.VMEM((HEADS, CHUNK, CHUNK), BF16),
                        pltpu.VMEM((HEADS, CHUNK, HEAD_DIM), F32),
                        pltpu.VMEM((MIX_WIDTH, d), BF16),
                        pltpu.VMEM((rows, MIX_WIDTH), BF16),
                        pltpu.VMEM((rows, MIX_WIDTH), BF16),
                        pltpu.VMEM((2, W_OUT_STAGE_ROWS, d), F32),
                        pltpu.SemaphoreType.DMA((2,))],
        compiler_params=pltpu.CompilerParams(
            dimension_semantics=("arbitrary",),
            vmem_limit_bytes=V7X_VMEM_LIMIT_BYTES),
        name="mixer_out_proj",
    )(proj, ca, sa, cb, sb, cbs, sbs, lng, lnb, ws, bs, gng, gnb, decay, zeta, xi, w_out, x,
      w_gate, w_up, w_down)


def _ffn_kernel(x_hbm, g2_ref, wg_ref, wu_ref, wd_ref, gf_ref, o_ref, h_ref, x_ref, x_sem, *, final_norm):
    i, f = pl.program_id(0), pl.program_id(1)
    tm = x_ref.shape[0]

    def x_copy(tile):
        start = pl.multiple_of(tile * tm, tm)
        return pltpu.make_async_copy(x_hbm.at[pl.ds(start, tm), :], x_ref, x_sem)

    @pl.when((i == 0) & (f == 0))
    def _():
        x_copy(0).start()

    @pl.when(f == 0)
    def _():
        x_copy(i).wait()
        x = x_ref[...]
        h_ref[...] = _rms_rows(x, g2_ref[...]).astype(BF16)
        o_ref[...] = x

    @pl.when((f == 1) & (i + 1 < pl.num_programs(0)))
    def _():
        x_copy(i + 1).start()

    h = h_ref[...]
    gate = jnp.dot(h, wg_ref[...], preferred_element_type=F32)
    up = jnp.dot(h, wu_ref[...], preferred_element_type=F32)
    a = (gate * jax.nn.sigmoid(gate) * up).astype(BF16)
    o_ref[...] += jnp.dot(a, wd_ref[...], preferred_element_type=F32)

    if final_norm:
        @pl.when(f == pl.num_programs(1) - 1)
        def _():
            o_ref[...] = _rms_rows(o_ref[...], gf_ref[...])


def _ffn(x1, g2, wg, wu, wd, gf, *, tm, tf, final_norm):
    s, d = x1.shape
    dff = wg.shape[1]
    return pl.pallas_call(
        functools.partial(_ffn_kernel, final_norm=final_norm),
        grid=(s // tm, dff // tf),
        in_specs=[
            pl.BlockSpec(memory_space=pl.ANY),
            pl.BlockSpec((1, d), lambda i, f: (0, 0)),
            pl.BlockSpec((d, tf), lambda i, f: (0, f)),
            pl.BlockSpec((d, tf), lambda i, f: (0, f)),
            pl.BlockSpec((tf, d), lambda i, f: (f, 0)),
            pl.BlockSpec((1, d), lambda i, f: (0, 0)),
        ],
        out_specs=pl.BlockSpec((tm, d), lambda i, f: (i, 0)),
        out_shape=jax.ShapeDtypeStruct((s, d), F32),
        scratch_shapes=[pltpu.VMEM((tm, d), BF16), pltpu.VMEM((tm, d), F32), pltpu.SemaphoreType.DMA(())],
        compiler_params=pltpu.CompilerParams(
            dimension_semantics=("arbitrary", "arbitrary"),
            vmem_limit_bytes=V7X_VMEM_LIMIT_BYTES),
        name="ffn",
    )(x1, g2, wg, wu, wd, gf)


def kernel(x, norm1_g, w_in, sgu_ln_g, sgu_ln_b, w_spatial, b_spatial, ret_gn_g, ret_gn_b,
           w_out, norm2_g, w_gate, w_up, w_down, final_norm_g):
    batch, seq, d = x.shape
    depth = w_in.shape[0]
    outs = []
    for b in range(batch):
        xb = x[b]
        for l in range(depth):
            proj = _in_proj(xb, norm1_g[l][None, :], w_in[l], tm=1024, tn=1024)
            x1, wg, wu, wd = _mixer_out_proj(proj, sgu_ln_g[l], sgu_ln_b[l], w_spatial[l], b_spatial[l],
                                             ret_gn_g[l], ret_gn_b[l], w_out[l], xb,
                                             w_gate[l], w_up[l], w_down[l], chunks_per_step=2)
            xb = _ffn(x1, norm2_g[l][None, :], wg, wu, wd, final_norm_g[None, :],
                      tm=1024, tf=512, final_norm=(l == depth - 1))
        outs.append(xb)
    return outs[0][None] if batch == 1 else jnp.stack(outs)
```

```python
import functools

import numpy as np

import jax
import jax.numpy as jnp
from jax import lax
from jax.experimental import pallas as pl
from jax.experimental.pallas import tpu as pltpu

D_MODEL = 2048
CHUNK = 128
HEADS = 8
HEAD_DIM = 128
SGU_WIDTH = HEADS * HEAD_DIM
RET_WIDTH = HEADS * HEAD_DIM
MIX_WIDTH = SGU_WIDTH + RET_WIDTH
IN_WIDTH = 2 * SGU_WIDTH + 4 * RET_WIDTH
ROPE_BASE = 10000.0
EPS = 1e-6

OFF_U, OFF_VS, OFF_Q, OFF_K, OFF_VR, OFF_G = (i * SGU_WIDTH for i in range(6))

V7X_VMEM_LIMIT_BYTES = 60 * 1024 * 1024

N_PROJ_PIECES = 8
W_OUT_STAGE_ROWS = 256

F32 = jnp.float32
BF16 = jnp.bfloat16


def _rms_rows(x, g):
    ms = jnp.mean(x * x, axis=-1, keepdims=True)
    return x * lax.rsqrt(ms + EPS) * g


def _row_tile_copy(x_hbm, x_ref, sem, tile):
    tm = x_ref.shape[0]
    start = pl.multiple_of(tile * tm, tm)
    return pltpu.make_async_copy(x_hbm.at[pl.ds(start, tm), :], x_ref, sem)


def _prefetched_row_tile(x_hbm, x_ref, sem, consume):
    i, j = pl.program_id(0), pl.program_id(1)

    @pl.when((i == 0) & (j == 0))
    def _():
        _row_tile_copy(x_hbm, x_ref, sem, 0).start()

    @pl.when(j == 0)
    def _():
        _row_tile_copy(x_hbm, x_ref, sem, i).wait()
        consume()

    @pl.when((j == 1) & (i + 1 < pl.num_programs(0)))
    def _():
        _row_tile_copy(x_hbm, x_ref, sem, i + 1).start()


def _in_proj_kernel(x_hbm, g_ref, w_ref, o_ref, h_ref, x_ref, x_sem, *, n_split):
    def normalise():
        h_ref[...] = _rms_rows(x_ref[...], g_ref[...]).astype(BF16)

    _prefetched_row_tile(x_hbm, x_ref, x_sem, normalise)
    cols = o_ref.shape[1] // n_split
    for s in range(n_split):
        n = slice(s * cols, (s + 1) * cols)
        o_ref[:, n] = jnp.dot(h_ref[...], w_ref[:, n].astype(BF16),
                              preferred_element_type=F32).astype(o_ref.dtype)


def _in_proj(x, g, w, *, tm, tn, n_split):
    s, d = x.shape
    n = w.shape[1]
    assert n // tn >= 2, "the row-tile prefetch starts at inner step 1"
    return pl.pallas_call(
        functools.partial(_in_proj_kernel, n_split=n_split),
        grid=(s // tm, n // tn),
        in_specs=[
            pl.BlockSpec(memory_space=pl.ANY),
            pl.BlockSpec((1, d), lambda i, j: (0, 0)),
            pl.BlockSpec((d, tn), lambda i, j: (0, j)),
        ],
        out_specs=pl.BlockSpec((tm, tn), lambda i, j: (i, j)),
        out_shape=jax.ShapeDtypeStruct((s, n), BF16),
        scratch_shapes=[pltpu.VMEM((tm, d), BF16), pltpu.VMEM((tm, d), F32), pltpu.SemaphoreType.DMA(())],
        compiler_params=pltpu.CompilerParams(
            dimension_semantics=("arbitrary", "arbitrary"),
            vmem_limit_bytes=V7X_VMEM_LIMIT_BYTES),
        name="in_proj",
    )(x, g, w)


def _norm_rows(x, g, b):
    mu = jnp.mean(x, axis=-1, keepdims=True)
    d = x - mu
    var = jnp.mean(d * d, axis=-1, keepdims=True)
    return d * lax.rsqrt(var + EPS) * g + b


def _mixer_kernel(p_ref, ca_ref, sa_ref, cb_ref, sb_ref, cbs_ref, sbs_ref,
                  lng_ref, lnb_ref, ws_ref, bs_ref, gng_ref, gnb_ref,
                  decay_ref, zeta_ref, xi_ref, wout_hbm, x_ref, wg_ref, wu_ref, wd_ref,
                  x1_ref, wgb_ref, wub_ref, wdb_ref,
                  state_ref, wc_ref, bias_ref, woutb_ref, o_ref, prev_ref, stage_ref, stage_sem,
                  *, chunks_per_step, gamma_c):
    step = pl.program_id(0)
    wgb_ref[...] = wg_ref[...].astype(BF16)
    wub_ref[...] = wu_ref[...].astype(BF16)
    wdb_ref[...] = wd_ref[...].astype(BF16)
    row_id = lax.broadcasted_iota(jnp.int32, (CHUNK, CHUNK), 0)
    col_id = lax.broadcasted_iota(jnp.int32, (CHUNK, CHUNK), 1)

    @pl.when(step == 0)
    def _():
        state_ref[...] = jnp.zeros_like(state_ref)
        o_ref[...] = jnp.zeros_like(o_ref)
        stage_rows = stage_ref.shape[1]
        n_stage = woutb_ref.shape[0] // stage_rows

        def stage_copy(r):
            return pltpu.make_async_copy(wout_hbm.at[pl.ds(r * stage_rows, stage_rows), :],
                                         stage_ref.at[r % 2], stage_sem.at[r % 2])

        stage_copy(0).start()
        for r in range(n_stage):
            if r + 1 < n_stage:
                stage_copy(r + 1).start()
            stage_copy(r).wait()
            woutb_ref[r * stage_rows:(r + 1) * stage_rows, :] = stage_ref[r % 2].astype(BF16)
        for h in range(HEADS):
            wc_ref[h] = jnp.where(row_id >= col_id, ws_ref[h], 0.0).astype(BF16)
            b_col = jnp.sum(jnp.where(row_id == col_id, bs_ref[h:h + 1, :], 0.0), axis=1, keepdims=True)
            bias_ref[h] = jnp.broadcast_to(b_col, (CHUNK, HEAD_DIM))

    k_scale = HEAD_DIM ** -0.5
    chunk0 = jnp.minimum(step, pl.num_programs(0) - 2) * chunks_per_step
    heads = range(HEADS)

    prev_ref[...] = o_ref[...]
    piece_cols = x1_ref.shape[1] // N_PROJ_PIECES
    heads_per_piece = chunks_per_step * HEADS // N_PROJ_PIECES

    def project_piece(c, h):
        done = c * HEADS + h + 1
        if done % heads_per_piece == 0:
            n = slice((done // heads_per_piece - 1) * piece_cols, (done // heads_per_piece) * piece_cols)
            x1_ref[:, n] = x_ref[:, n] + jnp.dot(prev_ref[...], woutb_ref[:, n], preferred_element_type=F32)

    def cols(off, h):
        return slice(off + h * HEAD_DIM, off + (h + 1) * HEAD_DIM)

    for c in range(chunks_per_step):
        rows = slice(c * CHUNK, (c + 1) * CHUNK)

        ca = ca_ref[pl.ds(chunk0 + c, 1), :]
        sa = sa_ref[pl.ds(chunk0 + c, 1), :]
        cos2 = ca * cb_ref[...] - sa * sb_ref[...]
        sin2 = sa * cbs_ref[...] + ca * sbs_ref[...]
        cos2k = cos2 * k_scale
        sin2k = sin2 * k_scale

        for h in heads:
            vn = _norm_rows(p_ref[rows, cols(OFF_VS, h)].astype(F32), lng_ref[h:h + 1, :], lnb_ref[h:h + 1, :])
            mixed = jnp.dot(wc_ref[h], vn.astype(BF16), preferred_element_type=F32) + bias_ref[h]
            u = p_ref[rows, cols(OFF_U, h)].astype(F32)
            o_ref[rows, cols(0, h)] = (u * mixed).astype(o_ref.dtype)

        outs = []
        for h in heads:
            q = p_ref[rows, cols(OFF_Q, h)].astype(F32)
            k = p_ref[rows, cols(OFF_K, h)].astype(F32)
            vr = p_ref[rows, cols(OFF_VR, h)]
            qb = (q * cos2 + pltpu.roll(q, HEAD_DIM // 2, axis=1) * sin2).astype(BF16)
            kr = k * cos2k + pltpu.roll(k, HEAD_DIM // 2, axis=1) * sin2k
            scores = lax.dot_general(qb, kr.astype(BF16), (((1,), (1,)), ((), ())),
                                     preferred_element_type=F32) * decay_ref[h]
            intra = jnp.dot(scores.astype(BF16), vr, preferred_element_type=F32)
            state = state_ref[h]
            inter = jnp.dot(qb, state.astype(BF16), preferred_element_type=F32) * xi_ref[h]
            kz = (kr * zeta_ref[h]).astype(BF16)
            kv = lax.dot_general(kz, vr, (((0,), (0,)), ((), ())), preferred_element_type=F32)
            state_ref[h] = gamma_c[h] * state + kv
            outs.append(intra + inter)
            project_piece(c, h)

        for h in heads:
            on = _norm_rows(outs[h], gng_ref[h:h + 1, :], gnb_ref[h:h + 1, :])
            g = p_ref[rows, cols(OFF_G, h)].astype(F32)
            o_ref[rows, cols(SGU_WIDTH, h)] = (g * jax.nn.sigmoid(g) * on).astype(o_ref.dtype)


def _mixer_constants(seq):
    half = HEAD_DIM // 2
    inv = 1.0 / (ROPE_BASE ** (np.arange(half, dtype=np.float64) / half))
    inv2 = np.concatenate([inv, inv])
    sign = np.concatenate([-np.ones(half), np.ones(half)])
    ang_a = (CHUNK * np.arange(seq // CHUNK, dtype=np.float64))[:, None] * inv2[None, :]
    ang_b = np.arange(CHUNK, dtype=np.float64)[:, None] * inv2[None, :]
    rot = [np.cos(ang_a), np.sin(ang_a), np.cos(ang_b), np.sin(ang_b),
           sign * np.cos(ang_b), sign * np.sin(ang_b)]

    log_gamma = np.log(1.0 - np.exp2(-5.0 - np.arange(HEADS, dtype=np.float64)))
    idx = np.arange(CHUNK, dtype=np.float64)
    diff = idx[:, None] - idx[None, :]
    decay = np.where(diff[None] >= 0, np.exp(np.maximum(diff, 0.0)[None] * log_gamma[:, None, None]), 0.0)
    zeta = np.exp((CHUNK - 1.0 - idx)[None, :] * log_gamma[:, None])
    xi = np.exp((idx + 1.0)[None, :] * log_gamma[:, None])
    bcast = lambda v: np.broadcast_to(v[:, :, None], (HEADS, CHUNK, HEAD_DIM))
    gamma_c = tuple(float(np.float32(v)) for v in np.exp(CHUNK * log_gamma))
    tables = [jnp.asarray(np.ascontiguousarray(t), dtype=F32) for t in rot + [decay, bcast(zeta), bcast(xi)]]
    return tables, gamma_c


def _mixer_out_proj(proj, lng, lnb, ws, bs, gng, gnb, w_out, x, w_gate, w_up, w_down, *, chunks_per_step):
    s, d = x.shape
    dff = w_gate.shape[1]
    n_chunks = s // CHUNK
    rows = chunks_per_step * CHUNK
    n_tiles = s // rows
    gate_rows, down_rows = d // n_tiles, dff // n_tiles
    cast_block = lambda i: (jnp.minimum(i, n_tiles - 1), 0)
    (ca, sa, cb, sb, cbs, sbs, decay, zeta, xi), gamma_c = _mixer_constants(s)
    full = lambda shape, **kw: pl.BlockSpec(shape, lambda i: (0,) * len(shape), **kw)
    head_tiles = full((HEADS, CHUNK, HEAD_DIM))
    head_rows = full((HEADS, HEAD_DIM))
    mixed_tile = lambda i: (jnp.minimum(i, n_tiles - 1), 0)
    projected_tile = lambda i: (jnp.maximum(i - 1, 0), 0)
    return pl.pallas_call(
        functools.partial(_mixer_kernel, chunks_per_step=chunks_per_step, gamma_c=gamma_c),
        grid=(n_tiles + 1,),
        in_specs=[
            pl.BlockSpec((rows, IN_WIDTH), mixed_tile),
            full((n_chunks, HEAD_DIM)), full((n_chunks, HEAD_DIM)),
            full((CHUNK, HEAD_DIM)), full((CHUNK, HEAD_DIM)), full((CHUNK, HEAD_DIM)), full((CHUNK, HEAD_DIM)),
            head_rows, head_rows, head_tiles, head_rows, head_rows, head_rows,
            head_tiles, head_tiles, head_tiles,
            pl.BlockSpec(memory_space=pl.ANY),
            pl.BlockSpec((rows, d), projected_tile),
            pl.BlockSpec((gate_rows, dff), cast_block),
            pl.BlockSpec((gate_rows, dff), cast_block),
            pl.BlockSpec((down_rows, d), cast_block),
        ],
        out_specs=[pl.BlockSpec((rows, d), projected_tile),
                   pl.BlockSpec((gate_rows, dff), cast_block),
                   pl.BlockSpec((gate_rows, dff), cast_block),
                   pl.BlockSpec((down_rows, d), cast_block)],
        out_shape=[jax.ShapeDtypeStruct((s, d), F32),
                   jax.ShapeDtypeStruct(w_gate.shape, BF16),
                   jax.ShapeDtypeStruct(w_up.shape, BF16),
                   jax.ShapeDtypeStruct(w_down.shape, BF16)],
        scratch_shapes=[pltpu.VMEM((HEADS, HEAD_DIM, HEAD_DIM), F32),
                        pltpu.VMEM((HEADS, CHUNK, CHUNK), BF16),
                        pltpu.VMEM((HEADS, CHUNK, HEAD_DIM), F32),
                        pltpu.VMEM((MIX_WIDTH, d), BF16),
                        pltpu.VMEM((rows, MIX_WIDTH), BF16),
                        pltpu.VMEM((rows, MIX_WIDTH), BF16),
                        pltpu.VMEM((2, W_OUT_STAGE_ROWS, d), F32),
                        pltpu.SemaphoreType.DMA((2,))],
        compiler_params=pltpu.CompilerParams(
            dimension_semantics=("arbitrary",),
            vmem_limit_bytes=V7X_VMEM_LIMIT_BYTES),
        name="mixer_out_proj",
    )(proj, ca, sa, cb, sb, cbs, sbs, lng, lnb, ws, bs, gng, gnb, decay, zeta, xi, w_out, x,
      w_gate, w_up, w_down)


def _ffn_kernel(x_hbm, g2_ref, wg_ref, wu_ref, wd_ref, gf_ref, o_ref, h_ref, x_ref, x_sem, *, final_norm):
    f = pl.program_id(1)

    def start_row_tile():
        x = x_ref[...]
        h_ref[...] = _rms_rows(x, g2_ref[...]).astype(BF16)
        o_ref[...] = x

    _prefetched_row_tile(x_hbm, x_ref, x_sem, start_row_tile)
    h = h_ref[...]
    gate = jnp.dot(h, wg_ref[...], preferred_element_type=F32)
    up = jnp.dot(h, wu_ref[...], preferred_element_type=F32)
    a = (gate * jax.nn.sigmoid(gate) * up).astype(BF16)
    o_ref[...] += jnp.dot(a, wd_ref[...], preferred_element_type=F32)

    if final_norm:
        @pl.when(f == pl.num_programs(1) - 1)
        def _():
            o_ref[...] = _rms_rows(o_ref[...], gf_ref[...])


def _ffn(x1, g2, wg, wu, wd, gf, *, tm, tf, final_norm):
    s, d = x1.shape
    dff = wg.shape[1]
    return pl.pallas_call(
        functools.partial(_ffn_kernel, final_norm=final_norm),
        grid=(s // tm, dff // tf),
        in_specs=[
            pl.BlockSpec(memory_space=pl.ANY),
            pl.BlockSpec((1, d), lambda i, f: (0, 0)),
            pl.BlockSpec((d, tf), lambda i, f: (0, f)),
            pl.BlockSpec((d, tf), lambda i, f: (0, f)),
            pl.BlockSpec((tf, d), lambda i, f: (f, 0)),
            pl.BlockSpec((1, d), lambda i, f: (0, 0)),
        ],
        out_specs=pl.BlockSpec((tm, d), lambda i, f: (i, 0)),
        out_shape=jax.ShapeDtypeStruct((s, d), F32),
        scratch_shapes=[pltpu.VMEM((tm, d), BF16), pltpu.VMEM((tm, d), F32), pltpu.SemaphoreType.DMA(())],
        compiler_params=pltpu.CompilerParams(
            dimension_semantics=("arbitrary", "arbitrary"),
            vmem_limit_bytes=V7X_VMEM_LIMIT_BYTES),
        name="ffn",
    )(x1, g2, wg, wu, wd, gf)


def kernel(x, norm1_g, w_in, sgu_ln_g, sgu_ln_b, w_spatial, b_spatial, ret_gn_g, ret_gn_b,
           w_out, norm2_g, w_gate, w_up, w_down, final_norm_g):
    batch, seq, d = x.shape
    depth = w_in.shape[0]
    outs = []
    for b in range(batch):
        xb = x[b]
        for l in range(depth):
            proj = _in_proj(xb, norm1_g[l][None, :], w_in[l], tm=2048, tn=1024, n_split=2)
            x1, wg, wu, wd = _mixer_out_proj(proj, sgu_ln_g[l], sgu_ln_b[l], w_spatial[l], b_spatial[l],
                                             ret_gn_g[l], ret_gn_b[l], w_out[l], xb,
                                             w_gate[l], w_up[l], w_down[l], chunks_per_step=2)
            xb = _ffn(x1, norm2_g[l][None, :], wg, wu, wd, final_norm_g[None, :],
                      tm=1024, tf=512, final_norm=(l == depth - 1))
        outs.append(xb)
    return outs[0][None] if batch == 1 else jnp.stack(outs)
```

```python
import functools

import numpy as np

import jax
import jax.numpy as jnp
from jax import lax
from jax.experimental import pallas as pl
from jax.experimental.pallas import tpu as pltpu

D_MODEL = 2048
CHUNK = 128
HEADS = 8
HEAD_DIM = 128
SGU_WIDTH = HEADS * HEAD_DIM
RET_WIDTH = HEADS * HEAD_DIM
MIX_WIDTH = SGU_WIDTH + RET_WIDTH
IN_WIDTH = 2 * SGU_WIDTH + 4 * RET_WIDTH
ROPE_BASE = 10000.0
EPS = 1e-6

OFF_U, OFF_VS, OFF_Q, OFF_K, OFF_VR, OFF_G = (i * SGU_WIDTH for i in range(6))

V7X_VMEM_LIMIT_BYTES = 60 * 1024 * 1024

N_PROJ_PIECES = 8
W_OUT_STAGE_ROWS = 256

F32 = jnp.float32
BF16 = jnp.bfloat16


def _rms_rows(x, g):
    ms = jnp.mean(x * x, axis=-1, keepdims=True)
    return x * lax.rsqrt(ms + EPS) * g


def _row_tile_copy(x_hbm, x_ref, sem, tile):
    tm = x_ref.shape[0]
    start = pl.multiple_of(tile * tm, tm)
    return pltpu.make_async_copy(x_hbm.at[pl.ds(start, tm), :], x_ref, sem)


def _prefetched_row_tile(x_hbm, x_ref, sem, consume):
    i, j = pl.program_id(0), pl.program_id(1)

    @pl.when((i == 0) & (j == 0))
    def _():
        _row_tile_copy(x_hbm, x_ref, sem, 0).start()

    @pl.when(j == 0)
    def _():
        _row_tile_copy(x_hbm, x_ref, sem, i).wait()
        consume()

    @pl.when((j == 1) & (i + 1 < pl.num_programs(0)))
    def _():
        _row_tile_copy(x_hbm, x_ref, sem, i + 1).start()


def _in_proj_kernel(x_hbm, g_ref, w_ref, o_ref, h_ref, x_ref, x_sem, *, n_split):
    def normalise():
        h_ref[...] = _rms_rows(x_ref[...], g_ref[...]).astype(BF16)

    _prefetched_row_tile(x_hbm, x_ref, x_sem, normalise)
    cols = o_ref.shape[1] // n_split
    for s in range(n_split):
        n = slice(s * cols, (s + 1) * cols)
        o_ref[:, n] = jnp.dot(h_ref[...], w_ref[:, n].astype(BF16),
                              preferred_element_type=F32).astype(o_ref.dtype)


def _in_proj(x, g, w, *, tm, tn, n_split):
    s, d = x.shape
    n = w.shape[1]
    assert n // tn >= 2, "the row-tile prefetch starts at inner step 1"
    return pl.pallas_call(
        functools.partial(_in_proj_kernel, n_split=n_split),
        grid=(s // tm, n // tn),
        in_specs=[
            pl.BlockSpec(memory_space=pl.ANY),
            pl.BlockSpec((1, d), lambda i, j: (0, 0)),
            pl.BlockSpec((d, tn), lambda i, j: (0, j)),
        ],
        out_specs=pl.BlockSpec((tm, tn), lambda i, j: (i, j)),
        out_shape=jax.ShapeDtypeStruct((s, n), BF16),
        scratch_shapes=[pltpu.VMEM((tm, d), BF16), pltpu.VMEM((tm, d), F32), pltpu.SemaphoreType.DMA(())],
        compiler_params=pltpu.CompilerParams(
            dimension_semantics=("arbitrary", "arbitrary"),
            vmem_limit_bytes=V7X_VMEM_LIMIT_BYTES),
        name="in_proj",
    )(x, g, w)


def _norm_rows(x, g, b):
    mu = jnp.mean(x, axis=-1, keepdims=True)
    d = x - mu
    var = jnp.mean(d * d, axis=-1, keepdims=True)
    return d * lax.rsqrt(var + EPS) * g + b


def _mixer_kernel(p_ref, ca_ref, sa_ref, cb_ref, sb_ref, cbs_ref, sbs_ref,
                  lng_ref, lnb_ref, ws_ref, bs_ref, gng_ref, gnb_ref,
                  decay_ref, zeta_ref, xi_ref, wout_hbm, x_ref, wg_ref, wu_ref, wd_ref,
                  x1_ref, wgb_ref, wub_ref, wdb_ref,
                  state_ref, wc_ref, bias_ref, woutb_ref, o_ref, prev_ref, stage_ref, stage_sem,
                  *, chunks_per_step, gamma_c):
    step = pl.program_id(0)
    wgb_ref[...] = wg_ref[...].astype(BF16)
    wub_ref[...] = wu_ref[...].astype(BF16)
    wdb_ref[...] = wd_ref[...].astype(BF16)
    row_id = lax.broadcasted_iota(jnp.int32, (CHUNK, CHUNK), 0)
    col_id = lax.broadcasted_iota(jnp.int32, (CHUNK, CHUNK), 1)

    @pl.when(step == 0)
    def _():
        state_ref[...] = jnp.zeros_like(state_ref)
        o_ref[...] = jnp.zeros_like(o_ref)
        stage_rows = stage_ref.shape[1]
        n_stage = woutb_ref.shape[0] // stage_rows

        def stage_copy(r):
            return pltpu.make_async_copy(wout_hbm.at[pl.ds(r * stage_rows, stage_rows), :],
                                         stage_ref.at[r % 2], stage_sem.at[r % 2])

        stage_copy(0).start()
        for r in range(n_stage):
            if r + 1 < n_stage:
                stage_copy(r + 1).start()
            stage_copy(r).wait()
            woutb_ref[r * stage_rows:(r + 1) * stage_rows, :] = stage_ref[r % 2].astype(BF16)
        for h in range(HEADS):
            wc_ref[h] = jnp.where(row_id >= col_id, ws_ref[h], 0.0).astype(BF16)
            b_col = jnp.sum(jnp.where(row_id == col_id, bs_ref[h:h + 1, :], 0.0), axis=1, keepdims=True)
            bias_ref[h] = jnp.broadcast_to(b_col, (CHUNK, HEAD_DIM))

    k_scale = HEAD_DIM ** -0.5
    chunk0 = jnp.minimum(step, pl.num_programs(0) - 2) * chunks_per_step
    heads = range(HEADS)

    prev_ref[...] = o_ref[...]
    piece_cols = x1_ref.shape[1] // N_PROJ_PIECES
    heads_per_piece = chunks_per_step * HEADS // N_PROJ_PIECES

    def project_piece(c, h):
        done = c * HEADS + h + 1
        if done % heads_per_piece == 0:
            n = slice((done // heads_per_piece - 1) * piece_cols, (done // heads_per_piece) * piece_cols)
            x1_ref[:, n] = x_ref[:, n] + jnp.dot(prev_ref[...], woutb_ref[:, n], preferred_element_type=F32)

    def cols(off, h):
        return slice(off + h * HEAD_DIM, off + (h + 1) * HEAD_DIM)

    for c in range(chunks_per_step):
        rows = slice(c * CHUNK, (c + 1) * CHUNK)

        ca = ca_ref[pl.ds(chunk0 + c, 1), :]
        sa = sa_ref[pl.ds(chunk0 + c, 1), :]
        cos2 = ca * cb_ref[...] - sa * sb_ref[...]
        sin2 = sa * cbs_ref[...] + ca * sbs_ref[...]
        cos2k = cos2 * k_scale
        sin2k = sin2 * k_scale

        for h in heads:
            vn = _norm_rows(p_ref[rows, cols(OFF_VS, h)].astype(F32), lng_ref[h:h + 1, :], lnb_ref[h:h + 1, :])
            mixed = jnp.dot(wc_ref[h], vn.astype(BF16), preferred_element_type=F32) + bias_ref[h]
            u = p_ref[rows, cols(OFF_U, h)].astype(F32)
            o_ref[rows, cols(0, h)] = (u * mixed).astype(o_ref.dtype)

        outs = []
        for h in heads:
            q = p_ref[rows, cols(OFF_Q, h)].astype(F32)
            k = p_ref[rows, cols(OFF_K, h)].astype(F32)
            vr = p_ref[rows, cols(OFF_VR, h)]
            qb = (q * cos2 + pltpu.roll(q, HEAD_DIM // 2, axis=1) * sin2).astype(BF16)
            kr = k * cos2k + pltpu.roll(k, HEAD_DIM // 2, axis=1) * sin2k
            scores = lax.dot_general(qb, kr.astype(BF16), (((1,), (1,)), ((), ())),
                                     preferred_element_type=F32) * decay_ref[h]
            intra = jnp.dot(scores.astype(BF16), vr, preferred_element_type=F32)
            state = state_ref[h]
            inter = jnp.dot(qb, state.astype(BF16), preferred_element_type=F32) * xi_ref[h]
            kz = (kr * zeta_ref[h]).astype(BF16)
            kv = lax.dot_general(kz, vr, (((0,), (0,)), ((), ())), preferred_element_type=F32)
            state_ref[h] = gamma_c[h] * state + kv
            outs.append(intra + inter)
            project_piece(c, h)

        for h in heads:
            on = _norm_rows(outs[h], gng_ref[h:h + 1, :], gnb_ref[h:h + 1, :])
            g = p_ref[rows, cols(OFF_G, h)].astype(F32)
            o_ref[rows, cols(SGU_WIDTH, h)] = (g * jax.nn.sigmoid(g) * on).astype(o_ref.dtype)


def _mixer_constants(seq):
    half = HEAD_DIM // 2
    inv = 1.0 / (ROPE_BASE ** (np.arange(half, dtype=np.float64) / half))
    inv2 = np.concatenate([inv, inv])
    sign = np.concatenate([-np.ones(half), np.ones(half)])
    ang_a = (CHUNK * np.arange(seq // CHUNK, dtype=np.float64))[:, None] * inv2[None, :]
    ang_b = np.arange(CHUNK, dtype=np.float64)[:, None] * inv2[None, :]
    rot = [np.cos(ang_a), np.sin(ang_a), np.cos(ang_b), np.sin(ang_b),
           sign * np.cos(ang_b), sign * np.sin(ang_b)]

    log_gamma = np.log(1.0 - np.exp2(-5.0 - np.arange(HEADS, dtype=np.float64)))
    idx = np.arange(CHUNK, dtype=np.float64)
    diff = idx[:, None] - idx[None, :]
    decay = np.where(diff[None] >= 0, np.exp(np.maximum(diff, 0.0)[None] * log_gamma[:, None, None]), 0.0)
    zeta = np.exp((CHUNK - 1.0 - idx)[None, :] * log_gamma[:, None])
    xi = np.exp((idx + 1.0)[None, :] * log_gamma[:, None])
    bcast = lambda v: np.broadcast_to(v[:, :, None], (HEADS, CHUNK, HEAD_DIM))
    gamma_c = tuple(float(np.float32(v)) for v in np.exp(CHUNK * log_gamma))
    tables = [jnp.asarray(np.ascontiguousarray(t), dtype=F32) for t in rot + [decay, bcast(zeta), bcast(xi)]]
    return tables, gamma_c


def _mixer_out_proj(proj, lng, lnb, ws, bs, gng, gnb, w_out, x, w_gate, w_up, w_down, *, chunks_per_step):
    s, d = x.shape
    dff = w_gate.shape[1]
    n_chunks = s // CHUNK
    rows = chunks_per_step * CHUNK
    n_tiles = s // rows
    gate_rows, down_rows = d // n_tiles, dff // n_tiles
    cast_block = lambda i: (jnp.minimum(i, n_tiles - 1), 0)
    (ca, sa, cb, sb, cbs, sbs, decay, zeta, xi), gamma_c = _mixer_constants(s)
    full = lambda shape, **kw: pl.BlockSpec(shape, lambda i: (0,) * len(shape), **kw)
    head_tiles = full((HEADS, CHUNK, HEAD_DIM))
    head_rows = full((HEADS, HEAD_DIM))
    mixed_tile = lambda i: (jnp.minimum(i, n_tiles - 1), 0)
    projected_tile = lambda i: (jnp.maximum(i - 1, 0), 0)
    return pl.pallas_call(
        functools.partial(_mixer_kernel, chunks_per_step=chunks_per_step, gamma_c=gamma_c),
        grid=(n_tiles + 1,),
        in_specs=[
            pl.BlockSpec((rows, IN_WIDTH), mixed_tile),
            full((n_chunks, HEAD_DIM)), full((n_chunks, HEAD_DIM)),
            full((CHUNK, HEAD_DIM)), full((CHUNK, HEAD_DIM)), full((CHUNK, HEAD_DIM)), full((CHUNK, HEAD_DIM)),
            head_rows, head_rows, head_tiles, head_rows, head_rows, head_rows,
            head_tiles, head_tiles, head_tiles,
            pl.BlockSpec(memory_space=pl.ANY),
            pl.BlockSpec((rows, d), projected_tile),
            pl.BlockSpec((gate_rows, dff), cast_block),
            pl.BlockSpec((gate_rows, dff), cast_block),
            pl.BlockSpec((down_rows, d), cast_block),
        ],
        out_specs=[pl.BlockSpec((rows, d), projected_tile),
                   pl.BlockSpec((gate_rows, dff), cast_block),
                   pl.BlockSpec((gate_rows, dff), cast_block),
                   pl.BlockSpec((down_rows, d), cast_block)],
        out_shape=[jax.ShapeDtypeStruct((s, d), F32),
                   jax.ShapeDtypeStruct(w_gate.shape, BF16),
                   jax.ShapeDtypeStruct(w_up.shape, BF16),
                   jax.ShapeDtypeStruct(w_down.shape, BF16)],
        scratch_shapes=[pltpu.VMEM((HEADS, HEAD_DIM, HEAD_DIM), F32),
                        pltpu.VMEM((HEADS, CHUNK, CHUNK), BF16),
                        pltpu.VMEM((HEADS, CHUNK, HEAD_DIM), F32),
                        pltpu.VMEM((MIX_WIDTH, d), BF16),
                        pltpu.VMEM((rows, MIX_WIDTH), BF16),
                        pltpu.VMEM((rows, MIX_WIDTH), BF16),
                        pltpu.VMEM((2, W_OUT_STAGE_ROWS, d), F32),
                        pltpu.SemaphoreType.DMA((2,))],
        compiler_params=pltpu.CompilerParams(
            dimension_semantics=("arbitrary",),
            vmem_limit_bytes=V7X_VMEM_LIMIT_BYTES),
        name="mixer_out_proj",
    )(proj, ca, sa, cb, sb, cbs, sbs, lng, lnb, ws, bs, gng, gnb, decay, zeta, xi, w_out, x,
      w_gate, w_up, w_down)


def _ffn_kernel(x_hbm, g2_ref, wg_ref, wu_ref, wd_ref, gf_ref, o_ref, h_ref, x_ref, x_sem, *, final_norm):
    f = pl.program_id(1)
    last = pl.num_programs(1) - 1

    def ffn_step(is_first, is_last):
        if is_first:
            base = x_ref[...]
            h = _rms_rows(base, g2_ref[...]).astype(BF16)
            h_ref[...] = h
        else:
            base = o_ref[...]
            h = h_ref[...]
        gate = jnp.dot(h, wg_ref[...], preferred_element_type=F32)
        up = jnp.dot(h, wu_ref[...], preferred_element_type=F32)
        a = (gate * jax.nn.sigmoid(gate) * up).astype(BF16)
        y = base + jnp.dot(a, wd_ref[...], preferred_element_type=F32)
        o_ref[...] = _rms_rows(y, gf_ref[...]) if (is_last and final_norm) else y

    _prefetched_row_tile(x_hbm, x_ref, x_sem, functools.partial(ffn_step, True, False))
    pl.when((f > 0) & (f < last))(functools.partial(ffn_step, False, False))
    pl.when(f == last)(functools.partial(ffn_step, False, True))


def _ffn(x1, g2, wg, wu, wd, gf, *, tm, tf, final_norm):
    s, d = x1.shape
    dff = wg.shape[1]
    assert dff // tf >= 2, "first and last inner steps are distinct code paths"
    return pl.pallas_call(
        functools.partial(_ffn_kernel, final_norm=final_norm),
        grid=(s // tm, dff // tf),
        in_specs=[
            pl.BlockSpec(memory_space=pl.ANY),
            pl.BlockSpec((1, d), lambda i, f: (0, 0)),
            pl.BlockSpec((d, tf), lambda i, f: (0, f)),
            pl.BlockSpec((d, tf), lambda i, f: (0, f)),
            pl.BlockSpec((tf, d), lambda i, f: (f, 0)),
            pl.BlockSpec((1, d), lambda i, f: (0, 0)),
        ],
        out_specs=pl.BlockSpec((tm, d), lambda i, f: (i, 0)),
        out_shape=jax.ShapeDtypeStruct((s, d), F32),
        scratch_shapes=[pltpu.VMEM((tm, d), BF16), pltpu.VMEM((tm, d), F32), pltpu.SemaphoreType.DMA(())],
        compiler_params=pltpu.CompilerParams(
            dimension_semantics=("arbitrary", "arbitrary"),
            vmem_limit_bytes=V7X_VMEM_LIMIT_BYTES),
        name="ffn",
    )(x1, g2, wg, wu, wd, gf)


def kernel(x, norm1_g, w_in, sgu_ln_g, sgu_ln_b, w_spatial, b_spatial, ret_gn_g, ret_gn_b,
           w_out, norm2_g, w_gate, w_up, w_down, final_norm_g):
    batch, seq, d = x.shape
    depth = w_in.shape[0]
    outs = []
    for b in range(batch):
        xb = x[b]
        for l in range(depth):
            proj = _in_proj(xb, norm1_g[l][None, :], w_in[l], tm=2048, tn=1024, n_split=2)
            x1, wg, wu, wd = _mixer_out_proj(proj, sgu_ln_g[l], sgu_ln_b[l], w_spatial[l], b_spatial[l],
                                             ret_gn_g[l], ret_gn_b[l], w_out[l], xb,
                                             w_gate[l], w_up[l], w_down[l], chunks_per_step=2)
            xb = _ffn(x1, norm2_g[l][None, :], wg, wu, wd, final_norm_g[None, :],
                      tm=1024, tf=512, final_norm=(l == depth - 1))
        outs.append(xb)
    return outs[0][None] if batch == 1 else jnp.stack(outs)
```

```python
import functools

import numpy as np

import jax
import jax.numpy as jnp
from jax import lax
from jax.experimental import pallas as pl
from jax.experimental.pallas import tpu as pltpu

D_MODEL = 2048
CHUNK = 128
HEADS = 8
HEAD_DIM = 128
SGU_WIDTH = HEADS * HEAD_DIM
RET_WIDTH = HEADS * HEAD_DIM
MIX_WIDTH = SGU_WIDTH + RET_WIDTH
IN_WIDTH = 2 * SGU_WIDTH + 4 * RET_WIDTH
ROPE_BASE = 10000.0
EPS = 1e-6

OFF_U, OFF_VS, OFF_Q, OFF_K, OFF_VR, OFF_G = (i * SGU_WIDTH for i in range(6))

V7X_VMEM_LIMIT_BYTES = 60 * 1024 * 1024

N_PROJ_PIECES = 8
W_OUT_STAGE_ROWS = 256

F32 = jnp.float32
BF16 = jnp.bfloat16


def _rms_rows(x, g):
    ms = jnp.mean(x * x, axis=-1, keepdims=True)
    return x * lax.rsqrt(ms + EPS) * g


def _row_tile_copy(x_hbm, x_ref, sem, tile):
    tm = x_ref.shape[0]
    start = pl.multiple_of(tile * tm, tm)
    return pltpu.make_async_copy(x_hbm.at[pl.ds(start, tm), :], x_ref, sem)


def _prefetched_row_tile(x_hbm, x_ref, sem, consume):
    i, j = pl.program_id(0), pl.program_id(1)

    @pl.when((i == 0) & (j == 0))
    def _():
        _row_tile_copy(x_hbm, x_ref, sem, 0).start()

    @pl.when(j == 0)
    def _():
        _row_tile_copy(x_hbm, x_ref, sem, i).wait()
        consume()

    @pl.when((j == 1) & (i + 1 < pl.num_programs(0)))
    def _():
        _row_tile_copy(x_hbm, x_ref, sem, i + 1).start()


def _in_proj_kernel(x_hbm, g_ref, w_ref, o_ref, h_ref, x_ref, x_sem, *, n_split):
    def normalise():
        h_ref[...] = _rms_rows(x_ref[...], g_ref[...]).astype(BF16)

    _prefetched_row_tile(x_hbm, x_ref, x_sem, normalise)
    cols = o_ref.shape[1] // n_split
    for s in range(n_split):
        n = slice(s * cols, (s + 1) * cols)
        o_ref[:, n] = jnp.dot(h_ref[...], w_ref[:, n].astype(BF16),
                              preferred_element_type=F32).astype(o_ref.dtype)


def _in_proj(x, g, w, *, tm, tn, n_split):
    s, d = x.shape
    n = w.shape[1]
    assert n // tn >= 2, "the row-tile prefetch starts at inner step 1"
    return pl.pallas_call(
        functools.partial(_in_proj_kernel, n_split=n_split),
        grid=(s // tm, n // tn),
        in_specs=[
            pl.BlockSpec(memory_space=pl.ANY),
            pl.BlockSpec((1, d), lambda i, j: (0, 0)),
            pl.BlockSpec((d, tn), lambda i, j: (0, j)),
        ],
        out_specs=pl.BlockSpec((tm, tn), lambda i, j: (i, j)),
        out_shape=jax.ShapeDtypeStruct((s, n), BF16),
        scratch_shapes=[pltpu.VMEM((tm, d), BF16), pltpu.VMEM((tm, d), F32), pltpu.SemaphoreType.DMA(())],
        compiler_params=pltpu.CompilerParams(
            dimension_semantics=("arbitrary", "arbitrary"),
            vmem_limit_bytes=V7X_VMEM_LIMIT_BYTES),
        name="in_proj",
    )(x, g, w)


def _norm_rows(x, g, b):
    mu = jnp.mean(x, axis=-1, keepdims=True)
    d = x - mu
    var = jnp.mean(d * d, axis=-1, keepdims=True)
    return d * lax.rsqrt(var + EPS) * g + b


def _mixer_kernel(p_ref, ca_ref, sa_ref, cb_ref, sb_ref, cbs_ref, sbs_ref,
                  lng_ref, lnb_ref, ws_ref, bs_ref, gng_ref, gnb_ref,
                  decay_ref, zeta_ref, xi_ref, wout_hbm, x_ref, wg_ref, wu_ref, wd_ref,
                  x1_ref, wgb_ref, wub_ref, wdb_ref,
                  state_ref, wc_ref, bias_ref, woutb_ref, o_ref, prev_ref, stage_ref, stage_sem,
                  *, chunks_per_step, gamma_c):
    step = pl.program_id(0)
    wgb_ref[...] = wg_ref[...].astype(BF16)
    wub_ref[...] = wu_ref[...].astype(BF16)
    wdb_ref[...] = wd_ref[...].astype(BF16)
    row_id = lax.broadcasted_iota(jnp.int32, (CHUNK, CHUNK), 0)
    col_id = lax.broadcasted_iota(jnp.int32, (CHUNK, CHUNK), 1)

    @pl.when(step == 0)
    def _():
        state_ref[...] = jnp.zeros_like(state_ref)
        o_ref[...] = jnp.zeros_like(o_ref)
        stage_rows = stage_ref.shape[1]
        n_stage = woutb_ref.shape[0] // stage_rows

        def stage_copy(r):
            return pltpu.make_async_copy(wout_hbm.at[pl.ds(r * stage_rows, stage_rows), :],
                                         stage_ref.at[r % 2], stage_sem.at[r % 2])

        stage_copy(0).start()
        for r in range(n_stage):
            if r + 1 < n_stage:
                stage_copy(r + 1).start()
            stage_copy(r).wait()
            woutb_ref[r * stage_rows:(r + 1) * stage_rows, :] = stage_ref[r % 2].astype(BF16)
        for h in range(HEADS):
            wc_ref[h] = jnp.where(row_id >= col_id, ws_ref[h], 0.0).astype(BF16)
            b_col = jnp.sum(jnp.where(row_id == col_id, bs_ref[h:h + 1, :], 0.0), axis=1, keepdims=True)
            bias_ref[h] = jnp.broadcast_to(b_col, (CHUNK, HEAD_DIM))

    k_scale = HEAD_DIM ** -0.5
    chunk0 = jnp.minimum(step, pl.num_programs(0) - 2) * chunks_per_step
    heads = range(HEADS)

    prev_ref[...] = o_ref[...]
    piece_cols = x1_ref.shape[1] // N_PROJ_PIECES
    pieces_per_chunk = N_PROJ_PIECES // chunks_per_step
    piece_heads = [(k + 1) * HEADS // (pieces_per_chunk - 1) - 1 for k in range(pieces_per_chunk - 1)]
    issued = []

    def project_piece():
        n = slice(len(issued) * piece_cols, (len(issued) + 1) * piece_cols)
        issued.append(n)
        x1_ref[:, n] = x_ref[:, n] + jnp.dot(prev_ref[...], woutb_ref[:, n], preferred_element_type=F32)

    def cols(off, h):
        return slice(off + h * HEAD_DIM, off + (h + 1) * HEAD_DIM)

    for c in range(chunks_per_step):
        rows = slice(c * CHUNK, (c + 1) * CHUNK)
        project_piece()

        ca = ca_ref[pl.ds(chunk0 + c, 1), :]
        sa = sa_ref[pl.ds(chunk0 + c, 1), :]
        cos2 = ca * cb_ref[...] - sa * sb_ref[...]
        sin2 = sa * cbs_ref[...] + ca * sbs_ref[...]
        cos2k = cos2 * k_scale
        sin2k = sin2 * k_scale

        for h in heads:
            vn = _norm_rows(p_ref[rows, cols(OFF_VS, h)].astype(F32), lng_ref[h:h + 1, :], lnb_ref[h:h + 1, :])
            mixed = jnp.dot(wc_ref[h], vn.astype(BF16), preferred_element_type=F32) + bias_ref[h]
            u = p_ref[rows, cols(OFF_U, h)].astype(F32)
            o_ref[rows, cols(0, h)] = (u * mixed).astype(o_ref.dtype)

        outs = []
        for h in heads:
            q = p_ref[rows, cols(OFF_Q, h)].astype(F32)
            k = p_ref[rows, cols(OFF_K, h)].astype(F32)
            vr = p_ref[rows, cols(OFF_VR, h)]
            qb = (q * cos2 + pltpu.roll(q, HEAD_DIM // 2, axis=1) * sin2).astype(BF16)
            kr = k * cos2k + pltpu.roll(k, HEAD_DIM // 2, axis=1) * sin2k
            scores = lax.dot_general(qb, kr.astype(BF16), (((1,), (1,)), ((), ())),
                                     preferred_element_type=F32) * decay_ref[h]
            intra = jnp.dot(scores.astype(BF16), vr, preferred_element_type=F32)
            state = state_ref[h]
            inter = jnp.dot(qb, state.astype(BF16), preferred_element_type=F32) * xi_ref[h]
            kz = (kr * zeta_ref[h]).astype(BF16)
            kv = lax.dot_general(kz, vr, (((0,), (0,)), ((), ())), preferred_element_type=F32)
            state_ref[h] = gamma_c[h] * state + kv
            outs.append(intra + inter)
            if h in piece_heads:
                project_piece()

        for h in heads:
            on = _norm_rows(outs[h], gng_ref[h:h + 1, :], gnb_ref[h:h + 1, :])
            g = p_ref[rows, cols(OFF_G, h)].astype(F32)
            o_ref[rows, cols(SGU_WIDTH, h)] = (g * jax.nn.sigmoid(g) * on).astype(o_ref.dtype)
    assert len(issued) == N_PROJ_PIECES


def _mixer_constants(seq):
    half = HEAD_DIM // 2
    inv = 1.0 / (ROPE_BASE ** (np.arange(half, dtype=np.float64) / half))
    inv2 = np.concatenate([inv, inv])
    sign = np.concatenate([-np.ones(half), np.ones(half)])
    ang_a = (CHUNK * np.arange(seq // CHUNK, dtype=np.float64))[:, None] * inv2[None, :]
    ang_b = np.arange(CHUNK, dtype=np.float64)[:, None] * inv2[None, :]
    rot = [np.cos(ang_a), np.sin(ang_a), np.cos(ang_b), np.sin(ang_b),
           sign * np.cos(ang_b), sign * np.sin(ang_b)]

    log_gamma = np.log(1.0 - np.exp2(-5.0 - np.arange(HEADS, dtype=np.float64)))
    idx = np.arange(CHUNK, dtype=np.float64)
    diff = idx[:, None] - idx[None, :]
    decay = np.where(diff[None] >= 0, np.exp(np.maximum(diff, 0.0)[None] * log_gamma[:, None, None]), 0.0)
    zeta = np.exp((CHUNK - 1.0 - idx)[None, :] * log_gamma[:, None])
    xi = np.exp((idx + 1.0)[None, :] * log_gamma[:, None])
    bcast = lambda v: np.broadcast_to(v[:, :, None], (HEADS, CHUNK, HEAD_DIM))
    gamma_c = tuple(float(np.float32(v)) for v in np.exp(CHUNK * log_gamma))
    tables = [jnp.asarray(np.ascontiguousarray(t), dtype=F32) for t in rot + [decay, bcast(zeta), bcast(xi)]]
    return tables, gamma_c


def _mixer_out_proj(proj, lng, lnb, ws, bs, gng, gnb, w_out, x, w_gate, w_up, w_down, *, chunks_per_step):
    s, d = x.shape
    dff = w_gate.shape[1]
    n_chunks = s // CHUNK
    rows = chunks_per_step * CHUNK
    n_tiles = s // rows
    gate_rows, down_rows = d // n_tiles, dff // n_tiles
    cast_block = lambda i: (jnp.minimum(i, n_tiles - 1), 0)
    (ca, sa, cb, sb, cbs, sbs, decay, zeta, xi), gamma_c = _mixer_constants(s)
    full = lambda shape, **kw: pl.BlockSpec(shape, lambda i: (0,) * len(shape), **kw)
    head_tiles = full((HEADS, CHUNK, HEAD_DIM))
    head_rows = full((HEADS, HEAD_DIM))
    mixed_tile = lambda i: (jnp.minimum(i, n_tiles - 1), 0)
    projected_tile = lambda i: (jnp.maximum(i - 1, 0), 0)
    return pl.pallas_call(
        functools.partial(_mixer_kernel, chunks_per_step=chunks_per_step, gamma_c=gamma_c),
        grid=(n_tiles + 1,),
        in_specs=[
            pl.BlockSpec((rows, IN_WIDTH), mixed_tile),
            full((n_chunks, HEAD_DIM)), full((n_chunks, HEAD_DIM)),
            full((CHUNK, HEAD_DIM)), full((CHUNK, HEAD_DIM)), full((CHUNK, HEAD_DIM)), full((CHUNK, HEAD_DIM)),
            head_rows, head_rows, head_tiles, head_rows, head_rows, head_rows,
            head_tiles, head_tiles, head_tiles,
            pl.BlockSpec(memory_space=pl.ANY),
            pl.BlockSpec((rows, d), projected_tile),
            pl.BlockSpec((gate_rows, dff), cast_block),
            pl.BlockSpec((gate_rows, dff), cast_block),
            pl.BlockSpec((down_rows, d), cast_block),
        ],
        out_specs=[pl.BlockSpec((rows, d), projected_tile),
                   pl.BlockSpec((gate_rows, dff), cast_block),
                   pl.BlockSpec((gate_rows, dff), cast_block),
                   pl.BlockSpec((down_rows, d), cast_block)],
        out_shape=[jax.ShapeDtypeStruct((s, d), F32),
                   jax.ShapeDtypeStruct(w_gate.shape, BF16),
                   jax.ShapeDtypeStruct(w_up.shape, BF16),
                   jax.ShapeDtypeStruct(w_down.shape, BF16)],
        scratch_shapes=[pltpu.VMEM((HEADS, HEAD_DIM, HEAD_DIM), F32),
                        pltpu.VMEM((HEADS, CHUNK, CHUNK), BF16),
                        pltpu.VMEM((HEADS, CHUNK, HEAD_DIM), F32),
                        pltpu.VMEM((MIX_WIDTH, d), BF16),
                        pltpu.VMEM((rows, MIX_WIDTH), BF16),
                        pltpu.VMEM((rows, MIX_WIDTH), BF16),
                        pltpu.VMEM((2, W_OUT_STAGE_ROWS, d), F32),
                        pltpu.SemaphoreType.DMA((2,))],
        compiler_params=pltpu.CompilerParams(
            dimension_semantics=("arbitrary",),
            vmem_limit_bytes=V7X_VMEM_LIMIT_BYTES),
        name="mixer_out_proj",
    )(proj, ca, sa, cb, sb, cbs, sbs, lng, lnb, ws, bs, gng, gnb, decay, zeta, xi, w_out, x,
      w_gate, w_up, w_down)


def _ffn_kernel(x_hbm, g2_ref, wg_ref, wu_ref, wd_ref, gf_ref, o_ref, h_ref, x_ref, x_sem, *, final_norm):
    f = pl.program_id(1)
    last = pl.num_programs(1) - 1

    def ffn_step(is_first, is_last):
        if is_first:
            base = x_ref[...]
            h = _rms_rows(base, g2_ref[...]).astype(BF16)
            h_ref[...] = h
        else:
            base = o_ref[...]
            h = h_ref[...]
        gate = jnp.dot(h, wg_ref[...], preferred_element_type=F32)
        up = jnp.dot(h, wu_ref[...], preferred_element_type=F32)
        a = (gate * jax.nn.sigmoid(gate) * up).astype(BF16)
        y = base + jnp.dot(a, wd_ref[...], preferred_element_type=F32)
        o_ref[...] = _rms_rows(y, gf_ref[...]) if (is_last and final_norm) else y

    _prefetched_row_tile(x_hbm, x_ref, x_sem, functools.partial(ffn_step, True, False))
    pl.when((f > 0) & (f < last))(functools.partial(ffn_step, False, False))
    pl.when(f == last)(functools.partial(ffn_step, False, True))


def _ffn(x1, g2, wg, wu, wd, gf, *, tm, tf, final_norm):
    s, d = x1.shape
    dff = wg.shape[1]
    assert dff // tf >= 2, "first and last inner steps are distinct code paths"
    return pl.pallas_call(
        functools.partial(_ffn_kernel, final_norm=final_norm),
        grid=(s // tm, dff // tf),
        in_specs=[
            pl.BlockSpec(memory_space=pl.ANY),
            pl.BlockSpec((1, d), lambda i, f: (0, 0)),
            pl.BlockSpec((d, tf), lambda i, f: (0, f)),
            pl.BlockSpec((d, tf), lambda i, f: (0, f)),
            pl.BlockSpec((tf, d), lambda i, f: (f, 0)),
            pl.BlockSpec((1, d), lambda i, f: (0, 0)),
        ],
        out_specs=pl.BlockSpec((tm, d), lambda i, f: (i, 0)),
        out_shape=jax.ShapeDtypeStruct((s, d), F32),
        scratch_shapes=[pltpu.VMEM((tm, d), BF16), pltpu.VMEM((tm, d), F32), pltpu.SemaphoreType.DMA(())],
        compiler_params=pltpu.CompilerParams(
            dimension_semantics=("arbitrary", "arbitrary"),
            vmem_limit_bytes=V7X_VMEM_LIMIT_BYTES),
        name="ffn",
    )(x1, g2, wg, wu, wd, gf)


def kernel(x, norm1_g, w_in, sgu_ln_g, sgu_ln_b, w_spatial, b_spatial, ret_gn_g, ret_gn_b,
           w_out, norm2_g, w_gate, w_up, w_down, final_norm_g):
    batch, seq, d = x.shape
    depth = w_in.shape[0]
    outs = []
    for b in range(batch):
        xb = x[b]
        for l in range(depth):
            proj = _in_proj(xb, norm1_g[l][None, :], w_in[l], tm=2048, tn=1024, n_split=2)
            x1, wg, wu, wd = _mixer_out_proj(proj, sgu_ln_g[l], sgu_ln_b[l], w_spatial[l], b_spatial[l],
                                             ret_gn_g[l], ret_gn_b[l], w_out[l], xb,
                                             w_gate[l], w_up[l], w_down[l], chunks_per_step=2)
            xb = _ffn(x1, norm2_g[l][None, :], wg, wu, wd, final_norm_g[None, :],
                      tm=1024, tf=512, final_norm=(l == depth - 1))
        outs.append(xb)
    return outs[0][None] if batch == 1 else jnp.stack(outs)
```

```python
import functools

import numpy as np

import jax
import jax.numpy as jnp
from jax import lax
from jax.experimental import pallas as pl
from jax.experimental.pallas import tpu as pltpu

D_MODEL = 2048
CHUNK = 128
HEADS = 8
HEAD_DIM = 128
SGU_WIDTH = HEADS * HEAD_DIM
RET_WIDTH = HEADS * HEAD_DIM
MIX_WIDTH = SGU_WIDTH + RET_WIDTH
IN_WIDTH = 2 * SGU_WIDTH + 4 * RET_WIDTH
ROPE_BASE = 10000.0
EPS = 1e-6

OFF_U, OFF_VS, OFF_Q, OFF_K, OFF_VR, OFF_G = (i * SGU_WIDTH for i in range(6))

V7X_VMEM_LIMIT_BYTES = 60 * 1024 * 1024

N_PROJ_PIECES = 8
W_OUT_STAGE_ROWS = 256

IN_PROJ_ROWS, IN_PROJ_COLS, IN_PROJ_DOTS_PER_STEP = 2048, 1024, 2
MIXER_CHUNKS_PER_STEP = 2
FFN_ROWS, FFN_COLS = 1024, 512

F32 = jnp.float32
BF16 = jnp.bfloat16


def _rms_rows(x, g):
    ms = jnp.mean(x * x, axis=-1, keepdims=True)
    return x * lax.rsqrt(ms + EPS) * g


def _row_tile_copy(x_hbm, x_ref, sem, tile):
    tm = x_ref.shape[0]
    start = pl.multiple_of(tile * tm, tm)
    return pltpu.make_async_copy(x_hbm.at[pl.ds(start, tm), :], x_ref, sem)


def _prefetched_row_tile(x_hbm, x_ref, sem, consume):
    i, j = pl.program_id(0), pl.program_id(1)

    @pl.when((i == 0) & (j == 0))
    def _():
        _row_tile_copy(x_hbm, x_ref, sem, 0).start()

    @pl.when(j == 0)
    def _():
        _row_tile_copy(x_hbm, x_ref, sem, i).wait()
        consume()

    @pl.when((j == 1) & (i + 1 < pl.num_programs(0)))
    def _():
        _row_tile_copy(x_hbm, x_ref, sem, i + 1).start()


def _in_proj_kernel(x_hbm, g_ref, w_ref, o_ref, h_ref, x_ref, x_sem, *, n_split):
    def normalise():
        h_ref[...] = _rms_rows(x_ref[...], g_ref[...]).astype(BF16)

    _prefetched_row_tile(x_hbm, x_ref, x_sem, normalise)
    cols = o_ref.shape[1] // n_split
    for s in range(n_split):
        n = slice(s * cols, (s + 1) * cols)
        o_ref[:, n] = jnp.dot(h_ref[...], w_ref[:, n].astype(BF16),
                              preferred_element_type=F32).astype(o_ref.dtype)


def _in_proj(x, g, w, *, tm, tn, n_split):
    s, d = x.shape
    n = w.shape[1]
    assert n // tn >= 2, "the row-tile prefetch starts at inner step 1"
    return pl.pallas_call(
        functools.partial(_in_proj_kernel, n_split=n_split),
        grid=(s // tm, n // tn),
        in_specs=[
            pl.BlockSpec(memory_space=pl.ANY),
            pl.BlockSpec((1, d), lambda i, j: (0, 0)),
            pl.BlockSpec((d, tn), lambda i, j: (0, j)),
        ],
        out_specs=pl.BlockSpec((tm, tn), lambda i, j: (i, j)),
        out_shape=jax.ShapeDtypeStruct((s, n), BF16),
        scratch_shapes=[pltpu.VMEM((tm, d), BF16), pltpu.VMEM((tm, d), F32), pltpu.SemaphoreType.DMA(())],
        compiler_params=pltpu.CompilerParams(
            dimension_semantics=("arbitrary", "arbitrary"),
            vmem_limit_bytes=V7X_VMEM_LIMIT_BYTES),
        name="in_proj",
    )(x, g, w)


def _norm_rows(x, g, b):
    mu = jnp.mean(x, axis=-1, keepdims=True)
    d = x - mu
    var = jnp.mean(d * d, axis=-1, keepdims=True)
    return d * lax.rsqrt(var + EPS) * g + b


def _mixer_kernel(p_ref, ca_ref, sa_ref, cb_ref, sb_ref, cbs_ref, sbs_ref,
                  lng_ref, lnb_ref, ws_ref, bs_ref, gng_ref, gnb_ref,
                  decay_ref, zeta_ref, xi_ref, wout_hbm, x_ref, wg_ref, wu_ref, wd_ref,
                  x1_ref, wgb_ref, wub_ref, wdb_ref,
                  state_ref, wc_ref, bias_ref, woutb_ref, o_ref, prev_ref, stage_ref, stage_sem,
                  *, chunks_per_step, gamma_c):
    step = pl.program_id(0)
    ffn_cols = wgb_ref.shape[2]
    for t in range(wgb_ref.shape[0]):
        wgb_ref[t] = wg_ref[:, t * ffn_cols:(t + 1) * ffn_cols].astype(BF16)
        wub_ref[t] = wu_ref[:, t * ffn_cols:(t + 1) * ffn_cols].astype(BF16)
    wdb_ref[...] = wd_ref[...].astype(BF16)
    row_id = lax.broadcasted_iota(jnp.int32, (CHUNK, CHUNK), 0)
    col_id = lax.broadcasted_iota(jnp.int32, (CHUNK, CHUNK), 1)

    @pl.when(step == 0)
    def _():
        state_ref[...] = jnp.zeros_like(state_ref)
        o_ref[...] = jnp.zeros_like(o_ref)
        stage_rows = stage_ref.shape[1]
        n_stage = woutb_ref.shape[0] // stage_rows

        def stage_copy(r):
            return pltpu.make_async_copy(wout_hbm.at[pl.ds(r * stage_rows, stage_rows), :],
                                         stage_ref.at[r % 2], stage_sem.at[r % 2])

        stage_copy(0).start()
        for r in range(n_stage):
            if r + 1 < n_stage:
                stage_copy(r + 1).start()
            stage_copy(r).wait()
            woutb_ref[r * stage_rows:(r + 1) * stage_rows, :] = stage_ref[r % 2].astype(BF16)
        for h in range(HEADS):
            wc_ref[h] = jnp.where(row_id >= col_id, ws_ref[h], 0.0).astype(BF16)
            b_col = jnp.sum(jnp.where(row_id == col_id, bs_ref[h:h + 1, :], 0.0), axis=1, keepdims=True)
            bias_ref[h] = jnp.broadcast_to(b_col, (CHUNK, HEAD_DIM))

    k_scale = HEAD_DIM ** -0.5
    chunk0 = jnp.minimum(step, pl.num_programs(0) - 2) * chunks_per_step
    heads = range(HEADS)

    prev_ref[...] = o_ref[...]
    piece_cols = x1_ref.shape[1] // N_PROJ_PIECES
    pieces_per_chunk = N_PROJ_PIECES // chunks_per_step
    piece_heads = [(k + 1) * HEADS // (pieces_per_chunk - 1) - 1 for k in range(pieces_per_chunk - 1)]
    issued = []

    def project_piece():
        n = slice(len(issued) * piece_cols, (len(issued) + 1) * piece_cols)
        issued.append(n)
        x1_ref[:, n] = x_ref[:, n] + jnp.dot(prev_ref[...], woutb_ref[:, n], preferred_element_type=F32)

    def cols(off, h):
        return slice(off + h * HEAD_DIM, off + (h + 1) * HEAD_DIM)

    for c in range(chunks_per_step):
        rows = slice(c * CHUNK, (c + 1) * CHUNK)
        project_piece()

        ca = ca_ref[pl.ds(chunk0 + c, 1), :]
        sa = sa_ref[pl.ds(chunk0 + c, 1), :]
        cos2 = ca * cb_ref[...] - sa * sb_ref[...]
        sin2 = sa * cbs_ref[...] + ca * sbs_ref[...]
        cos2k = cos2 * k_scale
        sin2k = sin2 * k_scale

        for h in heads:
            vn = _norm_rows(p_ref[rows, cols(OFF_VS, h)].astype(F32), lng_ref[h:h + 1, :], lnb_ref[h:h + 1, :])
            mixed = jnp.dot(wc_ref[h], vn.astype(BF16), preferred_element_type=F32) + bias_ref[h]
            u = p_ref[rows, cols(OFF_U, h)].astype(F32)
            o_ref[rows, cols(0, h)] = (u * mixed).astype(o_ref.dtype)

        outs = []
        for h in heads:
            q = p_ref[rows, cols(OFF_Q, h)].astype(F32)
            k = p_ref[rows, cols(OFF_K, h)].astype(F32)
            vr = p_ref[rows, cols(OFF_VR, h)]
            qb = (q * cos2 + pltpu.roll(q, HEAD_DIM // 2, axis=1) * sin2).astype(BF16)
            kr = k * cos2k + pltpu.roll(k, HEAD_DIM // 2, axis=1) * sin2k
            scores = lax.dot_general(qb, kr.astype(BF16), (((1,), (1,)), ((), ())),
                                     preferred_element_type=F32) * decay_ref[h]
            intra = jnp.dot(scores.astype(BF16), vr, preferred_element_type=F32)
            state = state_ref[h]
            inter = jnp.dot(qb, state.astype(BF16), preferred_element_type=F32) * xi_ref[h]
            kz = (kr * zeta_ref[h]).astype(BF16)
            kv = lax.dot_general(kz, vr, (((0,), (0,)), ((), ())), preferred_element_type=F32)
            state_ref[h] = gamma_c[h] * state + kv
            outs.append(intra + inter)
            if h in piece_heads:
                project_piece()

        for h in heads:
            on = _norm_rows(outs[h], gng_ref[h:h + 1, :], gnb_ref[h:h + 1, :])
            g = p_ref[rows, cols(OFF_G, h)].astype(F32)
            o_ref[rows, cols(SGU_WIDTH, h)] = (g * jax.nn.sigmoid(g) * on).astype(o_ref.dtype)
    assert len(issued) == N_PROJ_PIECES


def _mixer_constants(seq):
    half = HEAD_DIM // 2
    inv = 1.0 / (ROPE_BASE ** (np.arange(half, dtype=np.float64) / half))
    inv2 = np.concatenate([inv, inv])
    sign = np.concatenate([-np.ones(half), np.ones(half)])
    ang_a = (CHUNK * np.arange(seq // CHUNK, dtype=np.float64))[:, None] * inv2[None, :]
    ang_b = np.arange(CHUNK, dtype=np.float64)[:, None] * inv2[None, :]
    rot = [np.cos(ang_a), np.sin(ang_a), np.cos(ang_b), np.sin(ang_b),
           sign * np.cos(ang_b), sign * np.sin(ang_b)]

    log_gamma = np.log(1.0 - np.exp2(-5.0 - np.arange(HEADS, dtype=np.float64)))
    idx = np.arange(CHUNK, dtype=np.float64)
    diff = idx[:, None] - idx[None, :]
    decay = np.where(diff[None] >= 0, np.exp(np.maximum(diff, 0.0)[None] * log_gamma[:, None, None]), 0.0)
    zeta = np.exp((CHUNK - 1.0 - idx)[None, :] * log_gamma[:, None])
    xi = np.exp((idx + 1.0)[None, :] * log_gamma[:, None])
    bcast = lambda v: np.broadcast_to(v[:, :, None], (HEADS, CHUNK, HEAD_DIM))
    gamma_c = tuple(float(np.float32(v)) for v in np.exp(CHUNK * log_gamma))
    tables = [jnp.asarray(np.ascontiguousarray(t), dtype=F32) for t in rot + [decay, bcast(zeta), bcast(xi)]]
    return tables, gamma_c


def _mixer_out_proj(proj, lng, lnb, ws, bs, gng, gnb, w_out, x, w_gate, w_up, w_down, *, chunks_per_step, ffn_cols):
    s, d = x.shape
    dff = w_gate.shape[1]
    n_ffn_tiles = dff // ffn_cols
    n_chunks = s // CHUNK
    rows = chunks_per_step * CHUNK
    n_tiles = s // rows
    gate_rows, down_rows = d // n_tiles, dff // n_tiles
    cast_block = lambda i: (jnp.minimum(i, n_tiles - 1), 0)
    tiled_cast_block = lambda i: (0, jnp.minimum(i, n_tiles - 1), 0)
    (ca, sa, cb, sb, cbs, sbs, decay, zeta, xi), gamma_c = _mixer_constants(s)
    full = lambda shape, **kw: pl.BlockSpec(shape, lambda i: (0,) * len(shape), **kw)
    head_tiles = full((HEADS, CHUNK, HEAD_DIM))
    head_rows = full((HEADS, HEAD_DIM))
    mixed_tile = lambda i: (jnp.minimum(i, n_tiles - 1), 0)
    projected_tile = lambda i: (jnp.maximum(i - 1, 0), 0)
    return pl.pallas_call(
        functools.partial(_mixer_kernel, chunks_per_step=chunks_per_step, gamma_c=gamma_c),
        grid=(n_tiles + 1,),
        in_specs=[
            pl.BlockSpec((rows, IN_WIDTH), mixed_tile),
            full((n_chunks, HEAD_DIM)), full((n_chunks, HEAD_DIM)),
            full((CHUNK, HEAD_DIM)), full((CHUNK, HEAD_DIM)), full((CHUNK, HEAD_DIM)), full((CHUNK, HEAD_DIM)),
            head_rows, head_rows, head_tiles, head_rows, head_rows, head_rows,
            head_tiles, head_tiles, head_tiles,
            pl.BlockSpec(memory_space=pl.ANY),
            pl.BlockSpec((rows, d), projected_tile),
            pl.BlockSpec((gate_rows, dff), cast_block),
            pl.BlockSpec((gate_rows, dff), cast_block),
            pl.BlockSpec((down_rows, d), cast_block),
        ],
        out_specs=[pl.BlockSpec((rows, d), projected_tile),
                   pl.BlockSpec((n_ffn_tiles, gate_rows, ffn_cols), tiled_cast_block),
                   pl.BlockSpec((n_ffn_tiles, gate_rows, ffn_cols), tiled_cast_block),
                   pl.BlockSpec((down_rows, d), cast_block)],
        out_shape=[jax.ShapeDtypeStruct((s, d), F32),
                   jax.ShapeDtypeStruct((n_ffn_tiles, d, ffn_cols), BF16),
                   jax.ShapeDtypeStruct((n_ffn_tiles, d, ffn_cols), BF16),
                   jax.ShapeDtypeStruct(w_down.shape, BF16)],
        scratch_shapes=[pltpu.VMEM((HEADS, HEAD_DIM, HEAD_DIM), F32),
                        pltpu.VMEM((HEADS, CHUNK, CHUNK), BF16),
                        pltpu.VMEM((HEADS, CHUNK, HEAD_DIM), F32),
                        pltpu.VMEM((MIX_WIDTH, d), BF16),
                        pltpu.VMEM((rows, MIX_WIDTH), BF16),
                        pltpu.VMEM((rows, MIX_WIDTH), BF16),
                        pltpu.VMEM((2, W_OUT_STAGE_ROWS, d), F32),
                        pltpu.SemaphoreType.DMA((2,))],
        compiler_params=pltpu.CompilerParams(
            dimension_semantics=("arbitrary",),
            vmem_limit_bytes=V7X_VMEM_LIMIT_BYTES),
        name="mixer_out_proj",
    )(proj, ca, sa, cb, sb, cbs, sbs, lng, lnb, ws, bs, gng, gnb, decay, zeta, xi, w_out, x,
      w_gate, w_up, w_down)


def _ffn_kernel(x_hbm, g2_ref, wg_ref, wu_ref, wd_ref, gf_ref, o_ref, h_ref, x_ref, x_sem, *, final_norm):
    f = pl.program_id(1)
    last = pl.num_programs(1) - 1

    def ffn_step(is_first, is_last):
        if is_first:
            base = x_ref[...]
            h = _rms_rows(base, g2_ref[...]).astype(BF16)
            h_ref[...] = h
        else:
            base = o_ref[...]
            h = h_ref[...]
        gate = jnp.dot(h, wg_ref[...], preferred_element_type=F32)
        up = jnp.dot(h, wu_ref[...], preferred_element_type=F32)
        a = (gate * jax.nn.sigmoid(gate) * up).astype(BF16)
        y = base + jnp.dot(a, wd_ref[...], preferred_element_type=F32)
        o_ref[...] = _rms_rows(y, gf_ref[...]) if (is_last and final_norm) else y

    _prefetched_row_tile(x_hbm, x_ref, x_sem, functools.partial(ffn_step, True, False))
    pl.when((f > 0) & (f < last))(functools.partial(ffn_step, False, False))
    pl.when(f == last)(functools.partial(ffn_step, False, True))


def _ffn(x1, g2, wg, wu, wd, gf, *, tm, final_norm):
    s, d = x1.shape
    n_f, _, tf = wg.shape
    assert n_f >= 2, "first and last inner steps are distinct code paths"
    return pl.pallas_call(
        functools.partial(_ffn_kernel, final_norm=final_norm),
        grid=(s // tm, n_f),
        in_specs=[
            pl.BlockSpec(memory_space=pl.ANY),
            pl.BlockSpec((1, d), lambda i, f: (0, 0)),
            pl.BlockSpec((None, d, tf), lambda i, f: (f, 0, 0)),
            pl.BlockSpec((None, d, tf), lambda i, f: (f, 0, 0)),
            pl.BlockSpec((tf, d), lambda i, f: (f, 0)),
            pl.BlockSpec((1, d), lambda i, f: (0, 0)),
        ],
        out_specs=pl.BlockSpec((tm, d), lambda i, f: (i, 0)),
        out_shape=jax.ShapeDtypeStruct((s, d), F32),
        scratch_shapes=[pltpu.VMEM((tm, d), BF16), pltpu.VMEM((tm, d), F32), pltpu.SemaphoreType.DMA(())],
        compiler_params=pltpu.CompilerParams(
            dimension_semantics=("arbitrary", "arbitrary"),
            vmem_limit_bytes=V7X_VMEM_LIMIT_BYTES),
        name="ffn",
    )(x1, g2, wg, wu, wd, gf)


def kernel(x, norm1_g, w_in, sgu_ln_g, sgu_ln_b, w_spatial, b_spatial, ret_gn_g, ret_gn_b,
           w_out, norm2_g, w_gate, w_up, w_down, final_norm_g):
    batch, seq, d = x.shape
    depth = w_in.shape[0]
    outs = []
    for b in range(batch):
        xb = x[b]
        for l in range(depth):
            proj = _in_proj(xb, norm1_g[l][None, :], w_in[l],
                            tm=IN_PROJ_ROWS, tn=IN_PROJ_COLS, n_split=IN_PROJ_DOTS_PER_STEP)
            x1, wg, wu, wd = _mixer_out_proj(proj, sgu_ln_g[l], sgu_ln_b[l], w_spatial[l], b_spatial[l],
                                             ret_gn_g[l], ret_gn_b[l], w_out[l], xb,
                                             w_gate[l], w_up[l], w_down[l],
                                             chunks_per_step=MIXER_CHUNKS_PER_STEP, ffn_cols=FFN_COLS)
            xb = _ffn(x1, norm2_g[l][None, :], wg, wu, wd, final_norm_g[None, :],
                      tm=FFN_ROWS, final_norm=(l == depth - 1))
        outs.append(xb)
    return outs[0][None] if batch == 1 else jnp.stack(outs)
```

```python
import functools

import numpy as np

import jax
import jax.numpy as jnp
from jax import lax
from jax.experimental import pallas as pl
from jax.experimental.pallas import tpu as pltpu

D_MODEL = 2048
CHUNK = 128
HEADS = 8
HEAD_DIM = 128
SGU_WIDTH = HEADS * HEAD_DIM
RET_WIDTH = HEADS * HEAD_DIM
MIX_WIDTH = SGU_WIDTH + RET_WIDTH
IN_WIDTH = 2 * SGU_WIDTH + 4 * RET_WIDTH
ROPE_BASE = 10000.0
EPS = 1e-6

OFF_U, OFF_VS, OFF_Q, OFF_K, OFF_VR, OFF_G = (i * SGU_WIDTH for i in range(6))

V7X_VMEM_LIMIT_BYTES = 60 * 1024 * 1024

N_PROJ_PIECES = 8
W_OUT_STAGE_ROWS = 256

IN_PROJ_ROWS, IN_PROJ_COLS, IN_PROJ_DOTS_PER_STEP = 2048, 1024, 2
MIXER_CHUNKS_PER_STEP = 2
FFN_ROWS, FFN_COLS = 1024, 512

F32 = jnp.float32
BF16 = jnp.bfloat16


def _rms_rows(x, g):
    ms = jnp.mean(x * x, axis=-1, keepdims=True)
    return x * lax.rsqrt(ms + EPS) * g


def _row_tile_copy(x_hbm, x_ref, sem, tile):
    tm = x_ref.shape[0]
    start = pl.multiple_of(tile * tm, tm)
    return pltpu.make_async_copy(x_hbm.at[pl.ds(start, tm), :], x_ref, sem)


def _prefetched_row_tile(x_hbm, x_ref, sem, consume):
    i, j = pl.program_id(0), pl.program_id(1)

    @pl.when((i == 0) & (j == 0))
    def _():
        _row_tile_copy(x_hbm, x_ref, sem, 0).start()

    @pl.when(j == 0)
    def _():
        _row_tile_copy(x_hbm, x_ref, sem, i).wait()
        consume()

    @pl.when((j == 1) & (i + 1 < pl.num_programs(0)))
    def _():
        _row_tile_copy(x_hbm, x_ref, sem, i + 1).start()


def _in_proj_kernel(x_hbm, g_ref, w_ref, o_ref, h_ref, x_ref, x_sem, *, n_split):
    def normalise():
        h_ref[...] = _rms_rows(x_ref[...], g_ref[...]).astype(BF16)

    _prefetched_row_tile(x_hbm, x_ref, x_sem, normalise)
    cols = o_ref.shape[1] // n_split
    for s in range(n_split):
        n = slice(s * cols, (s + 1) * cols)
        o_ref[:, n] = jnp.dot(h_ref[...], w_ref[:, n].astype(BF16),
                              preferred_element_type=F32).astype(o_ref.dtype)


def _in_proj(x, g, w, *, tm, tn, n_split):
    s, d = x.shape
    n = w.shape[1]
    assert n // tn >= 2, "the row-tile prefetch starts at inner step 1"
    return pl.pallas_call(
        functools.partial(_in_proj_kernel, n_split=n_split),
        grid=(s // tm, n // tn),
        in_specs=[
            pl.BlockSpec(memory_space=pl.ANY),
            pl.BlockSpec((1, d), lambda i, j: (0, 0)),
            pl.BlockSpec((d, tn), lambda i, j: (0, j)),
        ],
        out_specs=pl.BlockSpec((tm, tn), lambda i, j: (i, j)),
        out_shape=jax.ShapeDtypeStruct((s, n), BF16),
        scratch_shapes=[pltpu.VMEM((tm, d), BF16), pltpu.VMEM((tm, d), F32), pltpu.SemaphoreType.DMA(())],
        compiler_params=pltpu.CompilerParams(
            dimension_semantics=("arbitrary", "arbitrary"),
            vmem_limit_bytes=V7X_VMEM_LIMIT_BYTES),
        name="in_proj",
    )(x, g, w)


def _norm_rows(x, g, b):
    mu = jnp.mean(x, axis=-1, keepdims=True)
    d = x - mu
    var = jnp.mean(d * d, axis=-1, keepdims=True)
    return d * lax.rsqrt(var + EPS) * g + b


def _mixer_kernel(p_ref, ca_ref, sa_ref, cb_ref, sb_ref, cbs_ref, sbs_ref,
                  lng_ref, lnb_ref, ws_ref, bs_ref, gng_ref, gnb_ref,
                  decay_ref, zeta_ref, xi_ref, wout_hbm, x_ref, wg_hbm, wu_hbm, wd_hbm,
                  x1_ref, wgb_ref, wub_ref, wdb_ref,
                  state_ref, wc_ref, bias_ref, woutb_ref, o_ref, prev_ref, stage_ref, stage_sem,
                  wg_stage, wu_stage, wd_stage, ffn_sem, *, chunks_per_step, gamma_c):
    step = pl.program_id(0)
    last_block = pl.num_programs(0) - 2
    ffn_weights = ((wg_hbm, wg_stage, wgb_ref), (wu_hbm, wu_stage, wub_ref), (wd_hbm, wd_stage, wdb_ref))

    def ffn_weight_copies(at_step):
        block = jnp.minimum(at_step, last_block)
        slot = at_step % 2
        copies = []
        for k, (w_hbm, stage, _) in enumerate(ffn_weights):
            rows = stage.shape[1]
            start = pl.multiple_of(block * rows, 8)
            copies.append(pltpu.make_async_copy(w_hbm.at[pl.ds(start, rows), :], stage.at[slot], ffn_sem.at[k, slot]))
        return copies

    row_id = lax.broadcasted_iota(jnp.int32, (CHUNK, CHUNK), 0)
    col_id = lax.broadcasted_iota(jnp.int32, (CHUNK, CHUNK), 1)

    @pl.when(step == 0)
    def _():
        for copy in ffn_weight_copies(0):
            copy.start()
        state_ref[...] = jnp.zeros_like(state_ref)
        o_ref[...] = jnp.zeros_like(o_ref)
        stage_rows = stage_ref.shape[1]
        n_stage = woutb_ref.shape[0] // stage_rows

        def stage_copy(r):
            return pltpu.make_async_copy(wout_hbm.at[pl.ds(r * stage_rows, stage_rows), :],
                                         stage_ref.at[r % 2], stage_sem.at[r % 2])

        stage_copy(0).start()
        for r in range(n_stage):
            if r + 1 < n_stage:
                stage_copy(r + 1).start()
            stage_copy(r).wait()
            woutb_ref[r * stage_rows:(r + 1) * stage_rows, :] = stage_ref[r % 2].astype(BF16)
        for h in range(HEADS):
            wc_ref[h] = jnp.where(row_id >= col_id, ws_ref[h], 0.0).astype(BF16)
            b_col = jnp.sum(jnp.where(row_id == col_id, bs_ref[h:h + 1, :], 0.0), axis=1, keepdims=True)
            bias_ref[h] = jnp.broadcast_to(b_col, (CHUNK, HEAD_DIM))

    @pl.when(step + 1 < pl.num_programs(0))
    def _():
        for copy in ffn_weight_copies(step + 1):
            copy.start()

    k_scale = HEAD_DIM ** -0.5
    chunk0 = jnp.minimum(step, last_block) * chunks_per_step
    heads = range(HEADS)

    prev_ref[...] = o_ref[...]
    piece_cols = x1_ref.shape[1] // N_PROJ_PIECES
    pieces_per_chunk = N_PROJ_PIECES // chunks_per_step
    piece_heads = [(k + 1) * HEADS // (pieces_per_chunk - 1) - 1 for k in range(pieces_per_chunk - 1)]
    issued = []

    def project_piece():
        n = slice(len(issued) * piece_cols, (len(issued) + 1) * piece_cols)
        issued.append(n)
        x1_ref[:, n] = x_ref[:, n] + jnp.dot(prev_ref[...], woutb_ref[:, n], preferred_element_type=F32)

    def cols(off, h):
        return slice(off + h * HEAD_DIM, off + (h + 1) * HEAD_DIM)

    for c in range(chunks_per_step):
        rows = slice(c * CHUNK, (c + 1) * CHUNK)
        project_piece()

        ca = ca_ref[pl.ds(chunk0 + c, 1), :]
        sa = sa_ref[pl.ds(chunk0 + c, 1), :]
        cos2 = ca * cb_ref[...] - sa * sb_ref[...]
        sin2 = sa * cbs_ref[...] + ca * sbs_ref[...]
        cos2k = cos2 * k_scale
        sin2k = sin2 * k_scale

        for h in heads:
            vn = _norm_rows(p_ref[rows, cols(OFF_VS, h)].astype(F32), lng_ref[h:h + 1, :], lnb_ref[h:h + 1, :])
            mixed = jnp.dot(wc_ref[h], vn.astype(BF16), preferred_element_type=F32) + bias_ref[h]
            u = p_ref[rows, cols(OFF_U, h)].astype(F32)
            o_ref[rows, cols(0, h)] = (u * mixed).astype(o_ref.dtype)

        outs = []
        for h in heads:
            q = p_ref[rows, cols(OFF_Q, h)].astype(F32)
            k = p_ref[rows, cols(OFF_K, h)].astype(F32)
            vr = p_ref[rows, cols(OFF_VR, h)]
            qb = (q * cos2 + pltpu.roll(q, HEAD_DIM // 2, axis=1) * sin2).astype(BF16)
            kr = k * cos2k + pltpu.roll(k, HEAD_DIM // 2, axis=1) * sin2k
            scores = lax.dot_general(qb, kr.astype(BF16), (((1,), (1,)), ((), ())),
                                     preferred_element_type=F32) * decay_ref[h]
            intra = jnp.dot(scores.astype(BF16), vr, preferred_element_type=F32)
            state = state_ref[h]
            inter = jnp.dot(qb, state.astype(BF16), preferred_element_type=F32) * xi_ref[h]
            kz = (kr * zeta_ref[h]).astype(BF16)
            kv = lax.dot_general(kz, vr, (((0,), (0,)), ((), ())), preferred_element_type=F32)
            state_ref[h] = gamma_c[h] * state + kv
            outs.append(intra + inter)
            if h in piece_heads:
                project_piece()

        for h in heads:
            on = _norm_rows(outs[h], gng_ref[h:h + 1, :], gnb_ref[h:h + 1, :])
            g = p_ref[rows, cols(OFF_G, h)].astype(F32)
            o_ref[rows, cols(SGU_WIDTH, h)] = (g * jax.nn.sigmoid(g) * on).astype(o_ref.dtype)
    assert len(issued) == N_PROJ_PIECES

    for copy, (_, stage, out_ref) in zip(ffn_weight_copies(step), ffn_weights):
        copy.wait()
        out_ref[...] = stage[step % 2].astype(BF16)


def _mixer_constants(seq):
    half = HEAD_DIM // 2
    inv = 1.0 / (ROPE_BASE ** (np.arange(half, dtype=np.float64) / half))
    inv2 = np.concatenate([inv, inv])
    sign = np.concatenate([-np.ones(half), np.ones(half)])
    ang_a = (CHUNK * np.arange(seq // CHUNK, dtype=np.float64))[:, None] * inv2[None, :]
    ang_b = np.arange(CHUNK, dtype=np.float64)[:, None] * inv2[None, :]
    rot = [np.cos(ang_a), np.sin(ang_a), np.cos(ang_b), np.sin(ang_b),
           sign * np.cos(ang_b), sign * np.sin(ang_b)]

    log_gamma = np.log(1.0 - np.exp2(-5.0 - np.arange(HEADS, dtype=np.float64)))
    idx = np.arange(CHUNK, dtype=np.float64)
    diff = idx[:, None] - idx[None, :]
    decay = np.where(diff[None] >= 0, np.exp(np.maximum(diff, 0.0)[None] * log_gamma[:, None, None]), 0.0)
    zeta = np.exp((CHUNK - 1.0 - idx)[None, :] * log_gamma[:, None])
    xi = np.exp((idx + 1.0)[None, :] * log_gamma[:, None])
    bcast = lambda v: np.broadcast_to(v[:, :, None], (HEADS, CHUNK, HEAD_DIM))
    gamma_c = tuple(float(np.float32(v)) for v in np.exp(CHUNK * log_gamma))
    tables = [jnp.asarray(np.ascontiguousarray(t), dtype=F32) for t in rot + [decay, bcast(zeta), bcast(xi)]]
    return tables, gamma_c


def _mixer_out_proj(proj, lng, lnb, ws, bs, gng, gnb, w_out, x, w_gate, w_up, w_down, *, chunks_per_step):
    s, d = x.shape
    dff = w_gate.shape[1]
    n_chunks = s // CHUNK
    rows = chunks_per_step * CHUNK
    n_tiles = s // rows
    gate_rows, down_rows = d // n_tiles, dff // n_tiles
    cast_block = lambda i: (jnp.minimum(i, n_tiles - 1), 0)
    (ca, sa, cb, sb, cbs, sbs, decay, zeta, xi), gamma_c = _mixer_constants(s)
    full = lambda shape, **kw: pl.BlockSpec(shape, lambda i: (0,) * len(shape), **kw)
    head_tiles = full((HEADS, CHUNK, HEAD_DIM))
    head_rows = full((HEADS, HEAD_DIM))
    mixed_tile = lambda i: (jnp.minimum(i, n_tiles - 1), 0)
    projected_tile = lambda i: (jnp.maximum(i - 1, 0), 0)
    return pl.pallas_call(
        functools.partial(_mixer_kernel, chunks_per_step=chunks_per_step, gamma_c=gamma_c),
        grid=(n_tiles + 1,),
        in_specs=[
            pl.BlockSpec((rows, IN_WIDTH), mixed_tile),
            full((n_chunks, HEAD_DIM)), full((n_chunks, HEAD_DIM)),
            full((CHUNK, HEAD_DIM)), full((CHUNK, HEAD_DIM)), full((CHUNK, HEAD_DIM)), full((CHUNK, HEAD_DIM)),
            head_rows, head_rows, head_tiles, head_rows, head_rows, head_rows,
            head_tiles, head_tiles, head_tiles,
            pl.BlockSpec(memory_space=pl.ANY),
            pl.BlockSpec((rows, d), projected_tile),
            pl.BlockSpec(memory_space=pl.ANY),
            pl.BlockSpec(memory_space=pl.ANY),
            pl.BlockSpec(memory_space=pl.ANY),
        ],
        out_specs=[pl.BlockSpec((rows, d), projected_tile),
                   pl.BlockSpec((gate_rows, dff), cast_block),
                   pl.BlockSpec((gate_rows, dff), cast_block),
                   pl.BlockSpec((down_rows, d), cast_block)],
        out_shape=[jax.ShapeDtypeStruct((s, d), F32),
                   jax.ShapeDtypeStruct(w_gate.shape, BF16),
                   jax.ShapeDtypeStruct(w_up.shape, BF16),
                   jax.ShapeDtypeStruct(w_down.shape, BF16)],
        scratch_shapes=[pltpu.VMEM((HEADS, HEAD_DIM, HEAD_DIM), F32),
                        pltpu.VMEM((HEADS, CHUNK, CHUNK), BF16),
                        pltpu.VMEM((HEADS, CHUNK, HEAD_DIM), F32),
                        pltpu.VMEM((MIX_WIDTH, d), BF16),
                        pltpu.VMEM((rows, MIX_WIDTH), BF16),
                        pltpu.VMEM((rows, MIX_WIDTH), BF16),
                        pltpu.VMEM((2, W_OUT_STAGE_ROWS, d), F32),
                        pltpu.SemaphoreType.DMA((2,)),
                        pltpu.VMEM((2, gate_rows, dff), F32),
                        pltpu.VMEM((2, gate_rows, dff), F32),
                        pltpu.VMEM((2, down_rows, d), F32),
                        pltpu.SemaphoreType.DMA((3, 2))],
        compiler_params=pltpu.CompilerParams(
            dimension_semantics=("arbitrary",),
            vmem_limit_bytes=V7X_VMEM_LIMIT_BYTES),
        name="mixer_out_proj",
    )(proj, ca, sa, cb, sb, cbs, sbs, lng, lnb, ws, bs, gng, gnb, decay, zeta, xi, w_out, x,
      w_gate, w_up, w_down)


def _ffn_kernel(x_hbm, g2_ref, wg_ref, wu_ref, wd_ref, gf_ref, o_ref, h_ref, x_ref, x_sem, *, final_norm):
    f = pl.program_id(1)
    last = pl.num_programs(1) - 1

    def ffn_step(is_first, is_last):
        if is_first:
            base = x_ref[...]
            h = _rms_rows(base, g2_ref[...]).astype(BF16)
            h_ref[...] = h
        else:
            base = o_ref[...]
            h = h_ref[...]
        gate = jnp.dot(h, wg_ref[...], preferred_element_type=F32)
        up = jnp.dot(h, wu_ref[...], preferred_element_type=F32)
        a = (gate * jax.nn.sigmoid(gate) * up).astype(BF16)
        y = base + jnp.dot(a, wd_ref[...], preferred_element_type=F32)
        o_ref[...] = _rms_rows(y, gf_ref[...]) if (is_last and final_norm) else y

    _prefetched_row_tile(x_hbm, x_ref, x_sem, functools.partial(ffn_step, True, False))
    pl.when((f > 0) & (f < last))(functools.partial(ffn_step, False, False))
    pl.when(f == last)(functools.partial(ffn_step, False, True))


def _ffn(x1, g2, wg, wu, wd, gf, *, tm, tf, final_norm):
    s, d = x1.shape
    dff = wg.shape[1]
    assert dff // tf >= 2, "first and last inner steps are distinct code paths"
    return pl.pallas_call(
        functools.partial(_ffn_kernel, final_norm=final_norm),
        grid=(s // tm, dff // tf),
        in_specs=[
            pl.BlockSpec(memory_space=pl.ANY),
            pl.BlockSpec((1, d), lambda i, f: (0, 0)),
            pl.BlockSpec((d, tf), lambda i, f: (0, f)),
            pl.BlockSpec((d, tf), lambda i, f: (0, f)),
            pl.BlockSpec((tf, d), lambda i, f: (f, 0)),
            pl.BlockSpec((1, d), lambda i, f: (0, 0)),
        ],
        out_specs=pl.BlockSpec((tm, d), lambda i, f: (i, 0)),
        out_shape=jax.ShapeDtypeStruct((s, d), F32),
        scratch_shapes=[pltpu.VMEM((tm, d), BF16), pltpu.VMEM((tm, d), F32), pltpu.SemaphoreType.DMA(())],
        compiler_params=pltpu.CompilerParams(
            dimension_semantics=("arbitrary", "arbitrary"),
            vmem_limit_bytes=V7X_VMEM_LIMIT_BYTES),
        name="ffn",
    )(x1, g2, wg, wu, wd, gf)


def kernel(x, norm1_g, w_in, sgu_ln_g, sgu_ln_b, w_spatial, b_spatial, ret_gn_g, ret_gn_b,
           w_out, norm2_g, w_gate, w_up, w_down, final_norm_g):
    batch, seq, d = x.shape
    depth = w_in.shape[0]
    outs = []
    for b in range(batch):
        xb = x[b]
        for l in range(depth):
            proj = _in_proj(xb, norm1_g[l][None, :], w_in[l],
                            tm=IN_PROJ_ROWS, tn=IN_PROJ_COLS, n_split=IN_PROJ_DOTS_PER_STEP)
            x1, wg, wu, wd = _mixer_out_proj(proj, sgu_ln_g[l], sgu_ln_b[l], w_spatial[l], b_spatial[l],
                                             ret_gn_g[l], ret_gn_b[l], w_out[l], xb,
                                             w_gate[l], w_up[l], w_down[l],
                                             chunks_per_step=MIXER_CHUNKS_PER_STEP)
            xb = _ffn(x1, norm2_g[l][None, :], wg, wu, wd, final_norm_g[None, :],
                      tm=FFN_ROWS, tf=FFN_COLS, final_norm=(l == depth - 1))
        outs.append(xb)
    return outs[0][None] if batch == 1 else jnp.stack(outs)
```

```python
import functools

import numpy as np

import jax
import jax.numpy as jnp
from jax import lax
from jax.experimental import pallas as pl
from jax.experimental.pallas import tpu as pltpu

D_MODEL = 2048
CHUNK = 128
HEADS = 8
HEAD_DIM = 128
SGU_WIDTH = HEADS * HEAD_DIM
RET_WIDTH = HEADS * HEAD_DIM
MIX_WIDTH = SGU_WIDTH + RET_WIDTH
IN_WIDTH = 2 * SGU_WIDTH + 4 * RET_WIDTH
ROPE_BASE = 10000.0
EPS = 1e-6

OFF_U, OFF_VS, OFF_Q, OFF_K, OFF_VR, OFF_G = (i * SGU_WIDTH for i in range(6))

V7X_VMEM_LIMIT_BYTES = 60 * 1024 * 1024

N_PROJ_PIECES = 8
W_OUT_STAGE_ROWS = 256

IN_PROJ_ROWS, IN_PROJ_COLS, IN_PROJ_DOTS_PER_STEP = 2048, 1024, 2
MIXER_CHUNKS_PER_STEP = 2
FFN_ROWS, FFN_COLS = 1024, 512

F32 = jnp.float32
BF16 = jnp.bfloat16


def _rms_rows(x, g):
    ms = jnp.mean(x * x, axis=-1, keepdims=True)
    return x * lax.rsqrt(ms + EPS) * g


def _row_tile_copy(x_hbm, x_ref, sem, tile):
    tm = x_ref.shape[0]
    start = pl.multiple_of(tile * tm, tm)
    return pltpu.make_async_copy(x_hbm.at[pl.ds(start, tm), :], x_ref, sem)


def _prefetched_row_tile(x_hbm, x_ref, sem, consume):
    i, j = pl.program_id(0), pl.program_id(1)

    @pl.when((i == 0) & (j == 0))
    def _():
        _row_tile_copy(x_hbm, x_ref, sem, 0).start()

    @pl.when(j == 0)
    def _():
        _row_tile_copy(x_hbm, x_ref, sem, i).wait()
        consume()

    @pl.when((j == 1) & (i + 1 < pl.num_programs(0)))
    def _():
        _row_tile_copy(x_hbm, x_ref, sem, i + 1).start()


def _in_proj_kernel(x_hbm, g_ref, w_ref, o_ref, h_ref, x_ref, x_sem, *, n_split):
    def normalise():
        h_ref[...] = _rms_rows(x_ref[...], g_ref[...]).astype(BF16)

    _prefetched_row_tile(x_hbm, x_ref, x_sem, normalise)
    cols = o_ref.shape[1] // n_split
    for s in range(n_split):
        n = slice(s * cols, (s + 1) * cols)
        o_ref[:, n] = jnp.dot(h_ref[...], w_ref[:, n].astype(BF16),
                              preferred_element_type=F32).astype(o_ref.dtype)


def _in_proj(x, g, w, *, tm, tn, n_split):
    s, d = x.shape
    n = w.shape[1]
    assert n // tn >= 2, "the row-tile prefetch starts at inner step 1"
    return pl.pallas_call(
        functools.partial(_in_proj_kernel, n_split=n_split),
        grid=(s // tm, n // tn),
        in_specs=[
            pl.BlockSpec(memory_space=pl.ANY),
            pl.BlockSpec((1, d), lambda i, j: (0, 0)),
            pl.BlockSpec((d, tn), lambda i, j: (0, j)),
        ],
        out_specs=pl.BlockSpec((tm, tn), lambda i, j: (i, j)),
        out_shape=jax.ShapeDtypeStruct((s, n), BF16),
        scratch_shapes=[pltpu.VMEM((tm, d), BF16), pltpu.VMEM((tm, d), F32), pltpu.SemaphoreType.DMA(())],
        compiler_params=pltpu.CompilerParams(
            dimension_semantics=("arbitrary", "arbitrary"),
            vmem_limit_bytes=V7X_VMEM_LIMIT_BYTES),
        name="in_proj",
    )(x, g, w)


def _norm_rows(x, g, b):
    mu = jnp.mean(x, axis=-1, keepdims=True)
    d = x - mu
    var = jnp.mean(d * d, axis=-1, keepdims=True)
    return d * lax.rsqrt(var + EPS) * g + b


def _mixer_kernel(p_ref, ca_ref, sa_ref, cb_ref, sb_ref, cbs_ref, sbs_ref,
                  lng_ref, lnb_ref, ws_ref, bs_ref, gng_ref, gnb_ref,
                  decay_ref, zeta_ref, xi_ref, wout_hbm, x_ref, wg_hbm, wu_hbm, wd_hbm,
                  x1_ref, wgb_ref, wub_ref, wdb_ref,
                  state_ref, wc_ref, bias_ref, woutb_ref, o_ref, prev_ref, stage_ref, stage_sem,
                  wg_stage, wu_stage, wd_stage, ffn_sem, *, chunks_per_step, gamma_c):
    step = pl.program_id(0)
    last_block = pl.num_programs(0) - 2
    ffn_weights = ((wg_hbm, wg_stage, wgb_ref), (wu_hbm, wu_stage, wub_ref), (wd_hbm, wd_stage, wdb_ref))

    def ffn_weight_copies(at_step):
        block = jnp.minimum(at_step, last_block)
        slot = at_step % 2
        copies = []
        for k, (w_hbm, stage, _) in enumerate(ffn_weights):
            rows = stage.shape[1]
            start = pl.multiple_of(block * rows, 8)
            copies.append(pltpu.make_async_copy(w_hbm.at[pl.ds(start, rows), :], stage.at[slot], ffn_sem.at[k, slot]))
        return copies

    row_id = lax.broadcasted_iota(jnp.int32, (CHUNK, CHUNK), 0)
    col_id = lax.broadcasted_iota(jnp.int32, (CHUNK, CHUNK), 1)

    @pl.when(step == 0)
    def _():
        for copy in ffn_weight_copies(0):
            copy.start()
        state_ref[...] = jnp.zeros_like(state_ref)
        o_ref[...] = jnp.zeros_like(o_ref)
        stage_rows = stage_ref.shape[1]
        n_stage = woutb_ref.shape[0] // stage_rows

        def stage_copy(r):
            return pltpu.make_async_copy(wout_hbm.at[pl.ds(r * stage_rows, stage_rows), :],
                                         stage_ref.at[r % 2], stage_sem.at[r % 2])

        stage_copy(0).start()
        for r in range(n_stage):
            if r + 1 < n_stage:
                stage_copy(r + 1).start()
            stage_copy(r).wait()
            woutb_ref[r * stage_rows:(r + 1) * stage_rows, :] = stage_ref[r % 2].astype(BF16)
        for h in range(HEADS):
            wc_ref[h] = jnp.where(row_id >= col_id, ws_ref[h], 0.0).astype(BF16)
            b_col = jnp.sum(jnp.where(row_id == col_id, bs_ref[h:h + 1, :], 0.0), axis=1, keepdims=True)
            bias_ref[h] = jnp.broadcast_to(b_col, (CHUNK, HEAD_DIM))

    @pl.when(step + 1 < pl.num_programs(0))
    def _():
        for copy in ffn_weight_copies(step + 1):
            copy.start()

    k_scale = HEAD_DIM ** -0.5
    chunk0 = jnp.minimum(step, last_block) * chunks_per_step
    heads = range(HEADS)

    prev_ref[...] = o_ref[...]
    piece_cols = x1_ref.shape[1] // N_PROJ_PIECES
    pieces_per_chunk = N_PROJ_PIECES // chunks_per_step
    piece_heads = [(k + 1) * HEADS // (pieces_per_chunk - 1) - 1 for k in range(pieces_per_chunk - 1)]
    issued = []

    def project_piece():
        n = slice(len(issued) * piece_cols, (len(issued) + 1) * piece_cols)
        issued.append(n)
        x1_ref[:, n] = x_ref[:, n] + jnp.dot(prev_ref[...], woutb_ref[:, n], preferred_element_type=F32)

    def cols(off, h):
        return slice(off + h * HEAD_DIM, off + (h + 1) * HEAD_DIM)

    for c in range(chunks_per_step):
        rows = slice(c * CHUNK, (c + 1) * CHUNK)
        project_piece()

        ca = ca_ref[pl.ds(chunk0 + c, 1), :]
        sa = sa_ref[pl.ds(chunk0 + c, 1), :]
        cos2 = ca * cb_ref[...] - sa * sb_ref[...]
        sin2 = sa * cbs_ref[...] + ca * sbs_ref[...]
        cos2k = cos2 * k_scale
        sin2k = sin2 * k_scale

        vn = [_norm_rows(p_ref[rows, cols(OFF_VS, h)].astype(F32), lng_ref[h:h + 1, :], lnb_ref[h:h + 1, :])
              .astype(BF16) for h in heads]
        mixed = [jnp.dot(wc_ref[h], vn[h], preferred_element_type=F32) for h in heads]
        for h in heads:
            u = p_ref[rows, cols(OFF_U, h)].astype(F32)
            o_ref[rows, cols(0, h)] = (u * (mixed[h] + bias_ref[h])).astype(o_ref.dtype)
        project_piece()

        vr = [p_ref[rows, cols(OFF_VR, h)] for h in heads]
        qb, kr = [], []
        for h in heads:
            q = p_ref[rows, cols(OFF_Q, h)].astype(F32)
            k = p_ref[rows, cols(OFF_K, h)].astype(F32)
            qb.append((q * cos2 + pltpu.roll(q, HEAD_DIM // 2, axis=1) * sin2).astype(BF16))
            kr.append(k * cos2k + pltpu.roll(k, HEAD_DIM // 2, axis=1) * sin2k)
        scores = [lax.dot_general(qb[h], kr[h].astype(BF16), (((1,), (1,)), ((), ())),
                                  preferred_element_type=F32) for h in heads]
        project_piece()
        scores_b = [(scores[h] * decay_ref[h]).astype(BF16) for h in heads]
        kz = [(kr[h] * zeta_ref[h]).astype(BF16) for h in heads]
        state = [state_ref[h] for h in heads]
        intra = [jnp.dot(scores_b[h], vr[h], preferred_element_type=F32) for h in heads]
        inter = [jnp.dot(qb[h], state[h].astype(BF16), preferred_element_type=F32) for h in heads]
        kv = [lax.dot_general(kz[h], vr[h], (((0,), (0,)), ((), ())), preferred_element_type=F32) for h in heads]
        project_piece()
        for h in heads:
            state_ref[h] = gamma_c[h] * state[h] + kv[h]
            on = _norm_rows(intra[h] + inter[h] * xi_ref[h], gng_ref[h:h + 1, :], gnb_ref[h:h + 1, :])
            g = p_ref[rows, cols(OFF_G, h)].astype(F32)
            o_ref[rows, cols(SGU_WIDTH, h)] = (g * jax.nn.sigmoid(g) * on).astype(o_ref.dtype)
    assert len(issued) == N_PROJ_PIECES

    for copy, (_, stage, out_ref) in zip(ffn_weight_copies(step), ffn_weights):
        copy.wait()
        out_ref[...] = stage[step % 2].astype(BF16)


def _mixer_constants(seq):
    half = HEAD_DIM // 2
    inv = 1.0 / (ROPE_BASE ** (np.arange(half, dtype=np.float64) / half))
    inv2 = np.concatenate([inv, inv])
    sign = np.concatenate([-np.ones(half), np.ones(half)])
    ang_a = (CHUNK * np.arange(seq // CHUNK, dtype=np.float64))[:, None] * inv2[None, :]
    ang_b = np.arange(CHUNK, dtype=np.float64)[:, None] * inv2[None, :]
    rot = [np.cos(ang_a), np.sin(ang_a), np.cos(ang_b), np.sin(ang_b),
           sign * np.cos(ang_b), sign * np.sin(ang_b)]

    log_gamma = np.log(1.0 - np.exp2(-5.0 - np.arange(HEADS, dtype=np.float64)))
    idx = np.arange(CHUNK, dtype=np.float64)
    diff = idx[:, None] - idx[None, :]
    decay = np.where(diff[None] >= 0, np.exp(np.maximum(diff, 0.0)[None] * log_gamma[:, None, None]), 0.0)
    zeta = np.exp((CHUNK - 1.0 - idx)[None, :] * log_gamma[:, None])
    xi = np.exp((idx + 1.0)[None, :] * log_gamma[:, None])
    bcast = lambda v: np.broadcast_to(v[:, :, None], (HEADS, CHUNK, HEAD_DIM))
    gamma_c = tuple(float(np.float32(v)) for v in np.exp(CHUNK * log_gamma))
    tables = [jnp.asarray(np.ascontiguousarray(t), dtype=F32) for t in rot + [decay, bcast(zeta), bcast(xi)]]
    return tables, gamma_c


def _mixer_out_proj(proj, lng, lnb, ws, bs, gng, gnb, w_out, x, w_gate, w_up, w_down, *, chunks_per_step):
    s, d = x.shape
    dff = w_gate.shape[1]
    n_chunks = s // CHUNK
    rows = chunks_per_step * CHUNK
    n_tiles = s // rows
    gate_rows, down_rows = d // n_tiles, dff // n_tiles
    cast_block = lambda i: (jnp.minimum(i, n_tiles - 1), 0)
    (ca, sa, cb, sb, cbs, sbs, decay, zeta, xi), gamma_c = _mixer_constants(s)
    full = lambda shape, **kw: pl.BlockSpec(shape, lambda i: (0,) * len(shape), **kw)
    head_tiles = full((HEADS, CHUNK, HEAD_DIM))
    head_rows = full((HEADS, HEAD_DIM))
    mixed_tile = lambda i: (jnp.minimum(i, n_tiles - 1), 0)
    projected_tile = lambda i: (jnp.maximum(i - 1, 0), 0)
    return pl.pallas_call(
        functools.partial(_mixer_kernel, chunks_per_step=chunks_per_step, gamma_c=gamma_c),
        grid=(n_tiles + 1,),
        in_specs=[
            pl.BlockSpec((rows, IN_WIDTH), mixed_tile),
            full((n_chunks, HEAD_DIM)), full((n_chunks, HEAD_DIM)),
            full((CHUNK, HEAD_DIM)), full((CHUNK, HEAD_DIM)), full((CHUNK, HEAD_DIM)), full((CHUNK, HEAD_DIM)),
            head_rows, head_rows, head_tiles, head_rows, head_rows, head_rows,
            head_tiles, head_tiles, head_tiles,
            pl.BlockSpec(memory_space=pl.ANY),
            pl.BlockSpec((rows, d), projected_tile),
            pl.BlockSpec(memory_space=pl.ANY),
            pl.BlockSpec(memory_space=pl.ANY),
            pl.BlockSpec(memory_space=pl.ANY),
        ],
        out_specs=[pl.BlockSpec((rows, d), projected_tile),
                   pl.BlockSpec((gate_rows, dff), cast_block),
                   pl.BlockSpec((gate_rows, dff), cast_block),
                   pl.BlockSpec((down_rows, d), cast_block)],
        out_shape=[jax.ShapeDtypeStruct((s, d), F32),
                   jax.ShapeDtypeStruct(w_gate.shape, BF16),
                   jax.ShapeDtypeStruct(w_up.shape, BF16),
                   jax.ShapeDtypeStruct(w_down.shape, BF16)],
        scratch_shapes=[pltpu.VMEM((HEADS, HEAD_DIM, HEAD_DIM), F32),
                        pltpu.VMEM((HEADS, CHUNK, CHUNK), BF16),
                        pltpu.VMEM((HEADS, CHUNK, HEAD_DIM), F32),
                        pltpu.VMEM((MIX_WIDTH, d), BF16),
                        pltpu.VMEM((rows, MIX_WIDTH), BF16),
                        pltpu.VMEM((rows, MIX_WIDTH), BF16),
                        pltpu.VMEM((2, W_OUT_STAGE_ROWS, d), F32),
                        pltpu.SemaphoreType.DMA((2,)),
                        pltpu.VMEM((2, gate_rows, dff), F32),
                        pltpu.VMEM((2, gate_rows, dff), F32),
                        pltpu.VMEM((2, down_rows, d), F32),
                        pltpu.SemaphoreType.DMA((3, 2))],
        compiler_params=pltpu.CompilerParams(
            dimension_semantics=("arbitrary",),
            vmem_limit_bytes=V7X_VMEM_LIMIT_BYTES),
        name="mixer_out_proj",
    )(proj, ca, sa, cb, sb, cbs, sbs, lng, lnb, ws, bs, gng, gnb, decay, zeta, xi, w_out, x,
      w_gate, w_up, w_down)


def _ffn_kernel(x_hbm, g2_ref, wg_ref, wu_ref, wd_ref, gf_ref, o_ref, h_ref, x_ref, x_sem, *, final_norm):
    f = pl.program_id(1)
    last = pl.num_programs(1) - 1

    def ffn_step(is_first, is_last):
        if is_first:
            base = x_ref[...]
            h = _rms_rows(base, g2_ref[...]).astype(BF16)
            h_ref[...] = h
        else:
            base = o_ref[...]
            h = h_ref[...]
        gate = jnp.dot(h, wg_ref[...], preferred_element_type=F32)
        up = jnp.dot(h, wu_ref[...], preferred_element_type=F32)
        a = (gate * jax.nn.sigmoid(gate) * up).astype(BF16)
        y = base + jnp.dot(a, wd_ref[...], preferred_element_type=F32)
        o_ref[...] = _rms_rows(y, gf_ref[...]) if (is_last and final_norm) else y

    _prefetched_row_tile(x_hbm, x_ref, x_sem, functools.partial(ffn_step, True, False))
    pl.when((f > 0) & (f < last))(functools.partial(ffn_step, False, False))
    pl.when(f == last)(functools.partial(ffn_step, False, True))


def _ffn(x1, g2, wg, wu, wd, gf, *, tm, tf, final_norm):
    s, d = x1.shape
    dff = wg.shape[1]
    assert dff // tf >= 2, "first and last inner steps are distinct code paths"
    return pl.pallas_call(
        functools.partial(_ffn_kernel, final_norm=final_norm),
        grid=(s // tm, dff // tf),
        in_specs=[
            pl.BlockSpec(memory_space=pl.ANY),
            pl.BlockSpec((1, d), lambda i, f: (0, 0)),
            pl.BlockSpec((d, tf), lambda i, f: (0, f)),
            pl.BlockSpec((d, tf), lambda i, f: (0, f)),
            pl.BlockSpec((tf, d), lambda i, f: (f, 0)),
            pl.BlockSpec((1, d), lambda i, f: (0, 0)),
        ],
        out_specs=pl.BlockSpec((tm, d), lambda i, f: (i, 0)),
        out_shape=jax.ShapeDtypeStruct((s, d), F32),
        scratch_shapes=[pltpu.VMEM((tm, d), BF16), pltpu.VMEM((tm, d), F32), pltpu.SemaphoreType.DMA(())],
        compiler_params=pltpu.CompilerParams(
            dimension_semantics=("arbitrary", "arbitrary"),
            vmem_limit_bytes=V7X_VMEM_LIMIT_BYTES),
        name="ffn",
    )(x1, g2, wg, wu, wd, gf)


def kernel(x, norm1_g, w_in, sgu_ln_g, sgu_ln_b, w_spatial, b_spatial, ret_gn_g, ret_gn_b,
           w_out, norm2_g, w_gate, w_up, w_down, final_norm_g):
    batch, seq, d = x.shape
    depth = w_in.shape[0]
    outs = []
    for b in range(batch):
        xb = x[b]
        for l in range(depth):
            proj = _in_proj(xb, norm1_g[l][None, :], w_in[l],
                            tm=IN_PROJ_ROWS, tn=IN_PROJ_COLS, n_split=IN_PROJ_DOTS_PER_STEP)
            x1, wg, wu, wd = _mixer_out_proj(proj, sgu_ln_g[l], sgu_ln_b[l], w_spatial[l], b_spatial[l],
                                             ret_gn_g[l], ret_gn_b[l], w_out[l], xb,
                                             w_gate[l], w_up[l], w_down[l],
                                             chunks_per_step=MIXER_CHUNKS_PER_STEP)
            xb = _ffn(x1, norm2_g[l][None, :], wg, wu, wd, final_norm_g[None, :],
                      tm=FFN_ROWS, tf=FFN_COLS, final_norm=(l == depth - 1))
        outs.append(xb)
    return outs[0][None] if batch == 1 else jnp.stack(outs)
```

```python
import functools

import numpy as np

import jax
import jax.numpy as jnp
from jax import lax
from jax.experimental import pallas as pl
from jax.experimental.pallas import tpu as pltpu

D_MODEL = 2048
CHUNK = 128
HEADS = 8
HEAD_DIM = 128
SGU_WIDTH = HEADS * HEAD_DIM
RET_WIDTH = HEADS * HEAD_DIM
MIX_WIDTH = SGU_WIDTH + RET_WIDTH
IN_WIDTH = 2 * SGU_WIDTH + 4 * RET_WIDTH
ROPE_BASE = 10000.0
EPS = 1e-6

OFF_U, OFF_VS, OFF_Q, OFF_K, OFF_VR, OFF_G = (i * SGU_WIDTH for i in range(6))

V7X_VMEM_LIMIT_BYTES = 60 * 1024 * 1024

N_PROJ_PIECES = 8
W_OUT_STAGE_ROWS = 256

IN_PROJ_ROWS, IN_PROJ_COLS, IN_PROJ_DOTS_PER_STEP = 2048, 1024, 2
MIXER_CHUNKS_PER_STEP = 2
FFN_ROWS, FFN_COLS = 1024, 512

F32 = jnp.float32
BF16 = jnp.bfloat16


def _rms_rows(x, g):
    ms = jnp.mean(x * x, axis=-1, keepdims=True)
    return x * lax.rsqrt(ms + EPS) * g


def _row_tile_copy(x_hbm, x_ref, sem, tile):
    tm = x_ref.shape[0]
    start = pl.multiple_of(tile * tm, tm)
    return pltpu.make_async_copy(x_hbm.at[pl.ds(start, tm), :], x_ref, sem)


def _prefetched_row_tile(x_hbm, x_ref, sem, consume):
    i, j = pl.program_id(0), pl.program_id(1)

    @pl.when((i == 0) & (j == 0))
    def _():
        _row_tile_copy(x_hbm, x_ref, sem, 0).start()

    @pl.when(j == 0)
    def _():
        _row_tile_copy(x_hbm, x_ref, sem, i).wait()
        consume()

    @pl.when((j == 1) & (i + 1 < pl.num_programs(0)))
    def _():
        _row_tile_copy(x_hbm, x_ref, sem, i + 1).start()


def _in_proj_kernel(x_hbm, g_ref, w_ref, o_ref, h_ref, x_ref, x_sem, *, n_split):
    def normalise():
        h_ref[...] = _rms_rows(x_ref[...], g_ref[...]).astype(BF16)

    _prefetched_row_tile(x_hbm, x_ref, x_sem, normalise)
    cols = o_ref.shape[1] // n_split
    for s in range(n_split):
        n = slice(s * cols, (s + 1) * cols)
        o_ref[:, n] = jnp.dot(h_ref[...], w_ref[:, n].astype(BF16),
                              preferred_element_type=F32).astype(o_ref.dtype)


def _in_proj(x, g, w, *, tm, tn, n_split):
    s, d = x.shape
    n = w.shape[1]
    assert n // tn >= 2, "the row-tile prefetch starts at inner step 1"
    return pl.pallas_call(
        functools.partial(_in_proj_kernel, n_split=n_split),
        grid=(s // tm, n // tn),
        in_specs=[
            pl.BlockSpec(memory_space=pl.ANY),
            pl.BlockSpec((1, d), lambda i, j: (0, 0)),
            pl.BlockSpec((d, tn), lambda i, j: (0, j)),
        ],
        out_specs=pl.BlockSpec((tm, tn), lambda i, j: (i, j)),
        out_shape=jax.ShapeDtypeStruct((s, n), BF16),
        scratch_shapes=[pltpu.VMEM((tm, d), BF16), pltpu.VMEM((tm, d), F32), pltpu.SemaphoreType.DMA(())],
        compiler_params=pltpu.CompilerParams(
            dimension_semantics=("arbitrary", "arbitrary"),
            vmem_limit_bytes=V7X_VMEM_LIMIT_BYTES),
        name="in_proj",
    )(x, g, w)


def _norm_rows(x, g, b):
    mu = jnp.mean(x, axis=-1, keepdims=True)
    d = x - mu
    var = jnp.mean(d * d, axis=-1, keepdims=True)
    return d * lax.rsqrt(var + EPS) * g + b


def _mixer_kernel(p_ref, ca_ref, sa_ref, cb_ref, sb_ref, cbs_ref, sbs_ref,
                  lng_ref, lnb_ref, ws_ref, bs_ref, gng_ref, gnb_ref,
                  decay_ref, zeta_ref, xi_ref, wout_hbm, x_ref, wg_hbm, wu_hbm, wd_hbm,
                  x1_ref, wgb_ref, wub_ref, wdb_ref,
                  state_ref, wc_ref, bias_ref, woutb_ref, o_ref, prev_ref, stage_ref, stage_sem,
                  wg_stage, wu_stage, wd_stage, ffn_sem, *, chunks_per_step, gamma_c):
    step = pl.program_id(0)
    last_block = pl.num_programs(0) - 2
    ffn_weights = ((wg_hbm, wg_stage, wgb_ref), (wu_hbm, wu_stage, wub_ref), (wd_hbm, wd_stage, wdb_ref))

    def ffn_weight_copies(at_step):
        block = jnp.minimum(at_step, last_block)
        slot = at_step % 2
        copies = []
        for k, (w_hbm, stage, _) in enumerate(ffn_weights):
            rows = stage.shape[1]
            start = pl.multiple_of(block * rows, 8)
            copies.append(pltpu.make_async_copy(w_hbm.at[pl.ds(start, rows), :], stage.at[slot], ffn_sem.at[k, slot]))
        return copies

    row_id = lax.broadcasted_iota(jnp.int32, (CHUNK, CHUNK), 0)
    col_id = lax.broadcasted_iota(jnp.int32, (CHUNK, CHUNK), 1)

    @pl.when(step == 0)
    def _():
        for copy in ffn_weight_copies(0):
            copy.start()
        state_ref[...] = jnp.zeros_like(state_ref)
        o_ref[...] = jnp.zeros_like(o_ref)
        stage_rows = stage_ref.shape[1]
        n_stage = woutb_ref.shape[0] // stage_rows

        def stage_copy(r):
            return pltpu.make_async_copy(wout_hbm.at[pl.ds(r * stage_rows, stage_rows), :],
                                         stage_ref.at[r % 2], stage_sem.at[r % 2])

        stage_copy(0).start()
        for r in range(n_stage):
            if r + 1 < n_stage:
                stage_copy(r + 1).start()
            stage_copy(r).wait()
            woutb_ref[r * stage_rows:(r + 1) * stage_rows, :] = stage_ref[r % 2].astype(BF16)
        for h in range(HEADS):
            wc_ref[h] = jnp.where(row_id >= col_id, ws_ref[h], 0.0).astype(BF16)
            b_col = jnp.sum(jnp.where(row_id == col_id, bs_ref[h:h + 1, :], 0.0), axis=1, keepdims=True)
            bias_ref[h] = jnp.broadcast_to(b_col, (CHUNK, HEAD_DIM))

    @pl.when(step + 1 < pl.num_programs(0))
    def _():
        for copy in ffn_weight_copies(step + 1):
            copy.start()

    k_scale = HEAD_DIM ** -0.5
    chunk0 = jnp.minimum(step, last_block) * chunks_per_step
    heads = range(HEADS)

    prev_ref[...] = o_ref[...]
    piece_cols = x1_ref.shape[1] // N_PROJ_PIECES
    issued = []

    def project_piece():
        n = slice(len(issued) * piece_cols, (len(issued) + 1) * piece_cols)
        issued.append(n)
        x1_ref[:, n] = x_ref[:, n] + jnp.dot(prev_ref[...], woutb_ref[:, n], preferred_element_type=F32)

    def cols(off, h):
        return slice(off + h * HEAD_DIM, off + (h + 1) * HEAD_DIM)

    def chunk_rows(c):
        return slice(c * CHUNK, (c + 1) * CHUNK)

    def spatial_gate(c):
        rows = chunk_rows(c)
        vn = [_norm_rows(p_ref[rows, cols(OFF_VS, h)].astype(F32), lng_ref[h:h + 1, :], lnb_ref[h:h + 1, :])
              .astype(BF16) for h in heads]
        mixed = [jnp.dot(wc_ref[h], vn[h], preferred_element_type=F32) for h in heads]
        for h in heads:
            u = p_ref[rows, cols(OFF_U, h)].astype(F32)
            o_ref[rows, cols(0, h)] = (u * (mixed[h] + bias_ref[h])).astype(o_ref.dtype)

    def retention_scores(c):
        rows = chunk_rows(c)
        ca = ca_ref[pl.ds(chunk0 + c, 1), :]
        sa = sa_ref[pl.ds(chunk0 + c, 1), :]
        cos2 = ca * cb_ref[...] - sa * sb_ref[...]
        sin2 = sa * cbs_ref[...] + ca * sbs_ref[...]
        cos2k = cos2 * k_scale
        sin2k = sin2 * k_scale
        qb, kr = [], []
        for h in heads:
            q = p_ref[rows, cols(OFF_Q, h)].astype(F32)
            k = p_ref[rows, cols(OFF_K, h)].astype(F32)
            qb.append((q * cos2 + pltpu.roll(q, HEAD_DIM // 2, axis=1) * sin2).astype(BF16))
            kr.append(k * cos2k + pltpu.roll(k, HEAD_DIM // 2, axis=1) * sin2k)
        scores = [lax.dot_general(qb[h], kr[h].astype(BF16), (((1,), (1,)), ((), ())),
                                  preferred_element_type=F32) for h in heads]
        return qb, kr, scores

    def retention_values(c, qb, kr, scores):
        rows = chunk_rows(c)
        vr = [p_ref[rows, cols(OFF_VR, h)] for h in heads]
        scores_b = [(scores[h] * decay_ref[h]).astype(BF16) for h in heads]
        kz = [(kr[h] * zeta_ref[h]).astype(BF16) for h in heads]
        state = [state_ref[h] for h in heads]
        intra = [jnp.dot(scores_b[h], vr[h], preferred_element_type=F32) for h in heads]
        inter = [jnp.dot(qb[h], state[h].astype(BF16), preferred_element_type=F32) for h in heads]
        kv = [lax.dot_general(kz[h], vr[h], (((0,), (0,)), ((), ())), preferred_element_type=F32) for h in heads]
        for h in heads:
            state_ref[h] = gamma_c[h] * state[h] + kv[h]
        return [intra[h] + inter[h] * xi_ref[h] for h in heads]

    def retention_gate(c, outs):
        rows = chunk_rows(c)
        for h in heads:
            on = _norm_rows(outs[h], gng_ref[h:h + 1, :], gnb_ref[h:h + 1, :])
            g = p_ref[rows, cols(OFF_G, h)].astype(F32)
            o_ref[rows, cols(SGU_WIDTH, h)] = (g * jax.nn.sigmoid(g) * on).astype(o_ref.dtype)

    chunks = range(chunks_per_step)
    project_piece()
    project_piece()
    for c in chunks:
        spatial_gate(c)
    project_piece()
    scored = [retention_scores(c) for c in chunks]
    project_piece()
    project_piece()
    outs = []
    for c in chunks:
        outs.append(retention_values(c, *scored[c]))
        project_piece()
    for c in chunks:
        retention_gate(c, outs[c])
    project_piece()
    assert len(issued) == N_PROJ_PIECES

    for copy, (_, stage, out_ref) in zip(ffn_weight_copies(step), ffn_weights):
        copy.wait()
        out_ref[...] = stage[step % 2].astype(BF16)


def _mixer_constants(seq):
    half = HEAD_DIM // 2
    inv = 1.0 / (ROPE_BASE ** (np.arange(half, dtype=np.float64) / half))
    inv2 = np.concatenate([inv, inv])
    sign = np.concatenate([-np.ones(half), np.ones(half)])
    ang_a = (CHUNK * np.arange(seq // CHUNK, dtype=np.float64))[:, None] * inv2[None, :]
    ang_b = np.arange(CHUNK, dtype=np.float64)[:, None] * inv2[None, :]
    rot = [np.cos(ang_a), np.sin(ang_a), np.cos(ang_b), np.sin(ang_b),
           sign * np.cos(ang_b), sign * np.sin(ang_b)]

    log_gamma = np.log(1.0 - np.exp2(-5.0 - np.arange(HEADS, dtype=np.float64)))
    idx = np.arange(CHUNK, dtype=np.float64)
    diff = idx[:, None] - idx[None, :]
    decay = np.where(diff[None] >= 0, np.exp(np.maximum(diff, 0.0)[None] * log_gamma[:, None, None]), 0.0)
    zeta = np.exp((CHUNK - 1.0 - idx)[None, :] * log_gamma[:, None])
    xi = np.exp((idx + 1.0)[None, :] * log_gamma[:, None])
    bcast = lambda v: np.broadcast_to(v[:, :, None], (HEADS, CHUNK, HEAD_DIM))
    gamma_c = tuple(float(np.float32(v)) for v in np.exp(CHUNK * log_gamma))
    tables = [jnp.asarray(np.ascontiguousarray(t), dtype=F32) for t in rot + [decay, bcast(zeta), bcast(xi)]]
    return tables, gamma_c


def _mixer_out_proj(proj, lng, lnb, ws, bs, gng, gnb, w_out, x, w_gate, w_up, w_down, *, chunks_per_step):
    s, d = x.shape
    dff = w_gate.shape[1]
    n_chunks = s // CHUNK
    rows = chunks_per_step * CHUNK
    n_tiles = s // rows
    gate_rows, down_rows = d // n_tiles, dff // n_tiles
    cast_block = lambda i: (jnp.minimum(i, n_tiles - 1), 0)
    (ca, sa, cb, sb, cbs, sbs, decay, zeta, xi), gamma_c = _mixer_constants(s)
    full = lambda shape, **kw: pl.BlockSpec(shape, lambda i: (0,) * len(shape), **kw)
    head_tiles = full((HEADS, CHUNK, HEAD_DIM))
    head_rows = full((HEADS, HEAD_DIM))
    mixed_tile = lambda i: (jnp.minimum(i, n_tiles - 1), 0)
    projected_tile = lambda i: (jnp.maximum(i - 1, 0), 0)
    return pl.pallas_call(
        functools.partial(_mixer_kernel, chunks_per_step=chunks_per_step, gamma_c=gamma_c),
        grid=(n_tiles + 1,),
        in_specs=[
            pl.BlockSpec((rows, IN_WIDTH), mixed_tile),
            full((n_chunks, HEAD_DIM)), full((n_chunks, HEAD_DIM)),
            full((CHUNK, HEAD_DIM)), full((CHUNK, HEAD_DIM)), full((CHUNK, HEAD_DIM)), full((CHUNK, HEAD_DIM)),
            head_rows, head_rows, head_tiles, head_rows, head_rows, head_rows,
            head_tiles, head_tiles, head_tiles,
            pl.BlockSpec(memory_space=pl.ANY),
            pl.BlockSpec((rows, d), projected_tile),
            pl.BlockSpec(memory_space=pl.ANY),
            pl.BlockSpec(memory_space=pl.ANY),
            pl.BlockSpec(memory_space=pl.ANY),
        ],
        out_specs=[pl.BlockSpec((rows, d), projected_tile),
                   pl.BlockSpec((gate_rows, dff), cast_block),
                   pl.BlockSpec((gate_rows, dff), cast_block),
                   pl.BlockSpec((down_rows, d), cast_block)],
        out_shape=[jax.ShapeDtypeStruct((s, d), F32),
                   jax.ShapeDtypeStruct(w_gate.shape, BF16),
                   jax.ShapeDtypeStruct(w_up.shape, BF16),
                   jax.ShapeDtypeStruct(w_down.shape, BF16)],
        scratch_shapes=[pltpu.VMEM((HEADS, HEAD_DIM, HEAD_DIM), F32),
                        pltpu.VMEM((HEADS, CHUNK, CHUNK), BF16),
                        pltpu.VMEM((HEADS, CHUNK, HEAD_DIM), F32),
                        pltpu.VMEM((MIX_WIDTH, d), BF16),
                        pltpu.VMEM((rows, MIX_WIDTH), BF16),
                        pltpu.VMEM((rows, MIX_WIDTH), BF16),
                        pltpu.VMEM((2, W_OUT_STAGE_ROWS, d), F32),
                        pltpu.SemaphoreType.DMA((2,)),
                        pltpu.VMEM((2, gate_rows, dff), F32),
                        pltpu.VMEM((2, gate_rows, dff), F32),
                        pltpu.VMEM((2, down_rows, d), F32),
                        pltpu.SemaphoreType.DMA((3, 2))],
        compiler_params=pltpu.CompilerParams(
            dimension_semantics=("arbitrary",),
            vmem_limit_bytes=V7X_VMEM_LIMIT_BYTES),
        name="mixer_out_proj",
    )(proj, ca, sa, cb, sb, cbs, sbs, lng, lnb, ws, bs, gng, gnb, decay, zeta, xi, w_out, x,
      w_gate, w_up, w_down)


def _ffn_kernel(x_hbm, g2_ref, wg_ref, wu_ref, wd_ref, gf_ref, o_ref, h_ref, x_ref, x_sem, *, final_norm):
    f = pl.program_id(1)
    last = pl.num_programs(1) - 1

    def ffn_step(is_first, is_last):
        if is_first:
            base = x_ref[...]
            h = _rms_rows(base, g2_ref[...]).astype(BF16)
            h_ref[...] = h
        else:
            base = o_ref[...]
            h = h_ref[...]
        gate = jnp.dot(h, wg_ref[...], preferred_element_type=F32)
        up = jnp.dot(h, wu_ref[...], preferred_element_type=F32)
        a = (gate * jax.nn.sigmoid(gate) * up).astype(BF16)
        y = base + jnp.dot(a, wd_ref[...], preferred_element_type=F32)
        o_ref[...] = _rms_rows(y, gf_ref[...]) if (is_last and final_norm) else y

    _prefetched_row_tile(x_hbm, x_ref, x_sem, functools.partial(ffn_step, True, False))
    pl.when((f > 0) & (f < last))(functools.partial(ffn_step, False, False))
    pl.when(f == last)(functools.partial(ffn_step, False, True))


def _ffn(x1, g2, wg, wu, wd, gf, *, tm, tf, final_norm):
    s, d = x1.shape
    dff = wg.shape[1]
    assert dff // tf >= 2, "first and last inner steps are distinct code paths"
    return pl.pallas_call(
        functools.partial(_ffn_kernel, final_norm=final_norm),
        grid=(s // tm, dff // tf),
        in_specs=[
            pl.BlockSpec(memory_space=pl.ANY),
            pl.BlockSpec((1, d), lambda i, f: (0, 0)),
            pl.BlockSpec((d, tf), lambda i, f: (0, f)),
            pl.BlockSpec((d, tf), lambda i, f: (0, f)),
            pl.BlockSpec((tf, d), lambda i, f: (f, 0)),
            pl.BlockSpec((1, d), lambda i, f: (0, 0)),
        ],
        out_specs=pl.BlockSpec((tm, d), lambda i, f: (i, 0)),
        out_shape=jax.ShapeDtypeStruct((s, d), F32),
        scratch_shapes=[pltpu.VMEM((tm, d), BF16), pltpu.VMEM((tm, d), F32), pltpu.SemaphoreType.DMA(())],
        compiler_params=pltpu.CompilerParams(
            dimension_semantics=("arbitrary", "arbitrary"),
            vmem_limit_bytes=V7X_VMEM_LIMIT_BYTES),
        name="ffn",
    )(x1, g2, wg, wu, wd, gf)


def kernel(x, norm1_g, w_in, sgu_ln_g, sgu_ln_b, w_spatial, b_spatial, ret_gn_g, ret_gn_b,
           w_out, norm2_g, w_gate, w_up, w_down, final_norm_g):
    batch, seq, d = x.shape
    depth = w_in.shape[0]
    outs = []
    for b in range(batch):
        xb = x[b]
        for l in range(depth):
            proj = _in_proj(xb, norm1_g[l][None, :], w_in[l],
                            tm=IN_PROJ_ROWS, tn=IN_PROJ_COLS, n_split=IN_PROJ_DOTS_PER_STEP)
            x1, wg, wu, wd = _mixer_out_proj(proj, sgu_ln_g[l], sgu_ln_b[l], w_spatial[l], b_spatial[l],
                                             ret_gn_g[l], ret_gn_b[l], w_out[l], xb,
                                             w_gate[l], w_up[l], w_down[l],
                                             chunks_per_step=MIXER_CHUNKS_PER_STEP)
            xb = _ffn(x1, norm2_g[l][None, :], wg, wu, wd, final_norm_g[None, :],
                      tm=FFN_ROWS, tf=FFN_COLS, final_norm=(l == depth - 1))
        outs.append(xb)
    return outs[0][None] if batch == 1 else jnp.stack(outs)
```

```python
import functools

import numpy as np

import jax
import jax.numpy as jnp
from jax import lax
from jax.experimental import pallas as pl
from jax.experimental.pallas import tpu as pltpu

CHUNK = 128
HEADS = 8
HEAD_DIM = 128
SGU_WIDTH = HEADS * HEAD_DIM
RET_WIDTH = HEADS * HEAD_DIM
MIX_WIDTH = SGU_WIDTH + RET_WIDTH
IN_WIDTH = 2 * SGU_WIDTH + 4 * RET_WIDTH
ROPE_BASE = 10000.0
EPS = 1e-6

OFF_U, OFF_VS, OFF_Q, OFF_K, OFF_VR, OFF_G = (i * SGU_WIDTH for i in range(6))

V7X_VMEM_LIMIT_BYTES = 60 * 1024 * 1024

N_PROJ_PIECES = 8
W_OUT_STAGE_ROWS = 256

IN_PROJ_ROWS, IN_PROJ_COLS, IN_PROJ_DOTS_PER_STEP = 2048, 1024, 2
MIXER_CHUNKS_PER_STEP = 2
FFN_ROWS, FFN_COLS = 1024, 512

F32 = jnp.float32
BF16 = jnp.bfloat16


def _rms_rows(x, g):
    ms = jnp.mean(x * x, axis=-1, keepdims=True)
    return x * lax.rsqrt(ms + EPS) * g


def _norm_rows(x, g, b):
    mu = jnp.mean(x, axis=-1, keepdims=True)
    d = x - mu
    var = jnp.mean(d * d, axis=-1, keepdims=True)
    return d * lax.rsqrt(var + EPS) * g + b


def _row_tile_copy(x_hbm, x_ref, sem, tile):
    tm = x_ref.shape[0]
    start = pl.multiple_of(tile * tm, tm)
    return pltpu.make_async_copy(x_hbm.at[pl.ds(start, tm), :], x_ref, sem)


def _prefetched_row_tile(x_hbm, x_ref, sem, consume):
    i, j = pl.program_id(0), pl.program_id(1)

    @pl.when((i == 0) & (j == 0))
    def _():
        _row_tile_copy(x_hbm, x_ref, sem, 0).start()

    @pl.when(j == 0)
    def _():
        _row_tile_copy(x_hbm, x_ref, sem, i).wait()
        consume()

    @pl.when((j == 1) & (i + 1 < pl.num_programs(0)))
    def _():
        _row_tile_copy(x_hbm, x_ref, sem, i + 1).start()


def _in_proj_kernel(x_hbm, g_ref, w_ref, o_ref, h_ref, x_ref, x_sem, *, n_split):
    def normalise():
        h_ref[...] = _rms_rows(x_ref[...], g_ref[...]).astype(BF16)

    _prefetched_row_tile(x_hbm, x_ref, x_sem, normalise)
    cols = o_ref.shape[1] // n_split
    for s in range(n_split):
        n = slice(s * cols, (s + 1) * cols)
        o_ref[:, n] = jnp.dot(h_ref[...], w_ref[:, n].astype(BF16),
                              preferred_element_type=F32).astype(o_ref.dtype)


def _in_proj(x, g, w, *, tm, tn, n_split):
    s, d = x.shape
    n = w.shape[1]
    assert n // tn >= 2, "the row-tile prefetch starts at inner step 1"
    return pl.pallas_call(
        functools.partial(_in_proj_kernel, n_split=n_split),
        grid=(s // tm, n // tn),
        in_specs=[
            pl.BlockSpec(memory_space=pl.ANY),
            pl.BlockSpec((1, d), lambda i, j: (0, 0)),
            pl.BlockSpec((d, tn), lambda i, j: (0, j)),
        ],
        out_specs=pl.BlockSpec((tm, tn), lambda i, j: (i, j)),
        out_shape=jax.ShapeDtypeStruct((s, n), BF16),
        scratch_shapes=[pltpu.VMEM((tm, d), BF16), pltpu.VMEM((tm, d), F32), pltpu.SemaphoreType.DMA(())],
        compiler_params=pltpu.CompilerParams(
            dimension_semantics=("arbitrary", "arbitrary"),
            vmem_limit_bytes=V7X_VMEM_LIMIT_BYTES),
        name="in_proj",
    )(x, g, w)


def _mixer_kernel(p_ref, ca_ref, sa_ref, cb_ref, sb_ref, cbs_ref, sbs_ref,
                  lng_ref, lnb_ref, ws_ref, bs_ref, gng_ref, gnb_ref,
                  decay_ref, zeta_ref, xi_ref, wout_hbm, x_ref, wg_hbm, wu_hbm, wd_hbm,
                  x1_ref, wgb_ref, wub_ref, wdb_ref,
                  state_ref, wc_ref, bias_ref, woutb_ref, o_ref, prev_ref, stage_ref, stage_sem,
                  wg_stage, wu_stage, wd_stage, ffn_sem, *, chunks_per_step, gamma_c):
    step = pl.program_id(0)
    last_block = pl.num_programs(0) - 2
    ffn_weights = ((wg_hbm, wg_stage, wgb_ref), (wu_hbm, wu_stage, wub_ref), (wd_hbm, wd_stage, wdb_ref))

    def ffn_weight_copies(at_step):
        block = jnp.minimum(at_step, last_block)
        slot = at_step % 2
        copies = []
        for k, (w_hbm, stage, _) in enumerate(ffn_weights):
            rows = stage.shape[1]
            start = pl.multiple_of(block * rows, 8)
            copies.append(pltpu.make_async_copy(w_hbm.at[pl.ds(start, rows), :], stage.at[slot], ffn_sem.at[k, slot]))
        return copies

    row_id = lax.broadcasted_iota(jnp.int32, (CHUNK, CHUNK), 0)
    col_id = lax.broadcasted_iota(jnp.int32, (CHUNK, CHUNK), 1)

    @pl.when(step == 0)
    def _():
        for copy in ffn_weight_copies(0):
            copy.start()
        state_ref[...] = jnp.zeros_like(state_ref)
        o_ref[...] = jnp.zeros_like(o_ref)
        stage_rows = stage_ref.shape[1]
        n_stage = woutb_ref.shape[0] // stage_rows

        def stage_copy(r):
            return pltpu.make_async_copy(wout_hbm.at[pl.ds(r * stage_rows, stage_rows), :],
                                         stage_ref.at[r % 2], stage_sem.at[r % 2])

        stage_copy(0).start()
        for r in range(n_stage):
            if r + 1 < n_stage:
                stage_copy(r + 1).start()
            stage_copy(r).wait()
            woutb_ref[r * stage_rows:(r + 1) * stage_rows, :] = stage_ref[r % 2].astype(BF16)
        for h in range(HEADS):
            wc_ref[h] = jnp.where(row_id >= col_id, ws_ref[h], 0.0).astype(BF16)
            b_col = jnp.sum(jnp.where(row_id == col_id, bs_ref[h:h + 1, :], 0.0), axis=1, keepdims=True)
            bias_ref[h] = jnp.broadcast_to(b_col, (CHUNK, HEAD_DIM))

    @pl.when(step + 1 < pl.num_programs(0))
    def _():
        for copy in ffn_weight_copies(step + 1):
            copy.start()

    k_scale = HEAD_DIM ** -0.5
    chunk0 = jnp.minimum(step, last_block) * chunks_per_step
    heads = range(HEADS)

    prev_ref[...] = o_ref[...]
    piece_cols = x1_ref.shape[1] // N_PROJ_PIECES
    issued = []

    def project_piece():
        n = slice(len(issued) * piece_cols, (len(issued) + 1) * piece_cols)
        issued.append(n)
        x1_ref[:, n] = x_ref[:, n] + jnp.dot(prev_ref[...], woutb_ref[:, n], preferred_element_type=F32)

    def cols(off, h):
        return slice(off + h * HEAD_DIM, off + (h + 1) * HEAD_DIM)

    def chunk_rows(c):
        return slice(c * CHUNK, (c + 1) * CHUNK)

    def spatial_gate(c):
        rows = chunk_rows(c)
        vn = [_norm_rows(p_ref[rows, cols(OFF_VS, h)].astype(F32), lng_ref[h:h + 1, :], lnb_ref[h:h + 1, :])
              .astype(BF16) for h in heads]
        mixed = [jnp.dot(wc_ref[h], vn[h], preferred_element_type=F32) for h in heads]
        for h in heads:
            u = p_ref[rows, cols(OFF_U, h)].astype(F32)
            o_ref[rows, cols(0, h)] = (u * (mixed[h] + bias_ref[h])).astype(o_ref.dtype)

    def retention_scores(c):
        rows = chunk_rows(c)
        ca = ca_ref[pl.ds(chunk0 + c, 1), :]
        sa = sa_ref[pl.ds(chunk0 + c, 1), :]
        cos2 = ca * cb_ref[...] - sa * sb_ref[...]
        sin2 = sa * cbs_ref[...] + ca * sbs_ref[...]
        cos2k = cos2 * k_scale
        sin2k = sin2 * k_scale
        qb, kr = [], []
        for h in heads:
            q = p_ref[rows, cols(OFF_Q, h)].astype(F32)
            k = p_ref[rows, cols(OFF_K, h)].astype(F32)
            qb.append((q * cos2 + pltpu.roll(q, HEAD_DIM // 2, axis=1) * sin2).astype(BF16))
            kr.append(k * cos2k + pltpu.roll(k, HEAD_DIM // 2, axis=1) * sin2k)
        scores = [lax.dot_general(qb[h], kr[h].astype(BF16), (((1,), (1,)), ((), ())),
                                  preferred_element_type=F32) for h in heads]
        return qb, kr, scores

    def retention_values(c, qb, kr, scores):
        rows = chunk_rows(c)
        vr = [p_ref[rows, cols(OFF_VR, h)] for h in heads]
        scores_b = [(scores[h] * decay_ref[h]).astype(BF16) for h in heads]
        kz = [(kr[h] * zeta_ref[h]).astype(BF16) for h in heads]
        state = [state_ref[h] for h in heads]
        intra = [jnp.dot(scores_b[h], vr[h], preferred_element_type=F32) for h in heads]
        inter = [jnp.dot(qb[h], state[h].astype(BF16), preferred_element_type=F32) for h in heads]
        kv = [lax.dot_general(kz[h], vr[h], (((0,), (0,)), ((), ())), preferred_element_type=F32) for h in heads]
        for h in heads:
            state_ref[h] = gamma_c[h] * state[h] + kv[h]
        return [intra[h] + inter[h] * xi_ref[h] for h in heads]

    def retention_gate(c, outs):
        rows = chunk_rows(c)
        for h in heads:
            on = _norm_rows(outs[h], gng_ref[h:h + 1, :], gnb_ref[h:h + 1, :])
            g = p_ref[rows, cols(OFF_G, h)].astype(F32)
            o_ref[rows, cols(SGU_WIDTH, h)] = (g * jax.nn.sigmoid(g) * on).astype(o_ref.dtype)

    for c in range(chunks_per_step):
        project_piece()
        spatial_gate(c)
        project_piece()
        scored = retention_scores(c)
        project_piece()
        outs = retention_values(c, *scored)
        project_piece()
        retention_gate(c, outs)
    assert len(issued) == N_PROJ_PIECES

    for copy, (_, stage, out_ref) in zip(ffn_weight_copies(step), ffn_weights):
        copy.wait()
        out_ref[...] = stage[step % 2].astype(BF16)


def _mixer_constants(seq):
    half = HEAD_DIM // 2
    inv = 1.0 / (ROPE_BASE ** (np.arange(half, dtype=np.float64) / half))
    inv2 = np.concatenate([inv, inv])
    sign = np.concatenate([-np.ones(half), np.ones(half)])
    ang_a = (CHUNK * np.arange(seq // CHUNK, dtype=np.float64))[:, None] * inv2[None, :]
    ang_b = np.arange(CHUNK, dtype=np.float64)[:, None] * inv2[None, :]
    rot = [np.cos(ang_a), np.sin(ang_a), np.cos(ang_b), np.sin(ang_b),
           sign * np.cos(ang_b), sign * np.sin(ang_b)]

    log_gamma = np.log(1.0 - np.exp2(-5.0 - np.arange(HEADS, dtype=np.float64)))
    idx = np.arange(CHUNK, dtype=np.float64)
    diff = idx[:, None] - idx[None, :]
    decay = np.where(diff[None] >= 0, np.exp(np.maximum(diff, 0.0)[None] * log_gamma[:, None, None]), 0.0)
    zeta = np.exp((CHUNK - 1.0 - idx)[None, :] * log_gamma[:, None])
    xi = np.exp((idx + 1.0)[None, :] * log_gamma[:, None])
    bcast = lambda v: np.broadcast_to(v[:, :, None], (HEADS, CHUNK, HEAD_DIM))
    gamma_c = tuple(float(np.float32(v)) for v in np.exp(CHUNK * log_gamma))
    tables = [jnp.asarray(np.ascontiguousarray(t), dtype=F32) for t in rot + [decay, bcast(zeta), bcast(xi)]]
    return tables, gamma_c


def _mixer_out_proj(proj, lng, lnb, ws, bs, gng, gnb, w_out, x, w_gate, w_up, w_down, *, chunks_per_step):
    s, d = x.shape
    dff = w_gate.shape[1]
    n_chunks = s // CHUNK
    rows = chunks_per_step * CHUNK
    n_tiles = s // rows
    gate_rows, down_rows = d // n_tiles, dff // n_tiles
    cast_block = lambda i: (jnp.minimum(i, n_tiles - 1), 0)
    (ca, sa, cb, sb, cbs, sbs, decay, zeta, xi), gamma_c = _mixer_constants(s)
    full = lambda shape, **kw: pl.BlockSpec(shape, lambda i: (0,) * len(shape), **kw)
    head_tiles = full((HEADS, CHUNK, HEAD_DIM))
    head_rows = full((HEADS, HEAD_DIM))
    mixed_tile = lambda i: (jnp.minimum(i, n_tiles - 1), 0)
    projected_tile = lambda i: (jnp.maximum(i - 1, 0), 0)
    return pl.pallas_call(
        functools.partial(_mixer_kernel, chunks_per_step=chunks_per_step, gamma_c=gamma_c),
        grid=(n_tiles + 1,),
        in_specs=[
            pl.BlockSpec((rows, IN_WIDTH), mixed_tile),
            full((n_chunks, HEAD_DIM)), full((n_chunks, HEAD_DIM)),
            full((CHUNK, HEAD_DIM)), full((CHUNK, HEAD_DIM)), full((CHUNK, HEAD_DIM)), full((CHUNK, HEAD_DIM)),
            head_rows, head_rows, head_tiles, head_rows, head_rows, head_rows,
            head_tiles, head_tiles, head_tiles,
            pl.BlockSpec(memory_space=pl.ANY),
            pl.BlockSpec((rows, d), projected_tile),
            pl.BlockSpec(memory_space=pl.ANY),
            pl.BlockSpec(memory_space=pl.ANY),
            pl.BlockSpec(memory_space=pl.ANY),
        ],
        out_specs=[pl.BlockSpec((rows, d), projected_tile),
                   pl.BlockSpec((gate_rows, dff), cast_block),
                   pl.BlockSpec((gate_rows, dff), cast_block),
                   pl.BlockSpec((down_rows, d), cast_block)],
        out_shape=[jax.ShapeDtypeStruct((s, d), F32),
                   jax.ShapeDtypeStruct(w_gate.shape, BF16),
                   jax.ShapeDtypeStruct(w_up.shape, BF16),
                   jax.ShapeDtypeStruct(w_down.shape, BF16)],
        scratch_shapes=[pltpu.VMEM((HEADS, HEAD_DIM, HEAD_DIM), F32),
                        pltpu.VMEM((HEADS, CHUNK, CHUNK), BF16),
                        pltpu.VMEM((HEADS, CHUNK, HEAD_DIM), F32),
                        pltpu.VMEM((MIX_WIDTH, d), BF16),
                        pltpu.VMEM((rows, MIX_WIDTH), BF16),
                        pltpu.VMEM((rows, MIX_WIDTH), BF16),
                        pltpu.VMEM((2, W_OUT_STAGE_ROWS, d), F32),
                        pltpu.SemaphoreType.DMA((2,)),
                        pltpu.VMEM((2, gate_rows, dff), F32),
                        pltpu.VMEM((2, gate_rows, dff), F32),
                        pltpu.VMEM((2, down_rows, d), F32),
                        pltpu.SemaphoreType.DMA((3, 2))],
        compiler_params=pltpu.CompilerParams(
            dimension_semantics=("arbitrary",),
            vmem_limit_bytes=V7X_VMEM_LIMIT_BYTES),
        name="mixer_out_proj",
    )(proj, ca, sa, cb, sb, cbs, sbs, lng, lnb, ws, bs, gng, gnb, decay, zeta, xi, w_out, x,
      w_gate, w_up, w_down)


def _ffn_kernel(x_hbm, g2_ref, wg_ref, wu_ref, wd_ref, gf_ref, o_ref, h_ref, x_ref, x_sem, *, final_norm):
    f = pl.program_id(1)
    last = pl.num_programs(1) - 1

    def ffn_step(is_first, is_last):
        if is_first:
            base = x_ref[...]
            h = _rms_rows(base, g2_ref[...]).astype(BF16)
            h_ref[...] = h
        else:
            base = o_ref[...]
            h = h_ref[...]
        gate = jnp.dot(h, wg_ref[...], preferred_element_type=F32)
        up = jnp.dot(h, wu_ref[...], preferred_element_type=F32)
        a = (gate * jax.nn.sigmoid(gate) * up).astype(BF16)
        y = base + jnp.dot(a, wd_ref[...], preferred_element_type=F32)
        o_ref[...] = _rms_rows(y, gf_ref[...]) if (is_last and final_norm) else y

    _prefetched_row_tile(x_hbm, x_ref, x_sem, functools.partial(ffn_step, True, False))
    pl.when((f > 0) & (f < last))(functools.partial(ffn_step, False, False))
    pl.when(f == last)(functools.partial(ffn_step, False, True))


def _ffn(x1, g2, wg, wu, wd, gf, *, tm, tf, final_norm):
    s, d = x1.shape
    dff = wg.shape[1]
    assert dff // tf >= 2, "first and last inner steps are distinct code paths"
    return pl.pallas_call(
        functools.partial(_ffn_kernel, final_norm=final_norm),
        grid=(s // tm, dff // tf),
        in_specs=[
            pl.BlockSpec(memory_space=pl.ANY),
            pl.BlockSpec((1, d), lambda i, f: (0, 0)),
            pl.BlockSpec((d, tf), lambda i, f: (0, f)),
            pl.BlockSpec((d, tf), lambda i, f: (0, f)),
            pl.BlockSpec((tf, d), lambda i, f: (f, 0)),
            pl.BlockSpec((1, d), lambda i, f: (0, 0)),
        ],
        out_specs=pl.BlockSpec((tm, d), lambda i, f: (i, 0)),
        out_shape=jax.ShapeDtypeStruct((s, d), F32),
        scratch_shapes=[pltpu.VMEM((tm, d), BF16), pltpu.VMEM((tm, d), F32), pltpu.SemaphoreType.DMA(())],
        compiler_params=pltpu.CompilerParams(
            dimension_semantics=("arbitrary", "arbitrary"),
            vmem_limit_bytes=V7X_VMEM_LIMIT_BYTES),
        name="ffn",
    )(x1, g2, wg, wu, wd, gf)


def kernel(x, norm1_g, w_in, sgu_ln_g, sgu_ln_b, w_spatial, b_spatial, ret_gn_g, ret_gn_b,
           w_out, norm2_g, w_gate, w_up, w_down, final_norm_g):
    batch, seq, d = x.shape
    depth = w_in.shape[0]
    outs = []
    for b in range(batch):
        xb = x[b]
        for l in range(depth):
            proj = _in_proj(xb, norm1_g[l][None, :], w_in[l],
                            tm=IN_PROJ_ROWS, tn=IN_PROJ_COLS, n_split=IN_PROJ_DOTS_PER_STEP)
            x1, wg, wu, wd = _mixer_out_proj(proj, sgu_ln_g[l], sgu_ln_b[l], w_spatial[l], b_spatial[l],
                                             ret_gn_g[l], ret_gn_b[l], w_out[l], xb,
                                             w_gate[l], w_up[l], w_down[l],
                                             chunks_per_step=MIXER_CHUNKS_PER_STEP)
            xb = _ffn(x1, norm2_g[l][None, :], wg, wu, wd, final_norm_g[None, :],
                      tm=FFN_ROWS, tf=FFN_COLS, final_norm=(l == depth - 1))
        outs.append(xb)
    return outs[0][None] if batch == 1 else jnp.stack(outs)
```

```python
import functools

import numpy as np

import jax
import jax.numpy as jnp
from jax import lax
from jax.experimental import pallas as pl
from jax.experimental.pallas import tpu as pltpu

CHUNK = 128
HEADS = 8
HEAD_DIM = 128
SGU_WIDTH = HEADS * HEAD_DIM
RET_WIDTH = HEADS * HEAD_DIM
MIX_WIDTH = SGU_WIDTH + RET_WIDTH
IN_WIDTH = 2 * SGU_WIDTH + 4 * RET_WIDTH
ROPE_BASE = 10000.0
EPS = 1e-6

OFF_U, OFF_VS, OFF_Q, OFF_K, OFF_VR, OFF_G = (i * SGU_WIDTH for i in range(6))

V7X_VMEM_LIMIT_BYTES = 60 * 1024 * 1024

N_PROJ_PIECES = 8
W_OUT_STAGE_ROWS = 256

IN_PROJ_ROWS, IN_PROJ_COLS, IN_PROJ_DOTS_PER_STEP = 2048, 1024, 2
MIXER_CHUNKS_PER_STEP = 2
FFN_ROWS, FFN_COLS = 1024, 512

F32 = jnp.float32
BF16 = jnp.bfloat16


def _rms_rows(x, g):
    ms = jnp.mean(x * x, axis=-1, keepdims=True)
    return x * lax.rsqrt(ms + EPS) * g


def _norm_rows(x, g, b):
    mu = jnp.mean(x, axis=-1, keepdims=True)
    d = x - mu
    var = jnp.mean(d * d, axis=-1, keepdims=True)
    return d * lax.rsqrt(var + EPS) * g + b


def _row_tile_copy(x_hbm, x_ref, sem, tile):
    tm = x_ref.shape[0]
    start = pl.multiple_of(tile * tm, tm)
    return pltpu.make_async_copy(x_hbm.at[pl.ds(start, tm), :], x_ref, sem)


def _prefetched_row_tile(x_hbm, x_ref, sem, consume):
    i, j = pl.program_id(0), pl.program_id(1)

    @pl.when((i == 0) & (j == 0))
    def _():
        _row_tile_copy(x_hbm, x_ref, sem, 0).start()

    @pl.when(j == 0)
    def _():
        _row_tile_copy(x_hbm, x_ref, sem, i).wait()
        consume()

    @pl.when((j == 1) & (i + 1 < pl.num_programs(0)))
    def _():
        _row_tile_copy(x_hbm, x_ref, sem, i + 1).start()


def _in_proj_kernel(x_hbm, g_ref, w_ref, o_ref, h_ref, x_ref, x_sem, *, n_split):
    def normalise():
        h_ref[...] = _rms_rows(x_ref[...], g_ref[...]).astype(BF16)

    _prefetched_row_tile(x_hbm, x_ref, x_sem, normalise)
    cols = o_ref.shape[1] // n_split
    for s in range(n_split):
        n = slice(s * cols, (s + 1) * cols)
        o_ref[:, n] = jnp.dot(h_ref[...], w_ref[:, n].astype(BF16),
                              preferred_element_type=F32).astype(o_ref.dtype)


def _in_proj(x, g, w, *, tm, tn, n_split):
    s, d = x.shape
    n = w.shape[1]
    assert n // tn >= 2, "the row-tile prefetch starts at inner step 1"
    return pl.pallas_call(
        functools.partial(_in_proj_kernel, n_split=n_split),
        grid=(s // tm, n // tn),
        in_specs=[
            pl.BlockSpec(memory_space=pl.ANY),
            pl.BlockSpec((1, d), lambda i, j: (0, 0)),
            pl.BlockSpec((d, tn), lambda i, j: (0, j)),
        ],
        out_specs=pl.BlockSpec((tm, tn), lambda i, j: (i, j)),
        out_shape=jax.ShapeDtypeStruct((s, n), BF16),
        scratch_shapes=[pltpu.VMEM((tm, d), BF16), pltpu.VMEM((tm, d), F32), pltpu.SemaphoreType.DMA(())],
        compiler_params=pltpu.CompilerParams(
            dimension_semantics=("arbitrary", "arbitrary"),
            vmem_limit_bytes=V7X_VMEM_LIMIT_BYTES),
        name="in_proj",
    )(x, g, w)


def _mixer_kernel(p_ref, ca_ref, sa_ref, cb_ref, sb_ref, cbs_ref, sbs_ref,
                  lng_ref, lnb_ref, ws_ref, bs_ref, gng_ref, gnb_ref,
                  decay_ref, zeta_ref, xi_ref, wout_hbm, x_ref, wg_hbm, wu_hbm, wd_hbm,
                  x1_ref, wgb_ref, wub_ref, wdb_ref,
                  state_ref, wc_ref, bias_ref, woutb_ref, o_ref, prev_ref, stage_ref, stage_sem,
                  wg_stage, wu_stage, wd_stage, ffn_sem, *, chunks_per_step, gamma_c):
    step = pl.program_id(0)
    last_block = pl.num_programs(0) - 2
    ffn_weights = ((wg_hbm, wg_stage, wgb_ref), (wu_hbm, wu_stage, wub_ref), (wd_hbm, wd_stage, wdb_ref))

    def ffn_weight_copies(at_step):
        block = jnp.minimum(at_step, last_block)
        slot = at_step % 2
        copies = []
        for k, (w_hbm, stage, _) in enumerate(ffn_weights):
            rows = stage.shape[1]
            start = pl.multiple_of(block * rows, 8)
            copies.append(pltpu.make_async_copy(w_hbm.at[pl.ds(start, rows), :], stage.at[slot], ffn_sem.at[k, slot]))
        return copies

    row_id = lax.broadcasted_iota(jnp.int32, (CHUNK, CHUNK), 0)
    col_id = lax.broadcasted_iota(jnp.int32, (CHUNK, CHUNK), 1)

    @pl.when(step == 0)
    def _():
        for copy in ffn_weight_copies(0):
            copy.start()
        state_ref[...] = jnp.zeros_like(state_ref)
        o_ref[...] = jnp.zeros_like(o_ref)
        stage_rows = stage_ref.shape[1]
        n_stage = woutb_ref.shape[0] // stage_rows

        def stage_copy(r):
            return pltpu.make_async_copy(wout_hbm.at[pl.ds(r * stage_rows, stage_rows), :],
                                         stage_ref.at[r % 2], stage_sem.at[r % 2])

        stage_copy(0).start()
        for r in range(n_stage):
            if r + 1 < n_stage:
                stage_copy(r + 1).start()
            stage_copy(r).wait()
            woutb_ref[r * stage_rows:(r + 1) * stage_rows, :] = stage_ref[r % 2].astype(BF16)
        for h in range(HEADS):
            wc_ref[h] = jnp.where(row_id >= col_id, ws_ref[h], 0.0).astype(BF16)
            b_col = jnp.sum(jnp.where(row_id == col_id, bs_ref[h:h + 1, :], 0.0), axis=1, keepdims=True)
            bias_ref[h] = jnp.broadcast_to(b_col, (CHUNK, HEAD_DIM))

    @pl.when(step + 1 < pl.num_programs(0))
    def _():
        for copy in ffn_weight_copies(step + 1):
            copy.start()

    k_scale = HEAD_DIM ** -0.5
    chunk0 = jnp.minimum(step, last_block) * chunks_per_step
    heads = range(HEADS)

    prev_ref[...] = o_ref[...]
    piece_cols = x1_ref.shape[1] // N_PROJ_PIECES
    issued = []

    def project_piece():
        n = slice(len(issued) * piece_cols, (len(issued) + 1) * piece_cols)
        issued.append(n)
        x1_ref[:, n] = x_ref[:, n] + jnp.dot(prev_ref[...], woutb_ref[:, n], preferred_element_type=F32)

    def cols(off, h):
        return slice(off + h * HEAD_DIM, off + (h + 1) * HEAD_DIM)

    def chunk_rows(c):
        return slice(c * CHUNK, (c + 1) * CHUNK)

    def spatial_gate(c):
        rows = chunk_rows(c)
        vn = [_norm_rows(p_ref[rows, cols(OFF_VS, h)].astype(F32), lng_ref[h:h + 1, :], lnb_ref[h:h + 1, :])
              .astype(BF16) for h in heads]
        mixed = [jnp.dot(wc_ref[h], vn[h], preferred_element_type=F32) for h in heads]
        for h in heads:
            u = p_ref[rows, cols(OFF_U, h)].astype(F32)
            o_ref[rows, cols(0, h)] = (u * (mixed[h] + bias_ref[h])).astype(o_ref.dtype)

    def retention_scores(c):
        rows = chunk_rows(c)
        ca = ca_ref[pl.ds(chunk0 + c, 1), :]
        sa = sa_ref[pl.ds(chunk0 + c, 1), :]
        cos2 = ca * cb_ref[...] - sa * sb_ref[...]
        sin2 = sa * cbs_ref[...] + ca * sbs_ref[...]
        cos2k = cos2 * k_scale
        sin2k = sin2 * k_scale
        qb, kr = [], []
        for h in heads:
            q = p_ref[rows, cols(OFF_Q, h)].astype(F32)
            k = p_ref[rows, cols(OFF_K, h)].astype(F32)
            qb.append((q * cos2 + pltpu.roll(q, HEAD_DIM // 2, axis=1) * sin2).astype(BF16))
            kr.append(k * cos2k + pltpu.roll(k, HEAD_DIM // 2, axis=1) * sin2k)
        scores = [lax.dot_general(qb[h], kr[h].astype(BF16), (((1,), (1,)), ((), ())),
                                  preferred_element_type=F32) for h in heads]
        return qb, kr, scores

    def retention_values(c, qb, kr, scores):
        rows = chunk_rows(c)
        vr = [p_ref[rows, cols(OFF_VR, h)] for h in heads]
        scores_b = [(scores[h] * decay_ref[h]).astype(BF16) for h in heads]
        kz = [(kr[h] * zeta_ref[h]).astype(BF16) for h in heads]
        state = [state_ref[h] for h in heads]
        intra = [jnp.dot(scores_b[h], vr[h], preferred_element_type=F32) for h in heads]
        inter = [jnp.dot(qb[h], state[h].astype(BF16), preferred_element_type=F32) for h in heads]
        kv = [lax.dot_general(kz[h], vr[h], (((0,), (0,)), ((), ())), preferred_element_type=F32) for h in heads]
        return intra, inter, kv, state

    def retention_gate(c, intra, inter, kv, state):
        rows = chunk_rows(c)
        for h in heads:
            state_ref[h] = gamma_c[h] * state[h] + kv[h]
            on = _norm_rows(intra[h] + inter[h] * xi_ref[h], gng_ref[h:h + 1, :], gnb_ref[h:h + 1, :])
            g = p_ref[rows, cols(OFF_G, h)].astype(F32)
            o_ref[rows, cols(SGU_WIDTH, h)] = (g * jax.nn.sigmoid(g) * on).astype(o_ref.dtype)

    for c in range(chunks_per_step):
        project_piece()
        spatial_gate(c)
        project_piece()
        scored = retention_scores(c)
        project_piece()
        values = retention_values(c, *scored)
        project_piece()
        retention_gate(c, *values)
    assert len(issued) == N_PROJ_PIECES

    for copy, (_, stage, out_ref) in zip(ffn_weight_copies(step), ffn_weights):
        copy.wait()
        out_ref[...] = stage[step % 2].astype(BF16)


def _mixer_constants(seq):
    half = HEAD_DIM // 2
    inv = 1.0 / (ROPE_BASE ** (np.arange(half, dtype=np.float64) / half))
    inv2 = np.concatenate([inv, inv])
    sign = np.concatenate([-np.ones(half), np.ones(half)])
    ang_a = (CHUNK * np.arange(seq // CHUNK, dtype=np.float64))[:, None] * inv2[None, :]
    ang_b = np.arange(CHUNK, dtype=np.float64)[:, None] * inv2[None, :]
    rot = [np.cos(ang_a), np.sin(ang_a), np.cos(ang_b), np.sin(ang_b),
           sign * np.cos(ang_b), sign * np.sin(ang_b)]

    log_gamma = np.log(1.0 - np.exp2(-5.0 - np.arange(HEADS, dtype=np.float64)))
    idx = np.arange(CHUNK, dtype=np.float64)
    diff = idx[:, None] - idx[None, :]
    decay = np.where(diff[None] >= 0, np.exp(np.maximum(diff, 0.0)[None] * log_gamma[:, None, None]), 0.0)
    zeta = np.exp((CHUNK - 1.0 - idx)[None, :] * log_gamma[:, None])
    xi = np.exp((idx + 1.0)[None, :] * log_gamma[:, None])
    bcast = lambda v: np.broadcast_to(v[:, :, None], (HEADS, CHUNK, HEAD_DIM))
    gamma_c = tuple(float(np.float32(v)) for v in np.exp(CHUNK * log_gamma))
    tables = [jnp.asarray(np.ascontiguousarray(t), dtype=F32) for t in rot + [decay, bcast(zeta), bcast(xi)]]
    return tables, gamma_c


def _mixer_out_proj(proj, lng, lnb, ws, bs, gng, gnb, w_out, x, w_gate, w_up, w_down, *, chunks_per_step):
    s, d = x.shape
    dff = w_gate.shape[1]
    n_chunks = s // CHUNK
    rows = chunks_per_step * CHUNK
    n_tiles = s // rows
    gate_rows, down_rows = d // n_tiles, dff // n_tiles
    cast_block = lambda i: (jnp.minimum(i, n_tiles - 1), 0)
    (ca, sa, cb, sb, cbs, sbs, decay, zeta, xi), gamma_c = _mixer_constants(s)
    full = lambda shape, **kw: pl.BlockSpec(shape, lambda i: (0,) * len(shape), **kw)
    head_tiles = full((HEADS, CHUNK, HEAD_DIM))
    head_rows = full((HEADS, HEAD_DIM))
    mixed_tile = lambda i: (jnp.minimum(i, n_tiles - 1), 0)
    projected_tile = lambda i: (jnp.maximum(i - 1, 0), 0)
    return pl.pallas_call(
        functools.partial(_mixer_kernel, chunks_per_step=chunks_per_step, gamma_c=gamma_c),
        grid=(n_tiles + 1,),
        in_specs=[
            pl.BlockSpec((rows, IN_WIDTH), mixed_tile),
            full((n_chunks, HEAD_DIM)), full((n_chunks, HEAD_DIM)),
            full((CHUNK, HEAD_DIM)), full((CHUNK, HEAD_DIM)), full((CHUNK, HEAD_DIM)), full((CHUNK, HEAD_DIM)),
            head_rows, head_rows, head_tiles, head_rows, head_rows, head_rows,
            head_tiles, head_tiles, head_tiles,
            pl.BlockSpec(memory_space=pl.ANY),
            pl.BlockSpec((rows, d), projected_tile),
            pl.BlockSpec(memory_space=pl.ANY),
            pl.BlockSpec(memory_space=pl.ANY),
            pl.BlockSpec(memory_space=pl.ANY),
        ],
        out_specs=[pl.BlockSpec((rows, d), projected_tile),
                   pl.BlockSpec((gate_rows, dff), cast_block),
                   pl.BlockSpec((gate_rows, dff), cast_block),
                   pl.BlockSpec((down_rows, d), cast_block)],
        out_shape=[jax.ShapeDtypeStruct((s, d), F32),
                   jax.ShapeDtypeStruct(w_gate.shape, BF16),
                   jax.ShapeDtypeStruct(w_up.shape, BF16),
                   jax.ShapeDtypeStruct(w_down.shape, BF16)],
        scratch_shapes=[pltpu.VMEM((HEADS, HEAD_DIM, HEAD_DIM), F32),
                        pltpu.VMEM((HEADS, CHUNK, CHUNK), BF16),
                        pltpu.VMEM((HEADS, CHUNK, HEAD_DIM), F32),
                        pltpu.VMEM((MIX_WIDTH, d), BF16),
                        pltpu.VMEM((rows, MIX_WIDTH), BF16),
                        pltpu.VMEM((rows, MIX_WIDTH), BF16),
                        pltpu.VMEM((2, W_OUT_STAGE_ROWS, d), F32),
                        pltpu.SemaphoreType.DMA((2,)),
                        pltpu.VMEM((2, gate_rows, dff), F32),
                        pltpu.VMEM((2, gate_rows, dff), F32),
                        pltpu.VMEM((2, down_rows, d), F32),
                        pltpu.SemaphoreType.DMA((3, 2))],
        compiler_params=pltpu.CompilerParams(
            dimension_semantics=("arbitrary",),
            vmem_limit_bytes=V7X_VMEM_LIMIT_BYTES),
        name="mixer_out_proj",
    )(proj, ca, sa, cb, sb, cbs, sbs, lng, lnb, ws, bs, gng, gnb, decay, zeta, xi, w_out, x,
      w_gate, w_up, w_down)


def _ffn_kernel(x_hbm, g2_ref, wg_ref, wu_ref, wd_ref, gf_ref, o_ref, h_ref, x_ref, x_sem, *, final_norm):
    f = pl.program_id(1)
    last = pl.num_programs(1) - 1

    def ffn_step(is_first, is_last):
        if is_first:
            base = x_ref[...]
            h = _rms_rows(base, g2_ref[...]).astype(BF16)
            h_ref[...] = h
        else:
            base = o_ref[...]
            h = h_ref[...]
        gate = jnp.dot(h, wg_ref[...], preferred_element_type=F32)
        up = jnp.dot(h, wu_ref[...], preferred_element_type=F32)
        a = (gate * jax.nn.sigmoid(gate) * up).astype(BF16)
        y = base + jnp.dot(a, wd_ref[...], preferred_element_type=F32)
        o_ref[...] = _rms_rows(y, gf_ref[...]) if (is_last and final_norm) else y

    _prefetched_row_tile(x_hbm, x_ref, x_sem, functools.partial(ffn_step, True, False))
    pl.when((f > 0) & (f < last))(functools.partial(ffn_step, False, False))
    pl.when(f == last)(functools.partial(ffn_step, False, True))


def _ffn(x1, g2, wg, wu, wd, gf, *, tm, tf, final_norm):
    s, d = x1.shape
    dff = wg.shape[1]
    assert dff // tf >= 2, "first and last inner steps are distinct code paths"
    return pl.pallas_call(
        functools.partial(_ffn_kernel, final_norm=final_norm),
        grid=(s // tm, dff // tf),
        in_specs=[
            pl.BlockSpec(memory_space=pl.ANY),
            pl.BlockSpec((1, d), lambda i, f: (0, 0)),
            pl.BlockSpec((d, tf), lambda i, f: (0, f)),
            pl.BlockSpec((d, tf), lambda i, f: (0, f)),
            pl.BlockSpec((tf, d), lambda i, f: (f, 0)),
            pl.BlockSpec((1, d), lambda i, f: (0, 0)),
        ],
        out_specs=pl.BlockSpec((tm, d), lambda i, f: (i, 0)),
        out_shape=jax.ShapeDtypeStruct((s, d), F32),
        scratch_shapes=[pltpu.VMEM((tm, d), BF16), pltpu.VMEM((tm, d), F32), pltpu.SemaphoreType.DMA(())],
        compiler_params=pltpu.CompilerParams(
            dimension_semantics=("arbitrary", "arbitrary"),
            vmem_limit_bytes=V7X_VMEM_LIMIT_BYTES),
        name="ffn",
    )(x1, g2, wg, wu, wd, gf)


def kernel(x, norm1_g, w_in, sgu_ln_g, sgu_ln_b, w_spatial, b_spatial, ret_gn_g, ret_gn_b,
           w_out, norm2_g, w_gate, w_up, w_down, final_norm_g):
    batch, seq, d = x.shape
    depth = w_in.shape[0]
    outs = []
    for b in range(batch):
        xb = x[b]
        for l in range(depth):
            proj = _in_proj(xb, norm1_g[l][None, :], w_in[l],
                            tm=IN_PROJ_ROWS, tn=IN_PROJ_COLS, n_split=IN_PROJ_DOTS_PER_STEP)
            x1, wg, wu, wd = _mixer_out_proj(proj, sgu_ln_g[l], sgu_ln_b[l], w_spatial[l], b_spatial[l],
                                             ret_gn_g[l], ret_gn_b[l], w_out[l], xb,
                                             w_gate[l], w_up[l], w_down[l],
                                             chunks_per_step=MIXER_CHUNKS_PER_STEP)
            xb = _ffn(x1, norm2_g[l][None, :], wg, wu, wd, final_norm_g[None, :],
                      tm=FFN_ROWS, tf=FFN_COLS, final_norm=(l == depth - 1))
        outs.append(xb)
    return outs[0][None] if batch == 1 else jnp.stack(outs)
```

```python
import functools

import numpy as np

import jax
import jax.numpy as jnp
from jax import lax
from jax.experimental import pallas as pl
from jax.experimental.pallas import tpu as pltpu

CHUNK = 128
HEADS = 8
HEAD_DIM = 128
SGU_WIDTH = HEADS * HEAD_DIM
RET_WIDTH = HEADS * HEAD_DIM
MIX_WIDTH = SGU_WIDTH + RET_WIDTH
IN_WIDTH = 2 * SGU_WIDTH + 4 * RET_WIDTH
ROPE_BASE = 10000.0
EPS = 1e-6

OFF_U, OFF_VS, OFF_Q, OFF_K, OFF_VR, OFF_G = (i * SGU_WIDTH for i in range(6))

V7X_VMEM_LIMIT_BYTES = 60 * 1024 * 1024

N_PROJ_PIECES = 8
W_OUT_STAGE_ROWS = 256

IN_PROJ_ROWS, IN_PROJ_COLS, IN_PROJ_DOTS_PER_STEP = 2048, 1024, 2
MIXER_CHUNKS_PER_STEP = 2
FFN_ROWS, FFN_COLS = 1024, 512

F32 = jnp.float32
BF16 = jnp.bfloat16


def _rms_rows(x, g):
    ms = jnp.mean(x * x, axis=-1, keepdims=True)
    return x * lax.rsqrt(ms + EPS) * g


def _norm_rows(x, g, b):
    mu = jnp.mean(x, axis=-1, keepdims=True)
    d = x - mu
    var = jnp.mean(d * d, axis=-1, keepdims=True)
    return d * lax.rsqrt(var + EPS) * g + b


def _row_tile_copy(x_hbm, x_ref, sem, tile):
    tm = x_ref.shape[0]
    start = pl.multiple_of(tile * tm, tm)
    return pltpu.make_async_copy(x_hbm.at[pl.ds(start, tm), :], x_ref, sem)


def _prefetched_row_tile(x_hbm, x_ref, sem, consume):
    i, j = pl.program_id(0), pl.program_id(1)

    @pl.when((i == 0) & (j == 0))
    def _():
        _row_tile_copy(x_hbm, x_ref, sem, 0).start()

    @pl.when(j == 0)
    def _():
        _row_tile_copy(x_hbm, x_ref, sem, i).wait()
        consume()

    @pl.when((j == 1) & (i + 1 < pl.num_programs(0)))
    def _():
        _row_tile_copy(x_hbm, x_ref, sem, i + 1).start()


def _in_proj_kernel(x_hbm, g_ref, w_ref, o_ref, h_ref, x_ref, x_sem, *, n_split):
    def normalise():
        h_ref[...] = _rms_rows(x_ref[...], g_ref[...]).astype(BF16)

    _prefetched_row_tile(x_hbm, x_ref, x_sem, normalise)
    cols = o_ref.shape[1] // n_split
    for s in range(n_split):
        n = slice(s * cols, (s + 1) * cols)
        o_ref[:, n] = jnp.dot(h_ref[...], w_ref[:, n].astype(BF16),
                              preferred_element_type=F32).astype(o_ref.dtype)


def _in_proj(x, g, w, *, tm, tn, n_split):
    s, d = x.shape
    n = w.shape[1]
    assert n // tn >= 2, "the row-tile prefetch starts at inner step 1"
    return pl.pallas_call(
        functools.partial(_in_proj_kernel, n_split=n_split),
        grid=(s // tm, n // tn),
        in_specs=[
            pl.BlockSpec(memory_space=pl.ANY),
            pl.BlockSpec((1, d), lambda i, j: (0, 0)),
            pl.BlockSpec((d, tn), lambda i, j: (0, j)),
        ],
        out_specs=pl.BlockSpec((tm, tn), lambda i, j: (i, j)),
        out_shape=jax.ShapeDtypeStruct((s, n), BF16),
        scratch_shapes=[pltpu.VMEM((tm, d), BF16), pltpu.VMEM((tm, d), F32), pltpu.SemaphoreType.DMA(())],
        compiler_params=pltpu.CompilerParams(
            dimension_semantics=("arbitrary", "arbitrary"),
            vmem_limit_bytes=V7X_VMEM_LIMIT_BYTES),
        name="in_proj",
    )(x, g, w)


def _mixer_kernel(p_ref, ca_ref, sa_ref, cb_ref, sb_ref, cbs_ref, sbs_ref,
                  lng_ref, lnb_ref, ws_ref, bs_ref, gng_ref, gnb_ref,
                  decay_ref, zeta_ref, xi_ref, wout_hbm, x_ref, wg_hbm, wu_hbm, wd_hbm,
                  x1_ref, wgb_ref, wub_ref, wdb_ref,
                  state_ref, wc_ref, bias_ref, woutb_ref, o_ref, prev_ref, stage_ref, stage_sem,
                  wg_stage, wu_stage, wd_stage, ffn_sem, *, chunks_per_step, gamma_c):
    step = pl.program_id(0)
    last_block = pl.num_programs(0) - 2
    ffn_weights = ((wg_hbm, wg_stage, wgb_ref), (wu_hbm, wu_stage, wub_ref), (wd_hbm, wd_stage, wdb_ref))

    def ffn_weight_copies(at_step):
        block = jnp.minimum(at_step, last_block)
        slot = at_step % 2
        copies = []
        for k, (w_hbm, stage, _) in enumerate(ffn_weights):
            rows = stage.shape[1]
            start = pl.multiple_of(block * rows, 8)
            copies.append(pltpu.make_async_copy(w_hbm.at[pl.ds(start, rows), :], stage.at[slot], ffn_sem.at[k, slot]))
        return copies

    row_id = lax.broadcasted_iota(jnp.int32, (CHUNK, CHUNK), 0)
    col_id = lax.broadcasted_iota(jnp.int32, (CHUNK, CHUNK), 1)

    @pl.when(step == 0)
    def _():
        for copy in ffn_weight_copies(0):
            copy.start()
        state_ref[...] = jnp.zeros_like(state_ref)
        o_ref[...] = jnp.zeros_like(o_ref)
        stage_rows = stage_ref.shape[1]
        n_stage = woutb_ref.shape[0] // stage_rows

        def stage_copy(r):
            return pltpu.make_async_copy(wout_hbm.at[pl.ds(r * stage_rows, stage_rows), :],
                                         stage_ref.at[r % 2], stage_sem.at[r % 2])

        stage_copy(0).start()
        for r in range(n_stage):
            if r + 1 < n_stage:
                stage_copy(r + 1).start()
            stage_copy(r).wait()
            woutb_ref[r * stage_rows:(r + 1) * stage_rows, :] = stage_ref[r % 2].astype(BF16)
        for h in range(HEADS):
            wc_ref[h] = jnp.where(row_id >= col_id, ws_ref[h], 0.0).astype(BF16)
            b_col = jnp.sum(jnp.where(row_id == col_id, bs_ref[h:h + 1, :], 0.0), axis=1, keepdims=True)
            bias_ref[h] = jnp.broadcast_to(b_col, (CHUNK, HEAD_DIM))

    @pl.when(step + 1 < pl.num_programs(0))
    def _():
        for copy in ffn_weight_copies(step + 1):
            copy.start()

    k_scale = HEAD_DIM ** -0.5
    chunk0 = jnp.minimum(step, last_block) * chunks_per_step
    heads = range(HEADS)

    prev_ref[...] = o_ref[...]
    piece_cols = x1_ref.shape[1] // N_PROJ_PIECES
    issued = []

    def project_piece():
        n = slice(len(issued) * piece_cols, (len(issued) + 1) * piece_cols)
        issued.append(n)
        x1_ref[:, n] = x_ref[:, n] + jnp.dot(prev_ref[...], woutb_ref[:, n], preferred_element_type=F32)

    def cols(off, h):
        return slice(off + h * HEAD_DIM, off + (h + 1) * HEAD_DIM)

    def chunk_rows(c):
        return slice(c * CHUNK, (c + 1) * CHUNK)

    def spatial_gate(c):
        rows = chunk_rows(c)
        vn = [_norm_rows(p_ref[rows, cols(OFF_VS, h)].astype(F32), lng_ref[h:h + 1, :], lnb_ref[h:h + 1, :])
              .astype(BF16) for h in heads]
        mixed = [jnp.dot(wc_ref[h], vn[h], preferred_element_type=F32) for h in heads]
        for h in heads:
            u = p_ref[rows, cols(OFF_U, h)].astype(F32)
            o_ref[rows, cols(0, h)] = (u * (mixed[h] + bias_ref[h])).astype(o_ref.dtype)

    def retention_scores(c):
        rows = chunk_rows(c)
        ca = ca_ref[pl.ds(chunk0 + c, 1), :]
        sa = sa_ref[pl.ds(chunk0 + c, 1), :]
        cos2 = ca * cb_ref[...] - sa * sb_ref[...]
        sin2 = sa * cbs_ref[...] + ca * sbs_ref[...]
        cos2k = cos2 * k_scale
        sin2k = sin2 * k_scale
        qb, kr = [], []
        for h in heads:
            q = p_ref[rows, cols(OFF_Q, h)].astype(F32)
            k = p_ref[rows, cols(OFF_K, h)].astype(F32)
            qb.append((q * cos2 + pltpu.roll(q, HEAD_DIM // 2, axis=1) * sin2).astype(BF16))
            kr.append(k * cos2k + pltpu.roll(k, HEAD_DIM // 2, axis=1) * sin2k)
        scores = [lax.dot_general(qb[h], kr[h].astype(BF16), (((1,), (1,)), ((), ())),
                                  preferred_element_type=F32) for h in heads]
        return qb, kr, scores

    def retention_values(c, qb, kr, scores):
        rows = chunk_rows(c)
        vr = [p_ref[rows, cols(OFF_VR, h)] for h in heads]
        scores_b = [(scores[h] * decay_ref[h]).astype(BF16) for h in heads]
        kz = [(kr[h] * zeta_ref[h]).astype(BF16) for h in heads]
        state = [state_ref[h] for h in heads]
        intra = [jnp.dot(scores_b[h], vr[h], preferred_element_type=F32) for h in heads]
        inter = [jnp.dot(qb[h], state[h].astype(BF16), preferred_element_type=F32) for h in heads]
        kv = [lax.dot_general(kz[h], vr[h], (((0,), (0,)), ((), ())), preferred_element_type=F32) for h in heads]
        return intra, inter, kv, state

    def retention_gate(c, intra, inter, kv, state):
        rows = chunk_rows(c)
        for h in heads:
            state_ref[h] = gamma_c[h] * state[h] + kv[h]
            on = _norm_rows(intra[h] + inter[h] * xi_ref[h], gng_ref[h:h + 1, :], gnb_ref[h:h + 1, :])
            g = p_ref[rows, cols(OFF_G, h)].astype(F32)
            o_ref[rows, cols(SGU_WIDTH, h)] = (g * jax.nn.sigmoid(g) * on).astype(o_ref.dtype)

    pending_gate = None
    for c in range(chunks_per_step):
        project_piece()
        spatial_gate(c)
        project_piece()
        if pending_gate is not None:
            retention_gate(*pending_gate)
        scored = retention_scores(c)
        project_piece()
        pending_gate = (c,) + tuple(retention_values(c, *scored))
        if c + 1 < chunks_per_step:
            project_piece()
    retention_gate(*pending_gate)
    project_piece()
    assert len(issued) == N_PROJ_PIECES

    for copy, (_, stage, out_ref) in zip(ffn_weight_copies(step), ffn_weights):
        copy.wait()
        out_ref[...] = stage[step % 2].astype(BF16)


def _mixer_constants(seq):
    half = HEAD_DIM // 2
    inv = 1.0 / (ROPE_BASE ** (np.arange(half, dtype=np.float64) / half))
    inv2 = np.concatenate([inv, inv])
    sign = np.concatenate([-np.ones(half), np.ones(half)])
    ang_a = (CHUNK * np.arange(seq // CHUNK, dtype=np.float64))[:, None] * inv2[None, :]
    ang_b = np.arange(CHUNK, dtype=np.float64)[:, None] * inv2[None, :]
    rot = [np.cos(ang_a), np.sin(ang_a), np.cos(ang_b), np.sin(ang_b),
           sign * np.cos(ang_b), sign * np.sin(ang_b)]

    log_gamma = np.log(1.0 - np.exp2(-5.0 - np.arange(HEADS, dtype=np.float64)))
    idx = np.arange(CHUNK, dtype=np.float64)
    diff = idx[:, None] - idx[None, :]
    decay = np.where(diff[None] >= 0, np.exp(np.maximum(diff, 0.0)[None] * log_gamma[:, None, None]), 0.0)
    zeta = np.exp((CHUNK - 1.0 - idx)[None, :] * log_gamma[:, None])
    xi = np.exp((idx + 1.0)[None, :] * log_gamma[:, None])
    bcast = lambda v: np.broadcast_to(v[:, :, None], (HEADS, CHUNK, HEAD_DIM))
    gamma_c = tuple(float(np.float32(v)) for v in np.exp(CHUNK * log_gamma))
    tables = [jnp.asarray(np.ascontiguousarray(t), dtype=F32) for t in rot + [decay, bcast(zeta), bcast(xi)]]
    return tables, gamma_c


def _mixer_out_proj(proj, lng, lnb, ws, bs, gng, gnb, w_out, x, w_gate, w_up, w_down, *, chunks_per_step):
    s, d = x.shape
    dff = w_gate.shape[1]
    n_chunks = s // CHUNK
    rows = chunks_per_step * CHUNK
    n_tiles = s // rows
    gate_rows, down_rows = d // n_tiles, dff // n_tiles
    cast_block = lambda i: (jnp.minimum(i, n_tiles - 1), 0)
    (ca, sa, cb, sb, cbs, sbs, decay, zeta, xi), gamma_c = _mixer_constants(s)
    full = lambda shape, **kw: pl.BlockSpec(shape, lambda i: (0,) * len(shape), **kw)
    head_tiles = full((HEADS, CHUNK, HEAD_DIM))
    head_rows = full((HEADS, HEAD_DIM))
    mixed_tile = lambda i: (jnp.minimum(i, n_tiles - 1), 0)
    projected_tile = lambda i: (jnp.maximum(i - 1, 0), 0)
    return pl.pallas_call(
        functools.partial(_mixer_kernel, chunks_per_step=chunks_per_step, gamma_c=gamma_c),
        grid=(n_tiles + 1,),
        in_specs=[
            pl.BlockSpec((rows, IN_WIDTH), mixed_tile),
            full((n_chunks, HEAD_DIM)), full((n_chunks, HEAD_DIM)),
            full((CHUNK, HEAD_DIM)), full((CHUNK, HEAD_DIM)), full((CHUNK, HEAD_DIM)), full((CHUNK, HEAD_DIM)),
            head_rows, head_rows, head_tiles, head_rows, head_rows, head_rows,
            head_tiles, head_tiles, head_tiles,
            pl.BlockSpec(memory_space=pl.ANY),
            pl.BlockSpec((rows, d), projected_tile),
            pl.BlockSpec(memory_space=pl.ANY),
            pl.BlockSpec(memory_space=pl.ANY),
            pl.BlockSpec(memory_space=pl.ANY),
        ],
        out_specs=[pl.BlockSpec((rows, d), projected_tile),
                   pl.BlockSpec((gate_rows, dff), cast_block),
                   pl.BlockSpec((gate_rows, dff), cast_block),
                   pl.BlockSpec((down_rows, d), cast_block)],
        out_shape=[jax.ShapeDtypeStruct((s, d), F32),
                   jax.ShapeDtypeStruct(w_gate.shape, BF16),
                   jax.ShapeDtypeStruct(w_up.shape, BF16),
                   jax.ShapeDtypeStruct(w_down.shape, BF16)],
        scratch_shapes=[pltpu.VMEM((HEADS, HEAD_DIM, HEAD_DIM), F32),
                        pltpu.VMEM((HEADS, CHUNK, CHUNK), BF16),
                        pltpu.VMEM((HEADS, CHUNK, HEAD_DIM), F32),
                        pltpu.VMEM((MIX_WIDTH, d), BF16),
                        pltpu.VMEM((rows, MIX_WIDTH), BF16),
                        pltpu.VMEM((rows, MIX_WIDTH), BF16),
                        pltpu.VMEM((2, W_OUT_STAGE_ROWS, d), F32),
                        pltpu.SemaphoreType.DMA((2,)),
                        pltpu.VMEM((2, gate_rows, dff), F32),
                        pltpu.VMEM((2, gate_rows, dff), F32),
                        pltpu.VMEM((2, down_rows, d), F32),
                        pltpu.SemaphoreType.DMA((3, 2))],
        compiler_params=pltpu.CompilerParams(
            dimension_semantics=("arbitrary",),
            vmem_limit_bytes=V7X_VMEM_LIMIT_BYTES),
        name="mixer_out_proj",
    )(proj, ca, sa, cb, sb, cbs, sbs, lng, lnb, ws, bs, gng, gnb, decay, zeta, xi, w_out, x,
      w_gate, w_up, w_down)


def _ffn_kernel(x_hbm, g2_ref, wg_ref, wu_ref, wd_ref, gf_ref, o_ref, h_ref, x_ref, x_sem, *, final_norm):
    f = pl.program_id(1)
    last = pl.num_programs(1) - 1

    def ffn_step(is_first, is_last):
        if is_first:
            base = x_ref[...]
            h = _rms_rows(base, g2_ref[...]).astype(BF16)
            h_ref[...] = h
        else:
            base = o_ref[...]
            h = h_ref[...]
        gate = jnp.dot(h, wg_ref[...], preferred_element_type=F32)
        up = jnp.dot(h, wu_ref[...], preferred_element_type=F32)
        a = (gate * jax.nn.sigmoid(gate) * up).astype(BF16)
        y = base + jnp.dot(a, wd_ref[...], preferred_element_type=F32)
        o_ref[...] = _rms_rows(y, gf_ref[...]) if (is_last and final_norm) else y

    _prefetched_row_tile(x_hbm, x_ref, x_sem, functools.partial(ffn_step, True, False))
    pl.when((f > 0) & (f < last))(functools.partial(ffn_step, False, False))
    pl.when(f == last)(functools.partial(ffn_step, False, True))


def _ffn(x1, g2, wg, wu, wd, gf, *, tm, tf, final_norm):
    s, d = x1.shape
    dff = wg.shape[1]
    assert dff // tf >= 2, "first and last inner steps are distinct code paths"
    return pl.pallas_call(
        functools.partial(_ffn_kernel, final_norm=final_norm),
        grid=(s // tm, dff // tf),
        in_specs=[
            pl.BlockSpec(memory_space=pl.ANY),
            pl.BlockSpec((1, d), lambda i, f: (0, 0)),
            pl.BlockSpec((d, tf), lambda i, f: (0, f)),
            pl.BlockSpec((d, tf), lambda i, f: (0, f)),
            pl.BlockSpec((tf, d), lambda i, f: (f, 0)),
            pl.BlockSpec((1, d), lambda i, f: (0, 0)),
        ],
        out_specs=pl.BlockSpec((tm, d), lambda i, f: (i, 0)),
        out_shape=jax.ShapeDtypeStruct((s, d), F32),
        scratch_shapes=[pltpu.VMEM((tm, d), BF16), pltpu.VMEM((tm, d), F32), pltpu.SemaphoreType.DMA(())],
        compiler_params=pltpu.CompilerParams(
            dimension_semantics=("arbitrary", "arbitrary"),
            vmem_limit_bytes=V7X_VMEM_LIMIT_BYTES),
        name="ffn",
    )(x1, g2, wg, wu, wd, gf)


def kernel(x, norm1_g, w_in, sgu_ln_g, sgu_ln_b, w_spatial, b_spatial, ret_gn_g, ret_gn_b,
           w_out, norm2_g, w_gate, w_up, w_down, final_norm_g):
    batch, seq, d = x.shape
    depth = w_in.shape[0]
    outs = []
    for b in range(batch):
        xb = x[b]
        for l in range(depth):
            proj = _in_proj(xb, norm1_g[l][None, :], w_in[l],
                            tm=IN_PROJ_ROWS, tn=IN_PROJ_COLS, n_split=IN_PROJ_DOTS_PER_STEP)
            x1, wg, wu, wd = _mixer_out_proj(proj, sgu_ln_g[l], sgu_ln_b[l], w_spatial[l], b_spatial[l],
                                             ret_gn_g[l], ret_gn_b[l], w_out[l], xb,
                                             w_gate[l], w_up[l], w_down[l],
                                             chunks_per_step=MIXER_CHUNKS_PER_STEP)
            xb = _ffn(x1, norm2_g[l][None, :], wg, wu, wd, final_norm_g[None, :],
                      tm=FFN_ROWS, tf=FFN_COLS, final_norm=(l == depth - 1))
        outs.append(xb)
    return outs[0][None] if batch == 1 else jnp.stack(outs)
```

```python
import functools

import numpy as np

import jax
import jax.numpy as jnp
from jax import lax
from jax.experimental import pallas as pl
from jax.experimental.pallas import tpu as pltpu

CHUNK = 128
HEADS = 8
HEAD_DIM = 128
SGU_WIDTH = HEADS * HEAD_DIM
RET_WIDTH = HEADS * HEAD_DIM
MIX_WIDTH = SGU_WIDTH + RET_WIDTH
IN_WIDTH = 2 * SGU_WIDTH + 4 * RET_WIDTH
ROPE_BASE = 10000.0
EPS = 1e-6

OFF_U, OFF_VS, OFF_Q, OFF_K, OFF_VR, OFF_G = (i * SGU_WIDTH for i in range(6))

V7X_VMEM_LIMIT_BYTES = 60 * 1024 * 1024

N_PROJ_PIECES = 8
W_OUT_STAGE_ROWS = 256

IN_PROJ_ROWS, IN_PROJ_COLS, IN_PROJ_DOTS_PER_STEP = 2048, 1024, 2
MIXER_CHUNKS_PER_STEP = 2
FFN_ROWS, FFN_COLS = 1024, 512

F32 = jnp.float32
BF16 = jnp.bfloat16


def _rms_rows(x, g):
    ms = jnp.mean(x * x, axis=-1, keepdims=True)
    return x * lax.rsqrt(ms + EPS) * g


def _norm_rows(x, g, b):
    mu = jnp.mean(x, axis=-1, keepdims=True)
    d = x - mu
    var = jnp.mean(d * d, axis=-1, keepdims=True)
    return d * lax.rsqrt(var + EPS) * g + b


def _row_tile_copy(x_hbm, x_ref, sem, tile):
    tm = x_ref.shape[0]
    start = pl.multiple_of(tile * tm, tm)
    return pltpu.make_async_copy(x_hbm.at[pl.ds(start, tm), :], x_ref, sem)


def _prefetched_row_tile(x_hbm, x_ref, sem, consume):
    i, j = pl.program_id(0), pl.program_id(1)

    @pl.when((i == 0) & (j == 0))
    def _():
        _row_tile_copy(x_hbm, x_ref, sem, 0).start()

    @pl.when(j == 0)
    def _():
        _row_tile_copy(x_hbm, x_ref, sem, i).wait()
        consume()

    @pl.when((j == 1) & (i + 1 < pl.num_programs(0)))
    def _():
        _row_tile_copy(x_hbm, x_ref, sem, i + 1).start()


def _in_proj_kernel(x_hbm, g_ref, w_ref, o_ref, h_ref, x_ref, x_sem, *, n_split):
    def normalise():
        h_ref[...] = _rms_rows(x_ref[...], g_ref[...]).astype(BF16)

    _prefetched_row_tile(x_hbm, x_ref, x_sem, normalise)
    cols = o_ref.shape[1] // n_split
    for s in range(n_split):
        n = slice(s * cols, (s + 1) * cols)
        o_ref[:, n] = jnp.dot(h_ref[...], w_ref[:, n].astype(BF16),
                              preferred_element_type=F32).astype(o_ref.dtype)


def _in_proj(x, g, w, *, tm, tn, n_split):
    s, d = x.shape
    n = w.shape[1]
    assert n // tn >= 2, "the row-tile prefetch starts at inner step 1"
    return pl.pallas_call(
        functools.partial(_in_proj_kernel, n_split=n_split),
        grid=(s // tm, n // tn),
        in_specs=[
            pl.BlockSpec(memory_space=pl.ANY),
            pl.BlockSpec((1, d), lambda i, j: (0, 0)),
            pl.BlockSpec((d, tn), lambda i, j: (0, j)),
        ],
        out_specs=pl.BlockSpec((tm, tn), lambda i, j: (i, j)),
        out_shape=jax.ShapeDtypeStruct((s, n), BF16),
        scratch_shapes=[pltpu.VMEM((tm, d), BF16), pltpu.VMEM((tm, d), F32), pltpu.SemaphoreType.DMA(())],
        compiler_params=pltpu.CompilerParams(
            dimension_semantics=("arbitrary", "arbitrary"),
            vmem_limit_bytes=V7X_VMEM_LIMIT_BYTES),
        name="in_proj",
    )(x, g, w)


def _mixer_kernel(p_ref, ca_ref, sa_ref, cb_ref, sb_ref, cbs_ref, sbs_ref,
                  lng_ref, lnb_ref, ws_ref, bs_ref, gng_ref, gnb_ref,
                  decay_ref, zeta_ref, xi_ref, wout_hbm, x_ref, wg_hbm, wu_hbm, wd_hbm,
                  x1_ref, wgb_ref, wub_ref, wdb_ref,
                  state_ref, wc_ref, bias_ref, woutb_ref, o_ref, prev_ref, stage_ref, stage_sem,
                  wg_stage, wu_stage, wd_stage, ffn_sem, *, chunks_per_step, gamma_c):
    step = pl.program_id(0)
    last_block = pl.num_programs(0) - 2
    ffn_weights = ((wg_hbm, wg_stage, wgb_ref), (wu_hbm, wu_stage, wub_ref), (wd_hbm, wd_stage, wdb_ref))

    def ffn_weight_copies(at_step):
        block = jnp.minimum(at_step, last_block)
        slot = at_step % 2
        copies = []
        for k, (w_hbm, stage, _) in enumerate(ffn_weights):
            rows = stage.shape[1]
            start = pl.multiple_of(block * rows, 8)
            copies.append(pltpu.make_async_copy(w_hbm.at[pl.ds(start, rows), :], stage.at[slot], ffn_sem.at[k, slot]))
        return copies

    row_id = lax.broadcasted_iota(jnp.int32, (CHUNK, CHUNK), 0)
    col_id = lax.broadcasted_iota(jnp.int32, (CHUNK, CHUNK), 1)

    @pl.when(step == 0)
    def _():
        for copy in ffn_weight_copies(0):
            copy.start()
        state_ref[...] = jnp.zeros_like(state_ref)
        o_ref[...] = jnp.zeros_like(o_ref)
        stage_rows = stage_ref.shape[1]
        n_stage = woutb_ref.shape[0] // stage_rows

        def stage_copy(r):
            return pltpu.make_async_copy(wout_hbm.at[pl.ds(r * stage_rows, stage_rows), :],
                                         stage_ref.at[r % 2], stage_sem.at[r % 2])

        stage_copy(0).start()
        for r in range(n_stage):
            if r + 1 < n_stage:
                stage_copy(r + 1).start()
            stage_copy(r).wait()
            woutb_ref[r * stage_rows:(r + 1) * stage_rows, :] = stage_ref[r % 2].astype(BF16)
        for h in range(HEADS):
            wc_ref[h] = jnp.where(row_id >= col_id, ws_ref[h], 0.0).astype(BF16)
            b_col = jnp.sum(jnp.where(row_id == col_id, bs_ref[h:h + 1, :], 0.0), axis=1, keepdims=True)
            bias_ref[h] = jnp.broadcast_to(b_col, (CHUNK, HEAD_DIM))

    @pl.when(step + 1 < pl.num_programs(0))
    def _():
        for copy in ffn_weight_copies(step + 1):
            copy.start()

    k_scale = HEAD_DIM ** -0.5
    chunk0 = jnp.minimum(step, last_block) * chunks_per_step
    heads = range(HEADS)

    prev_ref[...] = o_ref[...]
    piece_cols = x1_ref.shape[1] // N_PROJ_PIECES
    issued = []

    def project_piece():
        n = slice(len(issued) * piece_cols, (len(issued) + 1) * piece_cols)
        issued.append(n)
        x1_ref[:, n] = x_ref[:, n] + jnp.dot(prev_ref[...], woutb_ref[:, n], preferred_element_type=F32)

    def cols(off, h):
        return slice(off + h * HEAD_DIM, off + (h + 1) * HEAD_DIM)

    def chunk_rows(c):
        return slice(c * CHUNK, (c + 1) * CHUNK)

    def spatial_gate(c):
        rows = chunk_rows(c)
        vn = [_norm_rows(p_ref[rows, cols(OFF_VS, h)].astype(F32), lng_ref[h:h + 1, :], lnb_ref[h:h + 1, :])
              .astype(BF16) for h in heads]
        mixed = [jnp.dot(wc_ref[h], vn[h], preferred_element_type=F32) for h in heads]
        for h in heads:
            u = p_ref[rows, cols(OFF_U, h)].astype(F32)
            o_ref[rows, cols(0, h)] = (u * (mixed[h] + bias_ref[h])).astype(o_ref.dtype)

    def retention_scores():
        rows = slice(None)
        cos2, sin2 = [], []
        for c in range(chunks_per_step):
            ca = ca_ref[pl.ds(chunk0 + c, 1), :]
            sa = sa_ref[pl.ds(chunk0 + c, 1), :]
            cos2.append(ca * cb_ref[...] - sa * sb_ref[...])
            sin2.append(sa * cbs_ref[...] + ca * sbs_ref[...])
        cos2 = jnp.concatenate(cos2, axis=0)
        sin2 = jnp.concatenate(sin2, axis=0)
        cos2k = cos2 * k_scale
        sin2k = sin2 * k_scale
        qb, kr = [], []
        for h in heads:
            q = p_ref[rows, cols(OFF_Q, h)].astype(F32)
            k = p_ref[rows, cols(OFF_K, h)].astype(F32)
            qb.append((q * cos2 + pltpu.roll(q, HEAD_DIM // 2, axis=1) * sin2).astype(BF16))
            kr.append(k * cos2k + pltpu.roll(k, HEAD_DIM // 2, axis=1) * sin2k)
        scores = [lax.dot_general(qb[h], kr[h].astype(BF16), (((1,), (1,)), ((), ())),
                                  preferred_element_type=F32) for h in heads]
        return qb, kr, scores

    def retention_values(qb, kr, scores):
        rows = slice(None)
        vr = [p_ref[rows, cols(OFF_VR, h)] for h in heads]
        scores_b = [(scores[h] * decay_ref[h]).astype(BF16) for h in heads]
        kz = [(kr[h] * zeta_ref[h]).astype(BF16) for h in heads]
        state = [state_ref[h] for h in heads]
        intra = [jnp.dot(scores_b[h], vr[h], preferred_element_type=F32) for h in heads]
        inter = [jnp.dot(qb[h], state[h].astype(BF16), preferred_element_type=F32) for h in heads]
        kv = [lax.dot_general(kz[h], vr[h], (((0,), (0,)), ((), ())), preferred_element_type=F32) for h in heads]
        return intra, inter, kv, state

    def retention_gate(intra, inter, kv, state):
        rows = slice(None)
        for h in heads:
            state_ref[h] = gamma_c[h] * state[h] + kv[h]
            on = _norm_rows(intra[h] + inter[h] * xi_ref[h], gng_ref[h:h + 1, :], gnb_ref[h:h + 1, :])
            g = p_ref[rows, cols(OFF_G, h)].astype(F32)
            o_ref[rows, cols(SGU_WIDTH, h)] = (g * jax.nn.sigmoid(g) * on).astype(o_ref.dtype)

    later_pieces = N_PROJ_PIECES - chunks_per_step
    for c in range(chunks_per_step):
        project_piece()
        spatial_gate(c)
    for _ in range(later_pieces // 3):
        project_piece()
    scored = retention_scores()
    for _ in range(later_pieces // 3):
        project_piece()
    values = retention_values(*scored)
    for _ in range(later_pieces - 2 * (later_pieces // 3)):
        project_piece()
    retention_gate(*values)
    assert len(issued) == N_PROJ_PIECES

    for copy, (_, stage, out_ref) in zip(ffn_weight_copies(step), ffn_weights):
        copy.wait()
        out_ref[...] = stage[step % 2].astype(BF16)


def _mixer_constants(seq, block):
    half = HEAD_DIM // 2
    inv = 1.0 / (ROPE_BASE ** (np.arange(half, dtype=np.float64) / half))
    inv2 = np.concatenate([inv, inv])
    sign = np.concatenate([-np.ones(half), np.ones(half)])
    ang_a = (CHUNK * np.arange(seq // CHUNK, dtype=np.float64))[:, None] * inv2[None, :]
    ang_b = np.arange(CHUNK, dtype=np.float64)[:, None] * inv2[None, :]
    rot = [np.cos(ang_a), np.sin(ang_a), np.cos(ang_b), np.sin(ang_b),
           sign * np.cos(ang_b), sign * np.sin(ang_b)]

    log_gamma = np.log(1.0 - np.exp2(-5.0 - np.arange(HEADS, dtype=np.float64)))
    idx = np.arange(block, dtype=np.float64)
    diff = idx[:, None] - idx[None, :]
    decay = np.where(diff[None] >= 0, np.exp(np.maximum(diff, 0.0)[None] * log_gamma[:, None, None]), 0.0)
    zeta = np.exp((block - 1.0 - idx)[None, :] * log_gamma[:, None])
    xi = np.exp((idx + 1.0)[None, :] * log_gamma[:, None])
    bcast = lambda v: np.broadcast_to(v[:, :, None], (HEADS, block, HEAD_DIM))
    gamma_c = tuple(float(np.float32(v)) for v in np.exp(block * log_gamma))
    tables = [jnp.asarray(np.ascontiguousarray(t), dtype=F32) for t in rot + [decay, bcast(zeta), bcast(xi)]]
    return tables, gamma_c


def _mixer_out_proj(proj, lng, lnb, ws, bs, gng, gnb, w_out, x, w_gate, w_up, w_down, *, chunks_per_step):
    s, d = x.shape
    dff = w_gate.shape[1]
    n_chunks = s // CHUNK
    rows = chunks_per_step * CHUNK
    n_tiles = s // rows
    gate_rows, down_rows = d // n_tiles, dff // n_tiles
    cast_block = lambda i: (jnp.minimum(i, n_tiles - 1), 0)
    (ca, sa, cb, sb, cbs, sbs, decay, zeta, xi), gamma_c = _mixer_constants(s, rows)
    full = lambda shape, **kw: pl.BlockSpec(shape, lambda i: (0,) * len(shape), **kw)
    head_tiles = full((HEADS, CHUNK, HEAD_DIM))
    block_tiles = full((HEADS, rows, HEAD_DIM))
    head_rows = full((HEADS, HEAD_DIM))
    mixed_tile = lambda i: (jnp.minimum(i, n_tiles - 1), 0)
    projected_tile = lambda i: (jnp.maximum(i - 1, 0), 0)
    return pl.pallas_call(
        functools.partial(_mixer_kernel, chunks_per_step=chunks_per_step, gamma_c=gamma_c),
        grid=(n_tiles + 1,),
        in_specs=[
            pl.BlockSpec((rows, IN_WIDTH), mixed_tile),
            full((n_chunks, HEAD_DIM)), full((n_chunks, HEAD_DIM)),
            full((CHUNK, HEAD_DIM)), full((CHUNK, HEAD_DIM)), full((CHUNK, HEAD_DIM)), full((CHUNK, HEAD_DIM)),
            head_rows, head_rows, head_tiles, head_rows, head_rows, head_rows,
            full((HEADS, rows, rows)), block_tiles, block_tiles,
            pl.BlockSpec(memory_space=pl.ANY),
            pl.BlockSpec((rows, d), projected_tile),
            pl.BlockSpec(memory_space=pl.ANY),
            pl.BlockSpec(memory_space=pl.ANY),
            pl.BlockSpec(memory_space=pl.ANY),
        ],
        out_specs=[pl.BlockSpec((rows, d), projected_tile),
                   pl.BlockSpec((gate_rows, dff), cast_block),
                   pl.BlockSpec((gate_rows, dff), cast_block),
                   pl.BlockSpec((down_rows, d), cast_block)],
        out_shape=[jax.ShapeDtypeStruct((s, d), F32),
                   jax.ShapeDtypeStruct(w_gate.shape, BF16),
                   jax.ShapeDtypeStruct(w_up.shape, BF16),
                   jax.ShapeDtypeStruct(w_down.shape, BF16)],
        scratch_shapes=[pltpu.VMEM((HEADS, HEAD_DIM, HEAD_DIM), F32),
                        pltpu.VMEM((HEADS, CHUNK, CHUNK), BF16),
                        pltpu.VMEM((HEADS, CHUNK, HEAD_DIM), F32),
                        pltpu.VMEM((MIX_WIDTH, d), BF16),
                        pltpu.VMEM((rows, MIX_WIDTH), BF16),
                        pltpu.VMEM((rows, MIX_WIDTH), BF16),
                        pltpu.VMEM((2, W_OUT_STAGE_ROWS, d), F32),
                        pltpu.SemaphoreType.DMA((2,)),
                        pltpu.VMEM((2, gate_rows, dff), F32),
                        pltpu.VMEM((2, gate_rows, dff), F32),
                        pltpu.VMEM((2, down_rows, d), F32),
                        pltpu.SemaphoreType.DMA((3, 2))],
        compiler_params=pltpu.CompilerParams(
            dimension_semantics=("arbitrary",),
            vmem_limit_bytes=V7X_VMEM_LIMIT_BYTES),
        name="mixer_out_proj",
    )(proj, ca, sa, cb, sb, cbs, sbs, lng, lnb, ws, bs, gng, gnb, decay, zeta, xi, w_out, x,
      w_gate, w_up, w_down)


def _ffn_kernel(x_hbm, g2_ref, wg_ref, wu_ref, wd_ref, gf_ref, o_ref, h_ref, x_ref, x_sem, *, final_norm):
    f = pl.program_id(1)
    last = pl.num_programs(1) - 1

    def ffn_step(is_first, is_last):
        if is_first:
            base = x_ref[...]
            h = _rms_rows(base, g2_ref[...]).astype(BF16)
            h_ref[...] = h
        else:
            base = o_ref[...]
            h = h_ref[...]
        gate = jnp.dot(h, wg_ref[...], preferred_element_type=F32)
        up = jnp.dot(h, wu_ref[...], preferred_element_type=F32)
        a = (gate * jax.nn.sigmoid(gate) * up).astype(BF16)
        y = base + jnp.dot(a, wd_ref[...], preferred_element_type=F32)
        o_ref[...] = _rms_rows(y, gf_ref[...]) if (is_last and final_norm) else y

    _prefetched_row_tile(x_hbm, x_ref, x_sem, functools.partial(ffn_step, True, False))
    pl.when((f > 0) & (f < last))(functools.partial(ffn_step, False, False))
    pl.when(f == last)(functools.partial(ffn_step, False, True))


def _ffn(x1, g2, wg, wu, wd, gf, *, tm, tf, final_norm):
    s, d = x1.shape
    dff = wg.shape[1]
    assert dff // tf >= 2, "first and last inner steps are distinct code paths"
    return pl.pallas_call(
        functools.partial(_ffn_kernel, final_norm=final_norm),
        grid=(s // tm, dff // tf),
        in_specs=[
            pl.BlockSpec(memory_space=pl.ANY),
            pl.BlockSpec((1, d), lambda i, f: (0, 0)),
            pl.BlockSpec((d, tf), lambda i, f: (0, f)),
            pl.BlockSpec((d, tf), lambda i, f: (0, f)),
            pl.BlockSpec((tf, d), lambda i, f: (f, 0)),
            pl.BlockSpec((1, d), lambda i, f: (0, 0)),
        ],
        out_specs=pl.BlockSpec((tm, d), lambda i, f: (i, 0)),
        out_shape=jax.ShapeDtypeStruct((s, d), F32),
        scratch_shapes=[pltpu.VMEM((tm, d), BF16), pltpu.VMEM((tm, d), F32), pltpu.SemaphoreType.DMA(())],
        compiler_params=pltpu.CompilerParams(
            dimension_semantics=("arbitrary", "arbitrary"),
            vmem_limit_bytes=V7X_VMEM_LIMIT_BYTES),
        name="ffn",
    )(x1, g2, wg, wu, wd, gf)


def kernel(x, norm1_g, w_in, sgu_ln_g, sgu_ln_b, w_spatial, b_spatial, ret_gn_g, ret_gn_b,
           w_out, norm2_g, w_gate, w_up, w_down, final_norm_g):
    batch, seq, d = x.shape
    depth = w_in.shape[0]
    outs = []
    for b in range(batch):
        xb = x[b]
        for l in range(depth):
            proj = _in_proj(xb, norm1_g[l][None, :], w_in[l],
                            tm=IN_PROJ_ROWS, tn=IN_PROJ_COLS, n_split=IN_PROJ_DOTS_PER_STEP)
            x1, wg, wu, wd = _mixer_out_proj(proj, sgu_ln_g[l], sgu_ln_b[l], w_spatial[l], b_spatial[l],
                                             ret_gn_g[l], ret_gn_b[l], w_out[l], xb,
                                             w_gate[l], w_up[l], w_down[l],
                                             chunks_per_step=MIXER_CHUNKS_PER_STEP)
            xb = _ffn(x1, norm2_g[l][None, :], wg, wu, wd, final_norm_g[None, :],
                      tm=FFN_ROWS, tf=FFN_COLS, final_norm=(l == depth - 1))
        outs.append(xb)
    return outs[0][None] if batch == 1 else jnp.stack(outs)
```

```python
import functools

import numpy as np

import jax
import jax.numpy as jnp
from jax import lax
from jax.experimental import pallas as pl
from jax.experimental.pallas import tpu as pltpu

CHUNK = 128
HEADS = 8
HEAD_DIM = 128
SGU_WIDTH = HEADS * HEAD_DIM
RET_WIDTH = HEADS * HEAD_DIM
MIX_WIDTH = SGU_WIDTH + RET_WIDTH
IN_WIDTH = 2 * SGU_WIDTH + 4 * RET_WIDTH
ROPE_BASE = 10000.0
EPS = 1e-6

OFF_U, OFF_VS, OFF_Q, OFF_K, OFF_VR, OFF_G = (i * SGU_WIDTH for i in range(6))

V7X_VMEM_LIMIT_BYTES = 60 * 1024 * 1024

N_PROJ_PIECES = 8
W_OUT_STAGE_ROWS = 256

IN_PROJ_ROWS, IN_PROJ_COLS = 2048, 1024
MIXER_CHUNKS_PER_STEP = 2
FFN_ROWS, FFN_COLS, FFN_TILES_PER_STEP = 1024, 512, 2

F32 = jnp.float32
BF16 = jnp.bfloat16


def _rms_rows(x, g):
    ms = jnp.mean(x * x, axis=-1, keepdims=True)
    return x * lax.rsqrt(ms + EPS) * g


def _norm_rows(x, g, b):
    mu = jnp.mean(x, axis=-1, keepdims=True)
    d = x - mu
    var = jnp.mean(d * d, axis=-1, keepdims=True)
    return d * lax.rsqrt(var + EPS) * g + b


def _row_tile_copy(x_hbm, x_ref, sem, tile):
    tm = x_ref.shape[0]
    start = pl.multiple_of(tile * tm, tm)
    return pltpu.make_async_copy(x_hbm.at[pl.ds(start, tm), :], x_ref, sem)


def _prefetched_row_tile(x_hbm, x_ref, sem, consume):
    i, j = pl.program_id(0), pl.program_id(1)

    @pl.when((i == 0) & (j == 0))
    def _():
        _row_tile_copy(x_hbm, x_ref, sem, 0).start()

    @pl.when(j == 0)
    def _():
        _row_tile_copy(x_hbm, x_ref, sem, i).wait()
        consume()

    @pl.when((j == 1) & (i + 1 < pl.num_programs(0)))
    def _():
        _row_tile_copy(x_hbm, x_ref, sem, i + 1).start()


def _in_proj_kernel(x_hbm, g_ref, w_ref, o_ref, h_ref, x_ref, x_sem):
    def normalise():
        h_ref[...] = _rms_rows(x_ref[...], g_ref[...]).astype(BF16)

    _prefetched_row_tile(x_hbm, x_ref, x_sem, normalise)
    o_ref[...] = jnp.dot(h_ref[...], w_ref[...].astype(BF16), preferred_element_type=F32).astype(o_ref.dtype)


def _in_proj(x, g, w, *, tm, tn):
    s, d = x.shape
    n = w.shape[1]
    assert n // tn >= 2, "the row-tile prefetch starts at inner step 1"
    return pl.pallas_call(
        _in_proj_kernel,
        grid=(s // tm, n // tn),
        in_specs=[
            pl.BlockSpec(memory_space=pl.ANY),
            pl.BlockSpec((1, d), lambda i, j: (0, 0)),
            pl.BlockSpec((d, tn), lambda i, j: (0, j)),
        ],
        out_specs=pl.BlockSpec((tm, tn), lambda i, j: (i, j)),
        out_shape=jax.ShapeDtypeStruct((s, n), BF16),
        scratch_shapes=[pltpu.VMEM((tm, d), BF16), pltpu.VMEM((tm, d), F32), pltpu.SemaphoreType.DMA(())],
        compiler_params=pltpu.CompilerParams(
            dimension_semantics=("arbitrary", "arbitrary"),
            vmem_limit_bytes=V7X_VMEM_LIMIT_BYTES),
        name="in_proj",
    )(x, g, w)


def _mixer_kernel(p_ref, ca_ref, sa_ref, cb_ref, sb_ref, cbs_ref, sbs_ref,
                  lng_ref, lnb_ref, ws_ref, bs_ref, gng_ref, gnb_ref,
                  decay_ref, zeta_ref, xi_ref, wout_hbm, x_ref, wg_hbm, wu_hbm, wd_hbm,
                  x1_ref, wgb_ref, wub_ref, wdb_ref,
                  state_ref, wc_ref, bias_ref, woutb_ref, o_ref, prev_ref, stage_ref, stage_sem,
                  wg_stage, wu_stage, wd_stage, ffn_sem, *, chunks_per_step, gamma_c):
    step = pl.program_id(0)
    last_block = pl.num_programs(0) - 2
    ffn_weights = ((wg_hbm, wg_stage, wgb_ref), (wu_hbm, wu_stage, wub_ref), (wd_hbm, wd_stage, wdb_ref))

    def ffn_weight_copies(at_step):
        block = jnp.minimum(at_step, last_block)
        slot = at_step % 2
        copies = []
        for k, (w_hbm, stage, _) in enumerate(ffn_weights):
            rows = stage.shape[1]
            start = pl.multiple_of(block * rows, 8)
            copies.append(pltpu.make_async_copy(w_hbm.at[pl.ds(start, rows), :], stage.at[slot], ffn_sem.at[k, slot]))
        return copies

    row_id = lax.broadcasted_iota(jnp.int32, (CHUNK, CHUNK), 0)
    col_id = lax.broadcasted_iota(jnp.int32, (CHUNK, CHUNK), 1)

    @pl.when(step == 0)
    def _():
        for copy in ffn_weight_copies(0):
            copy.start()
        state_ref[...] = jnp.zeros_like(state_ref)
        o_ref[...] = jnp.zeros_like(o_ref)
        stage_rows = stage_ref.shape[1]
        n_stage = woutb_ref.shape[0] // stage_rows

        def stage_copy(r):
            return pltpu.make_async_copy(wout_hbm.at[pl.ds(r * stage_rows, stage_rows), :],
                                         stage_ref.at[r % 2], stage_sem.at[r % 2])

        stage_copy(0).start()
        for r in range(n_stage):
            if r + 1 < n_stage:
                stage_copy(r + 1).start()
            stage_copy(r).wait()
            woutb_ref[r * stage_rows:(r + 1) * stage_rows, :] = stage_ref[r % 2].astype(BF16)
        for h in range(HEADS):
            wc_ref[h] = jnp.where(row_id >= col_id, ws_ref[h], 0.0).astype(BF16)
            b_col = jnp.sum(jnp.where(row_id == col_id, bs_ref[h:h + 1, :], 0.0), axis=1, keepdims=True)
            bias_ref[h] = jnp.broadcast_to(b_col, (CHUNK, HEAD_DIM))

    @pl.when(step + 1 < pl.num_programs(0))
    def _():
        for copy in ffn_weight_copies(step + 1):
            copy.start()

    k_scale = HEAD_DIM ** -0.5
    chunk0 = jnp.minimum(step, last_block) * chunks_per_step
    heads = range(HEADS)

    prev_ref[...] = o_ref[...]
    piece_cols = x1_ref.shape[1] // N_PROJ_PIECES
    issued = []

    def project_piece():
        n = slice(len(issued) * piece_cols, (len(issued) + 1) * piece_cols)
        issued.append(n)
        x1_ref[:, n] = x_ref[:, n] + jnp.dot(prev_ref[...], woutb_ref[:, n], preferred_element_type=F32)

    def cols(off, h):
        return slice(off + h * HEAD_DIM, off + (h + 1) * HEAD_DIM)

    def chunk_rows(c):
        return slice(c * CHUNK, (c + 1) * CHUNK)

    def spatial_gate(c):
        rows = chunk_rows(c)
        vn = [_norm_rows(p_ref[rows, cols(OFF_VS, h)].astype(F32), lng_ref[h:h + 1, :], lnb_ref[h:h + 1, :])
              .astype(BF16) for h in heads]
        mixed = [jnp.dot(wc_ref[h], vn[h], preferred_element_type=F32) for h in heads]
        for h in heads:
            u = p_ref[rows, cols(OFF_U, h)].astype(F32)
            o_ref[rows, cols(0, h)] = (u * (mixed[h] + bias_ref[h])).astype(o_ref.dtype)

    def retention_scores(c):
        rows = chunk_rows(c)
        ca = ca_ref[pl.ds(chunk0 + c, 1), :]
        sa = sa_ref[pl.ds(chunk0 + c, 1), :]
        cos2 = ca * cb_ref[...] - sa * sb_ref[...]
        sin2 = sa * cbs_ref[...] + ca * sbs_ref[...]
        cos2k = cos2 * k_scale
        sin2k = sin2 * k_scale
        qb, kr = [], []
        for h in heads:
            q = p_ref[rows, cols(OFF_Q, h)].astype(F32)
            k = p_ref[rows, cols(OFF_K, h)].astype(F32)
            qb.append((q * cos2 + pltpu.roll(q, HEAD_DIM // 2, axis=1) * sin2).astype(BF16))
            kr.append(k * cos2k + pltpu.roll(k, HEAD_DIM // 2, axis=1) * sin2k)
        scores = [lax.dot_general(qb[h], kr[h].astype(BF16), (((1,), (1,)), ((), ())),
                                  preferred_element_type=F32) for h in heads]
        return qb, kr, scores

    def retention_values(c, qb, kr, scores):
        rows = chunk_rows(c)
        vr = [p_ref[rows, cols(OFF_VR, h)] for h in heads]
        scores_b = [(scores[h] * decay_ref[h]).astype(BF16) for h in heads]
        kz = [(kr[h] * zeta_ref[h]).astype(BF16) for h in heads]
        state = [state_ref[h] for h in heads]
        intra = [jnp.dot(scores_b[h], vr[h], preferred_element_type=F32) for h in heads]
        inter = [jnp.dot(qb[h], state[h].astype(BF16), preferred_element_type=F32) for h in heads]
        kv = [lax.dot_general(kz[h], vr[h], (((0,), (0,)), ((), ())), preferred_element_type=F32) for h in heads]
        return intra, inter, kv, state

    def retention_gate(c, intra, inter, kv, state):
        rows = chunk_rows(c)
        for h in heads:
            state_ref[h] = gamma_c[h] * state[h] + kv[h]
            on = _norm_rows(intra[h] + inter[h] * xi_ref[h], gng_ref[h:h + 1, :], gnb_ref[h:h + 1, :])
            g = p_ref[rows, cols(OFF_G, h)].astype(F32)
            o_ref[rows, cols(SGU_WIDTH, h)] = (g * jax.nn.sigmoid(g) * on).astype(o_ref.dtype)

    for c in range(chunks_per_step):
        project_piece()
        spatial_gate(c)
        project_piece()
        scored = retention_scores(c)
        project_piece()
        values = retention_values(c, *scored)
        project_piece()
        retention_gate(c, *values)
    assert len(issued) == N_PROJ_PIECES

    for copy, (_, stage, out_ref) in zip(ffn_weight_copies(step), ffn_weights):
        copy.wait()
        out_ref[...] = stage[step % 2].astype(BF16)


def _mixer_constants(seq):
    half = HEAD_DIM // 2
    inv = 1.0 / (ROPE_BASE ** (np.arange(half, dtype=np.float64) / half))
    inv2 = np.concatenate([inv, inv])
    sign = np.concatenate([-np.ones(half), np.ones(half)])
    ang_a = (CHUNK * np.arange(seq // CHUNK, dtype=np.float64))[:, None] * inv2[None, :]
    ang_b = np.arange(CHUNK, dtype=np.float64)[:, None] * inv2[None, :]
    rot = [np.cos(ang_a), np.sin(ang_a), np.cos(ang_b), np.sin(ang_b),
           sign * np.cos(ang_b), sign * np.sin(ang_b)]

    log_gamma = np.log(1.0 - np.exp2(-5.0 - np.arange(HEADS, dtype=np.float64)))
    idx = np.arange(CHUNK, dtype=np.float64)
    diff = idx[:, None] - idx[None, :]
    decay = np.where(diff[None] >= 0, np.exp(np.maximum(diff, 0.0)[None] * log_gamma[:, None, None]), 0.0)
    zeta = np.exp((CHUNK - 1.0 - idx)[None, :] * log_gamma[:, None])
    xi = np.exp((idx + 1.0)[None, :] * log_gamma[:, None])
    bcast = lambda v: np.broadcast_to(v[:, :, None], (HEADS, CHUNK, HEAD_DIM))
    gamma_c = tuple(float(np.float32(v)) for v in np.exp(CHUNK * log_gamma))
    tables = [jnp.asarray(np.ascontiguousarray(t), dtype=F32) for t in rot + [decay, bcast(zeta), bcast(xi)]]
    return tables, gamma_c


def _mixer_out_proj(proj, lng, lnb, ws, bs, gng, gnb, w_out, x, w_gate, w_up, w_down, *, chunks_per_step):
    s, d = x.shape
    dff = w_gate.shape[1]
    n_chunks = s // CHUNK
    rows = chunks_per_step * CHUNK
    n_tiles = s // rows
    gate_rows, down_rows = d // n_tiles, dff // n_tiles
    cast_block = lambda i: (jnp.minimum(i, n_tiles - 1), 0)
    (ca, sa, cb, sb, cbs, sbs, decay, zeta, xi), gamma_c = _mixer_constants(s)
    full = lambda shape, **kw: pl.BlockSpec(shape, lambda i: (0,) * len(shape), **kw)
    head_tiles = full((HEADS, CHUNK, HEAD_DIM))
    head_rows = full((HEADS, HEAD_DIM))
    mixed_tile = lambda i: (jnp.minimum(i, n_tiles - 1), 0)
    projected_tile = lambda i: (jnp.maximum(i - 1, 0), 0)
    return pl.pallas_call(
        functools.partial(_mixer_kernel, chunks_per_step=chunks_per_step, gamma_c=gamma_c),
        grid=(n_tiles + 1,),
        in_specs=[
            pl.BlockSpec((rows, IN_WIDTH), mixed_tile),
            full((n_chunks, HEAD_DIM)), full((n_chunks, HEAD_DIM)),
            full((CHUNK, HEAD_DIM)), full((CHUNK, HEAD_DIM)), full((CHUNK, HEAD_DIM)), full((CHUNK, HEAD_DIM)),
            head_rows, head_rows, head_tiles, head_rows, head_rows, head_rows,
            head_tiles, head_tiles, head_tiles,
            pl.BlockSpec(memory_space=pl.ANY),
            pl.BlockSpec((rows, d), projected_tile),
            pl.BlockSpec(memory_space=pl.ANY),
            pl.BlockSpec(memory_space=pl.ANY),
            pl.BlockSpec(memory_space=pl.ANY),
        ],
        out_specs=[pl.BlockSpec((rows, d), projected_tile),
                   pl.BlockSpec((gate_rows, dff), cast_block),
                   pl.BlockSpec((gate_rows, dff), cast_block),
                   pl.BlockSpec((down_rows, d), cast_block)],
        out_shape=[jax.ShapeDtypeStruct((s, d), F32),
                   jax.ShapeDtypeStruct(w_gate.shape, BF16),
                   jax.ShapeDtypeStruct(w_up.shape, BF16),
                   jax.ShapeDtypeStruct(w_down.shape, BF16)],
        scratch_shapes=[pltpu.VMEM((HEADS, HEAD_DIM, HEAD_DIM), F32),
                        pltpu.VMEM((HEADS, CHUNK, CHUNK), BF16),
                        pltpu.VMEM((HEADS, CHUNK, HEAD_DIM), F32),
                        pltpu.VMEM((MIX_WIDTH, d), BF16),
                        pltpu.VMEM((rows, MIX_WIDTH), BF16),
                        pltpu.VMEM((rows, MIX_WIDTH), BF16),
                        pltpu.VMEM((2, W_OUT_STAGE_ROWS, d), F32),
                        pltpu.SemaphoreType.DMA((2,)),
                        pltpu.VMEM((2, gate_rows, dff), F32),
                        pltpu.VMEM((2, gate_rows, dff), F32),
                        pltpu.VMEM((2, down_rows, d), F32),
                        pltpu.SemaphoreType.DMA((3, 2))],
        compiler_params=pltpu.CompilerParams(
            dimension_semantics=("arbitrary",),
            vmem_limit_bytes=V7X_VMEM_LIMIT_BYTES),
        name="mixer_out_proj",
    )(proj, ca, sa, cb, sb, cbs, sbs, lng, lnb, ws, bs, gng, gnb, decay, zeta, xi, w_out, x,
      w_gate, w_up, w_down)


def _ffn_kernel(x_hbm, g2_ref, wg_hbm, wu_hbm, wd_hbm, gf_ref, o_ref, h_ref, x_ref, x_sem,
                wg_buf, wu_buf, wd_buf, w_sem, *, n_tiles, tiles_per_step, final_norm):
    i, s = pl.program_id(0), pl.program_id(1)
    n_steps = pl.cdiv(n_tiles, tiles_per_step)
    last = n_steps - 1
    tf = wg_buf.shape[2]
    last_row_tile = pl.num_programs(0) - 1
    total_tiles = pl.num_programs(0) * n_tiles

    def weight_copies(local_tile, slot):
        off = pl.multiple_of(local_tile * tf, tf)
        return (pltpu.make_async_copy(wg_hbm.at[:, pl.ds(off, tf)], wg_buf.at[slot], w_sem.at[0, slot]),
                pltpu.make_async_copy(wu_hbm.at[:, pl.ds(off, tf)], wu_buf.at[slot], w_sem.at[1, slot]),
                pltpu.make_async_copy(wd_hbm.at[pl.ds(off, tf), :], wd_buf.at[slot], w_sem.at[2, slot]))

    @pl.when((i == 0) & (s == 0))
    def _():
        for copy in weight_copies(0, 0):
            copy.start()

    def ffn_step(is_first, is_last, tiles_here):
        for k in range(tiles_here):
            local_tile = s * tiles_per_step + k
            slot = (i * n_tiles + local_tile) & 1
            for copy in weight_copies(local_tile, slot):
                copy.wait()
            if is_last and k == tiles_here - 1:
                next_tile = jnp.where(i == last_row_tile, n_tiles - 1, 0)
            else:
                next_tile = local_tile + 1
            for copy in weight_copies(next_tile, 1 - slot):
                copy.start()
            if is_first and k == 0:
                base = x_ref[...]
                h = _rms_rows(base, g2_ref[...]).astype(BF16)
                h_ref[...] = h
            else:
                base = o_ref[...]
                h = h_ref[...]
            gate = jnp.dot(h, wg_buf[slot], preferred_element_type=F32)
            up = jnp.dot(h, wu_buf[slot], preferred_element_type=F32)
            a = (gate * jax.nn.sigmoid(gate) * up).astype(BF16)
            y = base + jnp.dot(a, wd_buf[slot], preferred_element_type=F32)
            finish = is_last and final_norm and k == tiles_here - 1
            o_ref[...] = _rms_rows(y, gf_ref[...]) if finish else y
        if is_last:
            @pl.when(i == last_row_tile)
            def _():
                for copy in weight_copies(n_tiles - 1, total_tiles & 1):
                    copy.wait()

    tiles_in_last = n_tiles - tiles_per_step * last
    _prefetched_row_tile(x_hbm, x_ref, x_sem, functools.partial(ffn_step, True, False, tiles_per_step))
    pl.when((s > 0) & (s < last))(functools.partial(ffn_step, False, False, tiles_per_step))
    pl.when(s == last)(functools.partial(ffn_step, False, True, tiles_in_last))


def _ffn(x1, g2, wg, wu, wd, gf, *, tm, tf, tiles_per_step, final_norm):
    s, d = x1.shape
    dff = wg.shape[1]
    n_tiles = dff // tf
    n_steps = pl.cdiv(n_tiles, tiles_per_step)
    assert n_steps >= 2, "first and last inner steps are distinct code paths"
    return pl.pallas_call(
        functools.partial(_ffn_kernel, n_tiles=n_tiles, tiles_per_step=tiles_per_step, final_norm=final_norm),
        grid=(s // tm, n_steps),
        in_specs=[
            pl.BlockSpec(memory_space=pl.ANY),
            pl.BlockSpec((1, d), lambda i, f: (0, 0)),
            pl.BlockSpec(memory_space=pl.ANY),
            pl.BlockSpec(memory_space=pl.ANY),
            pl.BlockSpec(memory_space=pl.ANY),
            pl.BlockSpec((1, d), lambda i, f: (0, 0)),
        ],
        out_specs=pl.BlockSpec((tm, d), lambda i, f: (i, 0)),
        out_shape=jax.ShapeDtypeStruct((s, d), F32),
        scratch_shapes=[pltpu.VMEM((tm, d), BF16), pltpu.VMEM((tm, d), F32), pltpu.SemaphoreType.DMA(()),
                        pltpu.VMEM((2, d, tf), BF16), pltpu.VMEM((2, d, tf), BF16), pltpu.VMEM((2, tf, d), BF16),
                        pltpu.SemaphoreType.DMA((3, 2))],
        compiler_params=pltpu.CompilerParams(
            dimension_semantics=("arbitrary", "arbitrary"),
            vmem_limit_bytes=V7X_VMEM_LIMIT_BYTES),
        name="ffn",
    )(x1, g2, wg, wu, wd, gf)


def kernel(x, norm1_g, w_in, sgu_ln_g, sgu_ln_b, w_spatial, b_spatial, ret_gn_g, ret_gn_b,
           w_out, norm2_g, w_gate, w_up, w_down, final_norm_g):
    batch, seq, d = x.shape
    depth = w_in.shape[0]
    outs = []
    for b in range(batch):
        xb = x[b]
        for l in range(depth):
            proj = _in_proj(xb, norm1_g[l][None, :], w_in[l], tm=IN_PROJ_ROWS, tn=IN_PROJ_COLS)
            x1, wg, wu, wd = _mixer_out_proj(proj, sgu_ln_g[l], sgu_ln_b[l], w_spatial[l], b_spatial[l],
                                             ret_gn_g[l], ret_gn_b[l], w_out[l], xb,
                                             w_gate[l], w_up[l], w_down[l],
                                             chunks_per_step=MIXER_CHUNKS_PER_STEP)
            xb = _ffn(x1, norm2_g[l][None, :], wg, wu, wd, final_norm_g[None, :],
                      tm=FFN_ROWS, tf=FFN_COLS, tiles_per_step=FFN_TILES_PER_STEP, final_norm=(l == depth - 1))
        outs.append(xb)
    return outs[0][None] if batch == 1 else jnp.stack(outs)
```

```python
import functools

import numpy as np

import jax
import jax.numpy as jnp
from jax import lax
from jax.experimental import pallas as pl
from jax.experimental.pallas import tpu as pltpu

CHUNK = 128
HEADS = 8
HEAD_DIM = 128
SGU_WIDTH = HEADS * HEAD_DIM
RET_WIDTH = HEADS * HEAD_DIM
MIX_WIDTH = SGU_WIDTH + RET_WIDTH
IN_WIDTH = 2 * SGU_WIDTH + 4 * RET_WIDTH
ROPE_BASE = 10000.0
EPS = 1e-6

OFF_U, OFF_VS, OFF_Q, OFF_K, OFF_VR, OFF_G = (i * SGU_WIDTH for i in range(6))

V7X_VMEM_LIMIT_BYTES = 60 * 1024 * 1024

N_PROJ_PIECES = 8
W_OUT_STAGE_ROWS = 256

IN_PROJ_ROWS, IN_PROJ_COLS, IN_PROJ_DOTS_PER_STEP = 2048, 1024, 2
MIXER_CHUNKS_PER_STEP = 2
FFN_ROWS, FFN_COLS = 1024, 512

F32 = jnp.float32
BF16 = jnp.bfloat16


def _rms_rows(x, g):
    ms = jnp.mean(x * x, axis=-1, keepdims=True)
    return x * lax.rsqrt(ms + EPS) * g


def _norm_rows(x, g, b):
    mu = jnp.mean(x, axis=-1, keepdims=True)
    d = x - mu
    var = jnp.mean(d * d, axis=-1, keepdims=True)
    return d * lax.rsqrt(var + EPS) * g + b


def _row_tile_copy(x_hbm, x_ref, sem, tile):
    tm = x_ref.shape[0]
    start = pl.multiple_of(tile * tm, tm)
    return pltpu.make_async_copy(x_hbm.at[pl.ds(start, tm), :], x_ref, sem)


def _prefetched_row_tile(x_hbm, x_ref, sem, consume):
    i, j = pl.program_id(0), pl.program_id(1)

    @pl.when((i == 0) & (j == 0))
    def _():
        _row_tile_copy(x_hbm, x_ref, sem, 0).start()

    @pl.when(j == 0)
    def _():
        _row_tile_copy(x_hbm, x_ref, sem, i).wait()
        consume()

    @pl.when((j == 1) & (i + 1 < pl.num_programs(0)))
    def _():
        _row_tile_copy(x_hbm, x_ref, sem, i + 1).start()


def _in_proj_kernel(x_hbm, g_ref, w_ref, o_ref, h_ref, x_ref, x_sem, *, n_split):
    cols = o_ref.shape[1] // n_split

    def project(is_first):
        if is_first:
            h = _rms_rows(x_ref[...], g_ref[...]).astype(BF16)
            h_ref[...] = h
        else:
            h = h_ref[...]
        for s in range(n_split):
            n = slice(s * cols, (s + 1) * cols)
            o_ref[:, n] = jnp.dot(h, w_ref[:, n].astype(BF16), preferred_element_type=F32).astype(o_ref.dtype)

    _prefetched_row_tile(x_hbm, x_ref, x_sem, functools.partial(project, True))
    pl.when(pl.program_id(1) > 0)(functools.partial(project, False))


def _in_proj(x, g, w, *, tm, tn, n_split):
    s, d = x.shape
    n = w.shape[1]
    assert n // tn >= 2, "the row-tile prefetch starts at inner step 1"
    return pl.pallas_call(
        functools.partial(_in_proj_kernel, n_split=n_split),
        grid=(s // tm, n // tn),
        in_specs=[
            pl.BlockSpec(memory_space=pl.ANY),
            pl.BlockSpec((1, d), lambda i, j: (0, 0)),
            pl.BlockSpec((d, tn), lambda i, j: (0, j)),
        ],
        out_specs=pl.BlockSpec((tm, tn), lambda i, j: (i, j)),
        out_shape=jax.ShapeDtypeStruct((s, n), BF16),
        scratch_shapes=[pltpu.VMEM((tm, d), BF16), pltpu.VMEM((tm, d), F32), pltpu.SemaphoreType.DMA(())],
        compiler_params=pltpu.CompilerParams(
            dimension_semantics=("arbitrary", "arbitrary"),
            vmem_limit_bytes=63 * 1024 * 1024 + 512 * 1024),
        name="in_proj",
    )(x, g, w)


def _mixer_kernel(p_ref, ca_ref, sa_ref, cb_ref, sb_ref, cbs_ref, sbs_ref,
                  lng_ref, lnb_ref, ws_ref, bs_ref, gng_ref, gnb_ref,
                  decay_ref, zeta_ref, xi_ref, wout_hbm, x_ref, wg_hbm, wu_hbm, wd_hbm,
                  x1_ref, wgb_ref, wub_ref, wdb_ref,
                  state_ref, wc_ref, bias_ref, woutb_ref, o_ref, prev_ref, stage_ref, stage_sem,
                  wg_stage, wu_stage, wd_stage, ffn_sem, *, chunks_per_step, gamma_c):
    step = pl.program_id(0)
    last_block = pl.num_programs(0) - 2
    ffn_weights = ((wg_hbm, wg_stage, wgb_ref), (wu_hbm, wu_stage, wub_ref), (wd_hbm, wd_stage, wdb_ref))

    def ffn_weight_copies(at_step):
        block = jnp.minimum(at_step, last_block)
        slot = at_step % 2
        copies = []
        for k, (w_hbm, stage, _) in enumerate(ffn_weights):
            rows = stage.shape[1]
            start = pl.multiple_of(block * rows, 8)
            copies.append(pltpu.make_async_copy(w_hbm.at[pl.ds(start, rows), :], stage.at[slot], ffn_sem.at[k, slot]))
        return copies

    row_id = lax.broadcasted_iota(jnp.int32, (CHUNK, CHUNK), 0)
    col_id = lax.broadcasted_iota(jnp.int32, (CHUNK, CHUNK), 1)

    @pl.when(step == 0)
    def _():
        for copy in ffn_weight_copies(0):
            copy.start()
        state_ref[...] = jnp.zeros_like(state_ref)
        o_ref[...] = jnp.zeros_like(o_ref)
        stage_rows = stage_ref.shape[1]
        n_stage = woutb_ref.shape[0] // stage_rows

        def stage_copy(r):
            return pltpu.make_async_copy(wout_hbm.at[pl.ds(r * stage_rows, stage_rows), :],
                                         stage_ref.at[r % 2], stage_sem.at[r % 2])

        stage_copy(0).start()
        for r in range(n_stage):
            if r + 1 < n_stage:
                stage_copy(r + 1).start()
            stage_copy(r).wait()
            woutb_ref[r * stage_rows:(r + 1) * stage_rows, :] = stage_ref[r % 2].astype(BF16)
        for h in range(HEADS):
            wc_ref[h] = jnp.where(row_id >= col_id, ws_ref[h], 0.0).astype(BF16)
            b_col = jnp.sum(jnp.where(row_id == col_id, bs_ref[h:h + 1, :], 0.0), axis=1, keepdims=True)
            bias_ref[h] = jnp.broadcast_to(b_col, (CHUNK, HEAD_DIM))

    @pl.when(step + 1 < pl.num_programs(0))
    def _():
        for copy in ffn_weight_copies(step + 1):
            copy.start()

    k_scale = HEAD_DIM ** -0.5
    chunk0 = jnp.minimum(step, last_block) * chunks_per_step
    heads = range(HEADS)

    prev_ref[...] = o_ref[...]
    piece_cols = x1_ref.shape[1] // N_PROJ_PIECES
    issued = []

    def project_piece():
        n = slice(len(issued) * piece_cols, (len(issued) + 1) * piece_cols)
        issued.append(n)
        x1_ref[:, n] = x_ref[:, n] + jnp.dot(prev_ref[...], woutb_ref[:, n], preferred_element_type=F32)

    def cols(off, h):
        return slice(off + h * HEAD_DIM, off + (h + 1) * HEAD_DIM)

    def chunk_rows(c):
        return slice(c * CHUNK, (c + 1) * CHUNK)

    def spatial_gate(c):
        rows = chunk_rows(c)
        vn = [_norm_rows(p_ref[rows, cols(OFF_VS, h)].astype(F32), lng_ref[h:h + 1, :], lnb_ref[h:h + 1, :])
              .astype(BF16) for h in heads]
        mixed = [jnp.dot(wc_ref[h], vn[h], preferred_element_type=F32) for h in heads]
        for h in heads:
            u = p_ref[rows, cols(OFF_U, h)].astype(F32)
            o_ref[rows, cols(0, h)] = (u * (mixed[h] + bias_ref[h])).astype(o_ref.dtype)

    def retention_scores(c):
        rows = chunk_rows(c)
        ca = ca_ref[pl.ds(chunk0 + c, 1), :]
        sa = sa_ref[pl.ds(chunk0 + c, 1), :]
        cos2 = ca * cb_ref[...] - sa * sb_ref[...]
        sin2 = sa * cbs_ref[...] + ca * sbs_ref[...]
        cos2k = cos2 * k_scale
        sin2k = sin2 * k_scale
        qb, kr = [], []
        for h in heads:
            q = p_ref[rows, cols(OFF_Q, h)].astype(F32)
            k = p_ref[rows, cols(OFF_K, h)].astype(F32)
            qb.append((q * cos2 + pltpu.roll(q, HEAD_DIM // 2, axis=1) * sin2).astype(BF16))
            kr.append(k * cos2k + pltpu.roll(k, HEAD_DIM // 2, axis=1) * sin2k)
        scores = [lax.dot_general(qb[h], kr[h].astype(BF16), (((1,), (1,)), ((), ())),
                                  preferred_element_type=F32) for h in heads]
        return qb, kr, scores

    def retention_values(c, qb, kr, scores):
        rows = chunk_rows(c)
        vr = [p_ref[rows, cols(OFF_VR, h)] for h in heads]
        scores_b = [(scores[h] * decay_ref[h]).astype(BF16) for h in heads]
        kz = [(kr[h] * zeta_ref[h]).astype(BF16) for h in heads]
        state = [state_ref[h] for h in heads]
        intra = [jnp.dot(scores_b[h], vr[h], preferred_element_type=F32) for h in heads]
        inter = [jnp.dot(qb[h], state[h].astype(BF16), preferred_element_type=F32) for h in heads]
        kv = [lax.dot_general(kz[h], vr[h], (((0,), (0,)), ((), ())), preferred_element_type=F32) for h in heads]
        return intra, inter, kv, state

    def retention_gate(c, intra, inter, kv, state):
        rows = chunk_rows(c)
        for h in heads:
            state_ref[h] = gamma_c[h] * state[h] + kv[h]
            on = _norm_rows(intra[h] + inter[h] * xi_ref[h], gng_ref[h:h + 1, :], gnb_ref[h:h + 1, :])
            g = p_ref[rows, cols(OFF_G, h)].astype(F32)
            o_ref[rows, cols(SGU_WIDTH, h)] = (g * jax.nn.sigmoid(g) * on).astype(o_ref.dtype)

    for c in range(chunks_per_step):
        project_piece()
        spatial_gate(c)
        project_piece()
        scored = retention_scores(c)
        project_piece()
        values = retention_values(c, *scored)
        project_piece()
        retention_gate(c, *values)
    assert len(issued) == N_PROJ_PIECES

    for copy, (_, stage, out_ref) in zip(ffn_weight_copies(step), ffn_weights):
        copy.wait()
        out_ref[...] = stage[step % 2].astype(BF16)


def _mixer_constants(seq):
    half = HEAD_DIM // 2
    inv = 1.0 / (ROPE_BASE ** (np.arange(half, dtype=np.float64) / half))
    inv2 = np.concatenate([inv, inv])
    sign = np.concatenate([-np.ones(half), np.ones(half)])
    ang_a = (CHUNK * np.arange(seq // CHUNK, dtype=np.float64))[:, None] * inv2[None, :]
    ang_b = np.arange(CHUNK, dtype=np.float64)[:, None] * inv2[None, :]
    rot = [np.cos(ang_a), np.sin(ang_a), np.cos(ang_b), np.sin(ang_b),
           sign * np.cos(ang_b), sign * np.sin(ang_b)]

    log_gamma = np.log(1.0 - np.exp2(-5.0 - np.arange(HEADS, dtype=np.float64)))
    idx = np.arange(CHUNK, dtype=np.float64)
    diff = idx[:, None] - idx[None, :]
    decay = np.where(diff[None] >= 0, np.exp(np.maximum(diff, 0.0)[None] * log_gamma[:, None, None]), 0.0)
    zeta = np.exp((CHUNK - 1.0 - idx)[None, :] * log_gamma[:, None])
    xi = np.exp((idx + 1.0)[None, :] * log_gamma[:, None])
    bcast = lambda v: np.broadcast_to(v[:, :, None], (HEADS, CHUNK, HEAD_DIM))
    gamma_c = tuple(float(np.float32(v)) for v in np.exp(CHUNK * log_gamma))
    tables = [jnp.asarray(np.ascontiguousarray(t), dtype=F32) for t in rot + [decay, bcast(zeta), bcast(xi)]]
    return tables, gamma_c


def _mixer_out_proj(proj, lng, lnb, ws, bs, gng, gnb, w_out, x, w_gate, w_up, w_down, *, chunks_per_step):
    s, d = x.shape
    dff = w_gate.shape[1]
    n_chunks = s // CHUNK
    rows = chunks_per_step * CHUNK
    n_tiles = s // rows
    gate_rows, down_rows = d // n_tiles, dff // n_tiles
    cast_block = lambda i: (jnp.minimum(i, n_tiles - 1), 0)
    (ca, sa, cb, sb, cbs, sbs, decay, zeta, xi), gamma_c = _mixer_constants(s)
    full = lambda shape, **kw: pl.BlockSpec(shape, lambda i: (0,) * len(shape), **kw)
    head_tiles = full((HEADS, CHUNK, HEAD_DIM))
    head_rows = full((HEADS, HEAD_DIM))
    mixed_tile = lambda i: (jnp.minimum(i, n_tiles - 1), 0)
    projected_tile = lambda i: (jnp.maximum(i - 1, 0), 0)
    return pl.pallas_call(
        functools.partial(_mixer_kernel, chunks_per_step=chunks_per_step, gamma_c=gamma_c),
        grid=(n_tiles + 1,),
        in_specs=[
            pl.BlockSpec((rows, IN_WIDTH), mixed_tile),
            full((n_chunks, HEAD_DIM)), full((n_chunks, HEAD_DIM)),
            full((CHUNK, HEAD_DIM)), full((CHUNK, HEAD_DIM)), full((CHUNK, HEAD_DIM)), full((CHUNK, HEAD_DIM)),
            head_rows, head_rows, head_tiles, head_rows, head_rows, head_rows,
            head_tiles, head_tiles, head_tiles,
            pl.BlockSpec(memory_space=pl.ANY),
            pl.BlockSpec((rows, d), projected_tile),
            pl.BlockSpec(memory_space=pl.ANY),
            pl.BlockSpec(memory_space=pl.ANY),
            pl.BlockSpec(memory_space=pl.ANY),
        ],
        out_specs=[pl.BlockSpec((rows, d), projected_tile),
                   pl.BlockSpec((gate_rows, dff), cast_block),
                   pl.BlockSpec((gate_rows, dff), cast_block),
                   pl.BlockSpec((down_rows, d), cast_block)],
        out_shape=[jax.ShapeDtypeStruct((s, d), F32),
                   jax.ShapeDtypeStruct(w_gate.shape, BF16),
                   jax.ShapeDtypeStruct(w_up.shape, BF16),
                   jax.ShapeDtypeStruct(w_down.shape, BF16)],
        scratch_shapes=[pltpu.VMEM((HEADS, HEAD_DIM, HEAD_DIM), F32),
                        pltpu.VMEM((HEADS, CHUNK, CHUNK), BF16),
                        pltpu.VMEM((HEADS, CHUNK, HEAD_DIM), F32),
                        pltpu.VMEM((MIX_WIDTH, d), BF16),
                        pltpu.VMEM((rows, MIX_WIDTH), BF16),
                        pltpu.VMEM((rows, MIX_WIDTH), BF16),
                        pltpu.VMEM((2, W_OUT_STAGE_ROWS, d), F32),
                        pltpu.SemaphoreType.DMA((2,)),
                        pltpu.VMEM((2, gate_rows, dff), F32),
                        pltpu.VMEM((2, gate_rows, dff), F32),
                        pltpu.VMEM((2, down_rows, d), F32),
                        pltpu.SemaphoreType.DMA((3, 2))],
        compiler_params=pltpu.CompilerParams(
            dimension_semantics=("arbitrary",),
            vmem_limit_bytes=V7X_VMEM_LIMIT_BYTES),
        name="mixer_out_proj",
    )(proj, ca, sa, cb, sb, cbs, sbs, lng, lnb, ws, bs, gng, gnb, decay, zeta, xi, w_out, x,
      w_gate, w_up, w_down)


def _ffn_kernel(x_hbm, g2_ref, wg_ref, wu_ref, wd_ref, gf_ref, o_ref, h_ref, x_ref, x_sem, *, final_norm):
    f = pl.program_id(1)
    last = pl.num_programs(1) - 1

    def ffn_step(is_first, is_last):
        if is_first:
            base = x_ref[...]
            h = _rms_rows(base, g2_ref[...]).astype(BF16)
            h_ref[...] = h
        else:
            base = o_ref[...]
            h = h_ref[...]
        gate = jnp.dot(h, wg_ref[...], preferred_element_type=F32)
        up = jnp.dot(h, wu_ref[...], preferred_element_type=F32)
        a = (gate * jax.nn.sigmoid(gate) * up).astype(BF16)
        y = base + jnp.dot(a, wd_ref[...], preferred_element_type=F32)
        o_ref[...] = _rms_rows(y, gf_ref[...]) if (is_last and final_norm) else y

    _prefetched_row_tile(x_hbm, x_ref, x_sem, functools.partial(ffn_step, True, False))
    pl.when((f > 0) & (f < last))(functools.partial(ffn_step, False, False))
    pl.when(f == last)(functools.partial(ffn_step, False, True))


def _ffn(x1, g2, wg, wu, wd, gf, *, tm, tf, final_norm):
    s, d = x1.shape
    dff = wg.shape[1]
    assert dff // tf >= 2, "first and last inner steps are distinct code paths"
    return pl.pallas_call(
        functools.partial(_ffn_kernel, final_norm=final_norm),
        grid=(s // tm, dff // tf),
        in_specs=[
            pl.BlockSpec(memory_space=pl.ANY),
            pl.BlockSpec((1, d), lambda i, f: (0, 0)),
            pl.BlockSpec((d, tf), lambda i, f: (0, f)),
            pl.BlockSpec((d, tf), lambda i, f: (0, f)),
            pl.BlockSpec((tf, d), lambda i, f: (f, 0)),
            pl.BlockSpec((1, d), lambda i, f: (0, 0)),
        ],
        out_specs=pl.BlockSpec((tm, d), lambda i, f: (i, 0)),
        out_shape=jax.ShapeDtypeStruct((s, d), F32),
        scratch_shapes=[pltpu.VMEM((tm, d), BF16), pltpu.VMEM((tm, d), F32), pltpu.SemaphoreType.DMA(())],
        compiler_params=pltpu.CompilerParams(
            dimension_semantics=("arbitrary", "arbitrary"),
            vmem_limit_bytes=V7X_VMEM_LIMIT_BYTES),
        name="ffn",
    )(x1, g2, wg, wu, wd, gf)


def kernel(x, norm1_g, w_in, sgu_ln_g, sgu_ln_b, w_spatial, b_spatial, ret_gn_g, ret_gn_b,
           w_out, norm2_g, w_gate, w_up, w_down, final_norm_g):
    batch, seq, d = x.shape
    depth = w_in.shape[0]
    outs = []
    for b in range(batch):
        xb = x[b]
        for l in range(depth):
            proj = _in_proj(xb, norm1_g[l][None, :], w_in[l],
                            tm=IN_PROJ_ROWS, tn=IN_PROJ_COLS, n_split=IN_PROJ_DOTS_PER_STEP)
            x1, wg, wu, wd = _mixer_out_proj(proj, sgu_ln_g[l], sgu_ln_b[l], w_spatial[l], b_spatial[l],
                                             ret_gn_g[l], ret_gn_b[l], w_out[l], xb,
                                             w_gate[l], w_up[l], w_down[l],
                                             chunks_per_step=MIXER_CHUNKS_PER_STEP)
            xb = _ffn(x1, norm2_g[l][None, :], wg, wu, wd, final_norm_g[None, :],
                      tm=FFN_ROWS, tf=FFN_COLS, final_norm=(l == depth - 1))
        outs.append(xb)
    return outs[0][None] if batch == 1 else jnp.stack(outs)
```

```python
import functools

import numpy as np

import jax
import jax.numpy as jnp
from jax import lax
from jax.experimental import pallas as pl
from jax.experimental.pallas import tpu as pltpu

CHUNK = 128
HEADS = 8
HEAD_DIM = 128
SGU_WIDTH = HEADS * HEAD_DIM
RET_WIDTH = HEADS * HEAD_DIM
MIX_WIDTH = SGU_WIDTH + RET_WIDTH
IN_WIDTH = 2 * SGU_WIDTH + 4 * RET_WIDTH
ROPE_BASE = 10000.0
EPS = 1e-6

OFF_U, OFF_VS, OFF_Q, OFF_K, OFF_VR, OFF_G = (i * SGU_WIDTH for i in range(6))

V7X_VMEM_LIMIT_BYTES = 60 * 1024 * 1024

N_PROJ_PIECES = 8
W_OUT_STAGE_ROWS = 256

IN_PROJ_ROWS, IN_PROJ_COLS, IN_PROJ_DOTS_PER_STEP = 2048, 1024, 2
MIXER_CHUNKS_PER_STEP = 2
FFN_ROWS, FFN_COLS = 1024, 512

F32 = jnp.float32
BF16 = jnp.bfloat16


def _rms_rows(x, g):
    ms = jnp.mean(x * x, axis=-1, keepdims=True)
    return x * lax.rsqrt(ms + EPS) * g


def _norm_rows(x, g, b):
    mu = jnp.mean(x, axis=-1, keepdims=True)
    d = x - mu
    var = jnp.mean(d * d, axis=-1, keepdims=True)
    return d * lax.rsqrt(var + EPS) * g + b


def _row_tile_copy(x_hbm, x_ref, sem, tile):
    tm = x_ref.shape[0]
    start = pl.multiple_of(tile * tm, tm)
    return pltpu.make_async_copy(x_hbm.at[pl.ds(start, tm), :], x_ref, sem)


def _prefetched_row_tile(x_hbm, x_ref, sem, consume):
    i, j = pl.program_id(0), pl.program_id(1)

    @pl.when((i == 0) & (j == 0))
    def _():
        _row_tile_copy(x_hbm, x_ref, sem, 0).start()

    @pl.when(j == 0)
    def _():
        _row_tile_copy(x_hbm, x_ref, sem, i).wait()
        consume()

    @pl.when((j == 1) & (i + 1 < pl.num_programs(0)))
    def _():
        _row_tile_copy(x_hbm, x_ref, sem, i + 1).start()


def _in_proj_kernel(x_hbm, g_ref, w_ref, o_ref, h_ref, x_ref, x_sem, *, n_split):
    def normalise():
        h_ref[...] = _rms_rows(x_ref[...], g_ref[...]).astype(BF16)

    _prefetched_row_tile(x_hbm, x_ref, x_sem, normalise)
    cols = o_ref.shape[1] // n_split
    for s in range(n_split):
        n = slice(s * cols, (s + 1) * cols)
        o_ref[:, n] = jnp.dot(h_ref[...], w_ref[:, n].astype(BF16),
                              preferred_element_type=F32).astype(o_ref.dtype)


def _in_proj(x, g, w, *, tm, tn, n_split):
    s, d = x.shape
    n = w.shape[1]
    assert n // tn >= 2, "the row-tile prefetch starts at inner step 1"
    return pl.pallas_call(
        functools.partial(_in_proj_kernel, n_split=n_split),
        grid=(s // tm, n // tn),
        in_specs=[
            pl.BlockSpec(memory_space=pl.ANY),
            pl.BlockSpec((1, d), lambda i, j: (0, 0)),
            pl.BlockSpec((d, tn), lambda i, j: (0, j)),
        ],
        out_specs=pl.BlockSpec((tm, tn), lambda i, j: (i, j)),
        out_shape=jax.ShapeDtypeStruct((s, n), BF16),
        scratch_shapes=[pltpu.VMEM((tm, d), BF16), pltpu.VMEM((tm, d), F32), pltpu.SemaphoreType.DMA(())],
        compiler_params=pltpu.CompilerParams(
            dimension_semantics=("arbitrary", "arbitrary"),
            vmem_limit_bytes=V7X_VMEM_LIMIT_BYTES),
        name="in_proj",
    )(x, g, w)


def _mixer_kernel(p_ref, ca_ref, sa_ref, cb_ref, sb_ref, cbs_ref, sbs_ref,
                  lng_ref, lnb_ref, ws_ref, bs_ref, gng_ref, gnb_ref,
                  decay_ref, zeta_ref, xi_ref, wout_hbm, x_ref, wg_hbm, wu_hbm, wd_hbm,
                  x1_ref, wgb_ref, wub_ref, wdb_ref,
                  state_ref, wc_ref, bias_ref, woutb_ref, o_ref, prev_ref, stage_ref, stage_sem,
                  wg_stage, wu_stage, wd_stage, ffn_sem, *, chunks_per_step, gamma_c):
    step = pl.program_id(0)
    last_block = pl.num_programs(0) - 2
    ffn_weights = ((wg_hbm, wg_stage, wgb_ref), (wu_hbm, wu_stage, wub_ref), (wd_hbm, wd_stage, wdb_ref))

    def ffn_weight_copies(at_step):
        block = jnp.minimum(at_step, last_block)
        slot = at_step % 2
        copies = []
        for k, (w_hbm, stage, _) in enumerate(ffn_weights):
            rows = stage.shape[1]
            start = pl.multiple_of(block * rows, 8)
            copies.append(pltpu.make_async_copy(w_hbm.at[pl.ds(start, rows), :], stage.at[slot], ffn_sem.at[k, slot]))
        return copies

    row_id = lax.broadcasted_iota(jnp.int32, (CHUNK, CHUNK), 0)
    col_id = lax.broadcasted_iota(jnp.int32, (CHUNK, CHUNK), 1)

    @pl.when(step == 0)
    def _():
        for copy in ffn_weight_copies(0):
            copy.start()
        state_ref[...] = jnp.zeros_like(state_ref)
        o_ref[...] = jnp.zeros_like(o_ref)
        stage_rows = stage_ref.shape[1]
        n_stage = woutb_ref.shape[0] // stage_rows

        def stage_copy(r):
            return pltpu.make_async_copy(wout_hbm.at[pl.ds(r * stage_rows, stage_rows), :],
                                         stage_ref.at[r % 2], stage_sem.at[r % 2])

        stage_copy(0).start()
        for r in range(n_stage):
            if r + 1 < n_stage:
                stage_copy(r + 1).start()
            stage_copy(r).wait()
            woutb_ref[r * stage_rows:(r + 1) * stage_rows, :] = stage_ref[r % 2].astype(BF16)
        for h in range(HEADS):
            wc_ref[h] = jnp.where(row_id >= col_id, ws_ref[h], 0.0).astype(BF16)
            b_col = jnp.sum(jnp.where(row_id == col_id, bs_ref[h:h + 1, :], 0.0), axis=1, keepdims=True)
            bias_ref[h] = jnp.broadcast_to(b_col, (CHUNK, HEAD_DIM))

    @pl.when(step + 1 < pl.num_programs(0))
    def _():
        for copy in ffn_weight_copies(step + 1):
            copy.start()

    k_scale = HEAD_DIM ** -0.5
    chunk0 = jnp.minimum(step, last_block) * chunks_per_step
    heads = range(HEADS)

    prev_ref[...] = o_ref[...]
    piece_cols = x1_ref.shape[1] // N_PROJ_PIECES
    issued = []

    def project_piece():
        n = slice(len(issued) * piece_cols, (len(issued) + 1) * piece_cols)
        issued.append(n)
        x1_ref[:, n] = x_ref[:, n] + jnp.dot(prev_ref[...], woutb_ref[:, n], preferred_element_type=F32)

    def cols(off, h):
        return slice(off + h * HEAD_DIM, off + (h + 1) * HEAD_DIM)

    def chunk_rows(c):
        return slice(c * CHUNK, (c + 1) * CHUNK)

    def spatial_gate(c):
        rows = chunk_rows(c)
        vn = [_norm_rows(p_ref[rows, cols(OFF_VS, h)].astype(F32), lng_ref[h:h + 1, :], lnb_ref[h:h + 1, :])
              .astype(BF16) for h in heads]
        mixed = [jnp.dot(wc_ref[h], vn[h], preferred_element_type=F32) for h in heads]
        for h in heads:
            u = p_ref[rows, cols(OFF_U, h)].astype(F32)
            o_ref[rows, cols(0, h)] = (u * (mixed[h] + bias_ref[h])).astype(o_ref.dtype)

    def retention_scores(c):
        rows = chunk_rows(c)
        ca = ca_ref[pl.ds(chunk0 + c, 1), :]
        sa = sa_ref[pl.ds(chunk0 + c, 1), :]
        cos2 = ca * cb_ref[...] - sa * sb_ref[...]
        sin2 = sa * cbs_ref[...] + ca * sbs_ref[...]
        cos2k = cos2 * k_scale
        sin2k = sin2 * k_scale
        qb, kr = [], []
        for h in heads:
            q = p_ref[rows, cols(OFF_Q, h)].astype(F32)
            k = p_ref[rows, cols(OFF_K, h)].astype(F32)
            qb.append((q * cos2 + pltpu.roll(q, HEAD_DIM // 2, axis=1) * sin2).astype(BF16))
            kr.append(k * cos2k + pltpu.roll(k, HEAD_DIM // 2, axis=1) * sin2k)
        scores = [lax.dot_general(qb[h], kr[h].astype(BF16), (((1,), (1,)), ((), ())),
                                  preferred_element_type=F32) for h in heads]
        return qb, kr, scores

    def retention_values(c, qb, kr, scores):
        rows = chunk_rows(c)
        vr = [p_ref[rows, cols(OFF_VR, h)] for h in heads]
        scores_b = [(scores[h] * decay_ref[h]).astype(BF16) for h in heads]
        kz = [(kr[h] * zeta_ref[h]).astype(BF16) for h in heads]
        state = [state_ref[h] for h in heads]
        intra = [jnp.dot(scores_b[h], vr[h], preferred_element_type=F32) for h in heads]
        inter = [jnp.dot(qb[h], state[h].astype(BF16), preferred_element_type=F32) for h in heads]
        kv = [lax.dot_general(kz[h], vr[h], (((0,), (0,)), ((), ())), preferred_element_type=F32) for h in heads]
        return intra, inter, kv, state

    def retention_gate(c, intra, inter, kv, state):
        rows = chunk_rows(c)
        for h in heads:
            state_ref[h] = gamma_c[h] * state[h] + kv[h]
            on = _norm_rows(intra[h] + inter[h] * xi_ref[h], gng_ref[h:h + 1, :], gnb_ref[h:h + 1, :])
            g = p_ref[rows, cols(OFF_G, h)].astype(F32)
            o_ref[rows, cols(SGU_WIDTH, h)] = (g * jax.nn.sigmoid(g) * on).astype(o_ref.dtype)

    for c in range(chunks_per_step):
        project_piece()
        spatial_gate(c)
        project_piece()
        scored = retention_scores(c)
        project_piece()
        values = retention_values(c, *scored)
        project_piece()
        retention_gate(c, *values)
    assert len(issued) == N_PROJ_PIECES

    for copy, (_, stage, out_ref) in zip(ffn_weight_copies(step), ffn_weights):
        copy.wait()
        out_ref[...] = stage[step % 2].astype(BF16)


def _mixer_constants(seq):
    half = HEAD_DIM // 2
    inv = 1.0 / (ROPE_BASE ** (np.arange(half, dtype=np.float64) / half))
    inv2 = np.concatenate([inv, inv])
    sign = np.concatenate([-np.ones(half), np.ones(half)])
    ang_a = (CHUNK * np.arange(seq // CHUNK, dtype=np.float64))[:, None] * inv2[None, :]
    ang_b = np.arange(CHUNK, dtype=np.float64)[:, None] * inv2[None, :]
    rot = [np.cos(ang_a), np.sin(ang_a), np.cos(ang_b), np.sin(ang_b),
           sign * np.cos(ang_b), sign * np.sin(ang_b)]

    log_gamma = np.log(1.0 - np.exp2(-5.0 - np.arange(HEADS, dtype=np.float64)))
    idx = np.arange(CHUNK, dtype=np.float64)
    diff = idx[:, None] - idx[None, :]
    decay = np.where(diff[None] >= 0, np.exp(np.maximum(diff, 0.0)[None] * log_gamma[:, None, None]), 0.0)
    zeta = np.exp((CHUNK - 1.0 - idx)[None, :] * log_gamma[:, None])
    xi = np.exp((idx + 1.0)[None, :] * log_gamma[:, None])
    bcast = lambda v: np.broadcast_to(v[:, :, None], (HEADS, CHUNK, HEAD_DIM))
    gamma_c = tuple(float(np.float32(v)) for v in np.exp(CHUNK * log_gamma))
    tables = [jnp.asarray(np.ascontiguousarray(t), dtype=F32) for t in rot + [decay, bcast(zeta), bcast(xi)]]
    return tables, gamma_c


def _mixer_out_proj(proj, lng, lnb, ws, bs, gng, gnb, w_out, x, w_gate, w_up, w_down, *, chunks_per_step):
    s, d = x.shape
    dff = w_gate.shape[1]
    n_chunks = s // CHUNK
    rows = chunks_per_step * CHUNK
    n_tiles = s // rows
    gate_rows, down_rows = d // n_tiles, dff // n_tiles
    cast_block = lambda i: (jnp.minimum(i, n_tiles - 1), 0)
    (ca, sa, cb, sb, cbs, sbs, decay, zeta, xi), gamma_c = _mixer_constants(s)
    full = lambda shape, **kw: pl.BlockSpec(shape, lambda i: (0,) * len(shape), **kw)
    head_tiles = full((HEADS, CHUNK, HEAD_DIM))
    head_rows = full((HEADS, HEAD_DIM))
    mixed_tile = lambda i: (jnp.minimum(i, n_tiles - 1), 0)
    projected_tile = lambda i: (jnp.maximum(i - 1, 0), 0)
    n_in, n_out = 21, 4
    streamed = list(range(16)) + [17]
    window_specs = [
        pl.BlockSpec((rows, IN_WIDTH), mixed_tile),
        full((n_chunks, HEAD_DIM)), full((n_chunks, HEAD_DIM)),
        full((CHUNK, HEAD_DIM)), full((CHUNK, HEAD_DIM)), full((CHUNK, HEAD_DIM)), full((CHUNK, HEAD_DIM)),
        head_rows, head_rows, head_tiles, head_rows, head_rows, head_rows,
        head_tiles, head_tiles, head_tiles,
        pl.BlockSpec((rows, d), projected_tile),
    ]
    out_window_specs = [pl.BlockSpec((rows, d), projected_tile),
                        pl.BlockSpec((gate_rows, dff), cast_block),
                        pl.BlockSpec((gate_rows, dff), cast_block),
                        pl.BlockSpec((down_rows, d), cast_block)]

    def whole_call(*refs):
        ins, outs, scratch = refs[:n_in], refs[n_in:n_in + n_out], refs[n_in + n_out:]

        def step(*blocks):
            in_blocks, out_blocks = blocks[:len(streamed)], blocks[len(streamed):]
            args = list(ins)
            for k, block in zip(streamed, in_blocks):
                args[k] = block
            _mixer_kernel(*args, *out_blocks, *scratch, chunks_per_step=chunks_per_step, gamma_c=gamma_c)

        pltpu.emit_pipeline(step, grid=(n_tiles + 1,), in_specs=window_specs, out_specs=out_window_specs)(
            *[ins[k] for k in streamed], *outs)

    return pl.pallas_call(
        whole_call,
        in_specs=[pl.BlockSpec(memory_space=pl.ANY)] * n_in,
        out_specs=[pl.BlockSpec(memory_space=pl.ANY)] * n_out,
        out_shape=[jax.ShapeDtypeStruct((s, d), F32),
                   jax.ShapeDtypeStruct(w_gate.shape, BF16),
                   jax.ShapeDtypeStruct(w_up.shape, BF16),
                   jax.ShapeDtypeStruct(w_down.shape, BF16)],
        scratch_shapes=[pltpu.VMEM((HEADS, HEAD_DIM, HEAD_DIM), F32),
                        pltpu.VMEM((HEADS, CHUNK, CHUNK), BF16),
                        pltpu.VMEM((HEADS, CHUNK, HEAD_DIM), F32),
                        pltpu.VMEM((MIX_WIDTH, d), BF16),
                        pltpu.VMEM((rows, MIX_WIDTH), BF16),
                        pltpu.VMEM((rows, MIX_WIDTH), BF16),
                        pltpu.VMEM((2, W_OUT_STAGE_ROWS, d), F32),
                        pltpu.SemaphoreType.DMA((2,)),
                        pltpu.VMEM((2, gate_rows, dff), F32),
                        pltpu.VMEM((2, gate_rows, dff), F32),
                        pltpu.VMEM((2, down_rows, d), F32),
                        pltpu.SemaphoreType.DMA((3, 2))],
        compiler_params=pltpu.CompilerParams(vmem_limit_bytes=V7X_VMEM_LIMIT_BYTES),
        name="mixer_out_proj",
    )(proj, ca, sa, cb, sb, cbs, sbs, lng, lnb, ws, bs, gng, gnb, decay, zeta, xi, w_out, x,
      w_gate, w_up, w_down)


def _ffn_kernel(x_hbm, g2_ref, wg_ref, wu_ref, wd_ref, gf_ref, o_ref, h_ref, x_ref, x_sem, *, final_norm):
    f = pl.program_id(1)
    last = pl.num_programs(1) - 1

    def ffn_step(is_first, is_last):
        if is_first:
            base = x_ref[...]
            h = _rms_rows(base, g2_ref[...]).astype(BF16)
            h_ref[...] = h
        else:
            base = o_ref[...]
            h = h_ref[...]
        gate = jnp.dot(h, wg_ref[...], preferred_element_type=F32)
        up = jnp.dot(h, wu_ref[...], preferred_element_type=F32)
        a = (gate * jax.nn.sigmoid(gate) * up).astype(BF16)
        y = base + jnp.dot(a, wd_ref[...], preferred_element_type=F32)
        o_ref[...] = _rms_rows(y, gf_ref[...]) if (is_last and final_norm) else y

    _prefetched_row_tile(x_hbm, x_ref, x_sem, functools.partial(ffn_step, True, False))
    pl.when((f > 0) & (f < last))(functools.partial(ffn_step, False, False))
    pl.when(f == last)(functools.partial(ffn_step, False, True))


def _ffn(x1, g2, wg, wu, wd, gf, *, tm, tf, final_norm):
    s, d = x1.shape
    dff = wg.shape[1]
    assert dff // tf >= 2, "first and last inner steps are distinct code paths"
    return pl.pallas_call(
        functools.partial(_ffn_kernel, final_norm=final_norm),
        grid=(s // tm, dff // tf),
        in_specs=[
            pl.BlockSpec(memory_space=pl.ANY),
            pl.BlockSpec((1, d), lambda i, f: (0, 0)),
            pl.BlockSpec((d, tf), lambda i, f: (0, f)),
            pl.BlockSpec((d, tf), lambda i, f: (0, f)),
            pl.BlockSpec((tf, d), lambda i, f: (f, 0)),
            pl.BlockSpec((1, d), lambda i, f: (0, 0)),
        ],
        out_specs=pl.BlockSpec((tm, d), lambda i, f: (i, 0)),
        out_shape=jax.ShapeDtypeStruct((s, d), F32),
        scratch_shapes=[pltpu.VMEM((tm, d), BF16), pltpu.VMEM((tm, d), F32), pltpu.SemaphoreType.DMA(())],
        compiler_params=pltpu.CompilerParams(
            dimension_semantics=("arbitrary", "arbitrary"),
            vmem_limit_bytes=V7X_VMEM_LIMIT_BYTES),
        name="ffn",
    )(x1, g2, wg, wu, wd, gf)


def kernel(x, norm1_g, w_in, sgu_ln_g, sgu_ln_b, w_spatial, b_spatial, ret_gn_g, ret_gn_b,
           w_out, norm2_g, w_gate, w_up, w_down, final_norm_g):
    batch, seq, d = x.shape
    depth = w_in.shape[0]
    outs = []
    for b in range(batch):
        xb = x[b]
        for l in range(depth):
            proj = _in_proj(xb, norm1_g[l][None, :], w_in[l],
                            tm=IN_PROJ_ROWS, tn=IN_PROJ_COLS, n_split=IN_PROJ_DOTS_PER_STEP)
            x1, wg, wu, wd = _mixer_out_proj(proj, sgu_ln_g[l], sgu_ln_b[l], w_spatial[l], b_spatial[l],
                                             ret_gn_g[l], ret_gn_b[l], w_out[l], xb,
                                             w_gate[l], w_up[l], w_down[l],
                                             chunks_per_step=MIXER_CHUNKS_PER_STEP)
            xb = _ffn(x1, norm2_g[l][None, :], wg, wu, wd, final_norm_g[None, :],
                      tm=FFN_ROWS, tf=FFN_COLS, final_norm=(l == depth - 1))
        outs.append(xb)
    return outs[0][None] if batch == 1 else jnp.stack(outs)
```

```python
import functools

import numpy as np

import jax
import jax.numpy as jnp
from jax import lax
from jax.experimental import pallas as pl
from jax.experimental.pallas import tpu as pltpu

CHUNK = 128
HEADS = 8
HEAD_DIM = 128
SGU_WIDTH = HEADS * HEAD_DIM
RET_WIDTH = HEADS * HEAD_DIM
MIX_WIDTH = SGU_WIDTH + RET_WIDTH
IN_WIDTH = 2 * SGU_WIDTH + 4 * RET_WIDTH
ROPE_BASE = 10000.0
EPS = 1e-6

OFF_U, OFF_VS, OFF_Q, OFF_K, OFF_VR, OFF_G = (i * SGU_WIDTH for i in range(6))

V7X_VMEM_LIMIT_BYTES = 60 * 1024 * 1024

N_PROJ_PIECES = 8
W_OUT_STAGE_ROWS = 256

IN_PROJ_ROWS, IN_PROJ_COLS, IN_PROJ_DOTS_PER_STEP = 2048, 1024, 2
MIXER_CHUNKS_PER_STEP = 2
FFN_ROWS, FFN_COLS = 1024, 512

F32 = jnp.float32
BF16 = jnp.bfloat16


def _rms_rows(x, g):
    ms = jnp.mean(x * x, axis=-1, keepdims=True)
    return x * lax.rsqrt(ms + EPS) * g


def _norm_rows(x, g, b):
    mu = jnp.mean(x, axis=-1, keepdims=True)
    d = x - mu
    var = jnp.mean(d * d, axis=-1, keepdims=True)
    return d * lax.rsqrt(var + EPS) * g + b


def _row_tile_copy(x_hbm, x_ref, sem, tile):
    tm = x_ref.shape[0]
    start = pl.multiple_of(tile * tm, tm)
    return pltpu.make_async_copy(x_hbm.at[pl.ds(start, tm), :], x_ref, sem)


def _prefetched_row_tile(x_hbm, x_ref, sem, consume):
    i, j = pl.program_id(0), pl.program_id(1)

    @pl.when((i == 0) & (j == 0))
    def _():
        _row_tile_copy(x_hbm, x_ref, sem, 0).start()

    @pl.when(j == 0)
    def _():
        _row_tile_copy(x_hbm, x_ref, sem, i).wait()
        consume()

    @pl.when((j == 1) & (i + 1 < pl.num_programs(0)))
    def _():
        _row_tile_copy(x_hbm, x_ref, sem, i + 1).start()


def _in_proj_kernel(x_hbm, g_ref, w_ref, o_ref, h_ref, x_ref, x_sem, *, n_split):
    def normalise():
        h_ref[...] = _rms_rows(x_ref[...], g_ref[...]).astype(BF16)

    _prefetched_row_tile(x_hbm, x_ref, x_sem, normalise)
    cols = o_ref.shape[1] // n_split
    for s in range(n_split):
        n = slice(s * cols, (s + 1) * cols)
        o_ref[:, n] = jnp.dot(h_ref[...], w_ref[:, n].astype(BF16),
                              preferred_element_type=F32).astype(o_ref.dtype)


def _in_proj(x, g, w, *, tm, tn, n_split):
    s, d = x.shape
    n = w.shape[1]
    assert n // tn >= 2, "the row-tile prefetch starts at inner step 1"
    return pl.pallas_call(
        functools.partial(_in_proj_kernel, n_split=n_split),
        grid=(s // tm, n // tn),
        in_specs=[
            pl.BlockSpec(memory_space=pl.ANY),
            pl.BlockSpec((1, d), lambda i, j: (0, 0)),
            pl.BlockSpec((d, tn), lambda i, j: (0, j)),
        ],
        out_specs=pl.BlockSpec((tm, tn), lambda i, j: (i, j)),
        out_shape=jax.ShapeDtypeStruct((s, n), BF16),
        scratch_shapes=[pltpu.VMEM((tm, d), BF16), pltpu.VMEM((tm, d), F32), pltpu.SemaphoreType.DMA(())],
        compiler_params=pltpu.CompilerParams(
            dimension_semantics=("arbitrary", "arbitrary"),
            vmem_limit_bytes=V7X_VMEM_LIMIT_BYTES),
        name="in_proj",
    )(x, g, w)


def _mixer_kernel(p_ref, ca_ref, sa_ref, cb_ref, sb_ref, cbs_ref, sbs_ref,
                  lng_ref, lnb_ref, ws_ref, bs_ref, gng_ref, gnb_ref,
                  decay_ref, zeta_ref, xi_ref, wout_hbm, x_ref, wg_hbm, wu_hbm, wd_hbm,
                  x1_ref, wgb_ref, wub_ref, wdb_ref,
                  state_ref, wc_ref, bias_ref, woutb_ref, o_ref, prev_ref, stage_ref, stage_sem,
                  wg_stage, wu_stage, wd_stage, ffn_sem, *, chunks_per_step, gamma_c):
    step = pl.program_id(0)
    last_block = pl.num_programs(0) - 2
    ffn_weights = ((wg_hbm, wg_stage, wgb_ref), (wu_hbm, wu_stage, wub_ref), (wd_hbm, wd_stage, wdb_ref))

    def ffn_weight_copies(at_step):
        block = jnp.minimum(at_step, last_block)
        slot = at_step % 2
        copies = []
        for k, (w_hbm, stage, _) in enumerate(ffn_weights):
            rows = stage.shape[1]
            start = pl.multiple_of(block * rows, 8)
            copies.append(pltpu.make_async_copy(w_hbm.at[pl.ds(start, rows), :], stage.at[slot], ffn_sem.at[k, slot]))
        return copies

    row_id = lax.broadcasted_iota(jnp.int32, (CHUNK, CHUNK), 0)
    col_id = lax.broadcasted_iota(jnp.int32, (CHUNK, CHUNK), 1)

    @pl.when(step == 0)
    def _():
        for copy in ffn_weight_copies(0):
            copy.start()
        state_ref[...] = jnp.zeros_like(state_ref)
        o_ref[...] = jnp.zeros_like(o_ref)
        stage_rows = stage_ref.shape[1]
        n_stage = woutb_ref.shape[0] // stage_rows

        def stage_copy(r):
            return pltpu.make_async_copy(wout_hbm.at[pl.ds(r * stage_rows, stage_rows), :],
                                         stage_ref.at[r % 2], stage_sem.at[r % 2])

        stage_copy(0).start()
        for r in range(n_stage):
            if r + 1 < n_stage:
                stage_copy(r + 1).start()
            stage_copy(r).wait()
            woutb_ref[r * stage_rows:(r + 1) * stage_rows, :] = stage_ref[r % 2].astype(BF16)
        for h in range(HEADS):
            wc_ref[h] = jnp.where(row_id >= col_id, ws_ref[h], 0.0).astype(BF16)
            b_col = jnp.sum(jnp.where(row_id == col_id, bs_ref[h:h + 1, :], 0.0), axis=1, keepdims=True)
            bias_ref[h] = jnp.broadcast_to(b_col, (CHUNK, HEAD_DIM))

    @pl.when(step + 1 < pl.num_programs(0))
    def _():
        for copy in ffn_weight_copies(step + 1):
            copy.start()

    k_scale = HEAD_DIM ** -0.5
    chunk0 = jnp.minimum(step, last_block) * chunks_per_step
    heads = range(HEADS)

    prev_ref[...] = o_ref[...]
    piece_cols = x1_ref.shape[1] // N_PROJ_PIECES
    issued = []

    def project_piece():
        n = slice(len(issued) * piece_cols, (len(issued) + 1) * piece_cols)
        issued.append(n)
        x1_ref[:, n] = x_ref[:, n] + jnp.dot(prev_ref[...], woutb_ref[:, n], preferred_element_type=F32)

    def cols(off, h):
        return slice(off + h * HEAD_DIM, off + (h + 1) * HEAD_DIM)

    def chunk_rows(c):
        return slice(c * CHUNK, (c + 1) * CHUNK)

    def spatial_gate(c):
        rows = chunk_rows(c)
        vn = [_norm_rows(p_ref[rows, cols(OFF_VS, h)].astype(F32), lng_ref[h:h + 1, :], lnb_ref[h:h + 1, :])
              .astype(BF16) for h in heads]
        mixed = [jnp.dot(wc_ref[h], vn[h], preferred_element_type=F32) for h in heads]
        for h in heads:
            u = p_ref[rows, cols(OFF_U, h)].astype(F32)
            o_ref[rows, cols(0, h)] = (u * (mixed[h] + bias_ref[h])).astype(o_ref.dtype)

    def retention_scores(c):
        rows = chunk_rows(c)
        ca = ca_ref[pl.ds(chunk0 + c, 1), :]
        sa = sa_ref[pl.ds(chunk0 + c, 1), :]
        cos2 = ca * cb_ref[...] - sa * sb_ref[...]
        sin2 = sa * cbs_ref[...] + ca * sbs_ref[...]
        cos2k = cos2 * k_scale
        sin2k = sin2 * k_scale
        qb, kr = [], []
        for h in heads:
            q = p_ref[rows, cols(OFF_Q, h)].astype(F32)
            k = p_ref[rows, cols(OFF_K, h)].astype(F32)
            qb.append((q * cos2 + pltpu.roll(q, HEAD_DIM // 2, axis=1) * sin2).astype(BF16))
            kr.append(k * cos2k + pltpu.roll(k, HEAD_DIM // 2, axis=1) * sin2k)
        scores = [lax.dot_general(qb[h], kr[h].astype(BF16), (((1,), (1,)), ((), ())),
                                  preferred_element_type=F32) for h in heads]
        return qb, kr, scores

    def retention_values(c, qb, kr, scores):
        rows = chunk_rows(c)
        vr = [p_ref[rows, cols(OFF_VR, h)] for h in heads]
        scores_b = [(scores[h] * decay_ref[h]).astype(BF16) for h in heads]
        kz = [(kr[h] * zeta_ref[h]).astype(BF16) for h in heads]
        state = [state_ref[h] for h in heads]
        intra = [jnp.dot(scores_b[h], vr[h], preferred_element_type=F32) for h in heads]
        inter = [jnp.dot(qb[h], state[h].astype(BF16), preferred_element_type=F32) for h in heads]
        kv = [lax.dot_general(kz[h], vr[h], (((0,), (0,)), ((), ())), preferred_element_type=F32) for h in heads]
        return intra, inter, kv, state

    def retention_gate(c, intra, inter, kv, state):
        rows = chunk_rows(c)
        for h in heads:
            state_ref[h] = gamma_c[h] * state[h] + kv[h]
            on = _norm_rows(intra[h] + inter[h] * xi_ref[h], gng_ref[h:h + 1, :], gnb_ref[h:h + 1, :])
            g = p_ref[rows, cols(OFF_G, h)].astype(F32)
            o_ref[rows, cols(SGU_WIDTH, h)] = (g * jax.nn.sigmoid(g) * on).astype(o_ref.dtype)

    for c in range(chunks_per_step):
        project_piece()
        spatial_gate(c)
        project_piece()
        scored = retention_scores(c)
        project_piece()
        values = retention_values(c, *scored)
        project_piece()
        retention_gate(c, *values)
    assert len(issued) == N_PROJ_PIECES

    for copy, (_, stage, out_ref) in zip(ffn_weight_copies(step), ffn_weights):
        copy.wait()
        out_ref[...] = stage[step % 2].astype(BF16)


def _mixer_constants(seq):
    half = HEAD_DIM // 2
    inv = 1.0 / (ROPE_BASE ** (np.arange(half, dtype=np.float64) / half))
    inv2 = np.concatenate([inv, inv])
    sign = np.concatenate([-np.ones(half), np.ones(half)])
    ang_a = (CHUNK * np.arange(seq // CHUNK, dtype=np.float64))[:, None] * inv2[None, :]
    ang_b = np.arange(CHUNK, dtype=np.float64)[:, None] * inv2[None, :]
    rot = [np.cos(ang_a), np.sin(ang_a), np.cos(ang_b), np.sin(ang_b),
           sign * np.cos(ang_b), sign * np.sin(ang_b)]

    log_gamma = np.log(1.0 - np.exp2(-5.0 - np.arange(HEADS, dtype=np.float64)))
    idx = np.arange(CHUNK, dtype=np.float64)
    diff = idx[:, None] - idx[None, :]
    decay = np.where(diff[None] >= 0, np.exp(np.maximum(diff, 0.0)[None] * log_gamma[:, None, None]), 0.0)
    zeta = np.exp((CHUNK - 1.0 - idx)[None, :] * log_gamma[:, None])
    xi = np.exp((idx + 1.0)[None, :] * log_gamma[:, None])
    bcast = lambda v: np.broadcast_to(v[:, :, None], (HEADS, CHUNK, HEAD_DIM))
    gamma_c = tuple(float(np.float32(v)) for v in np.exp(CHUNK * log_gamma))
    tables = [jnp.asarray(np.ascontiguousarray(t), dtype=F32) for t in rot + [decay, bcast(zeta), bcast(xi)]]
    return tables, gamma_c


def _mixer_out_proj(proj, lng, lnb, ws, bs, gng, gnb, w_out, x, w_gate, w_up, w_down, *, chunks_per_step):
    s, d = x.shape
    dff = w_gate.shape[1]
    n_chunks = s // CHUNK
    rows = chunks_per_step * CHUNK
    n_tiles = s // rows
    gate_rows, down_rows = d // n_tiles, dff // n_tiles
    cast_block = lambda i: (jnp.minimum(i, n_tiles - 1), 0)
    (ca, sa, cb, sb, cbs, sbs, decay, zeta, xi), gamma_c = _mixer_constants(s)
    full = lambda shape, **kw: pl.BlockSpec(shape, lambda i: (0,) * len(shape), **kw)
    head_tiles = full((HEADS, CHUNK, HEAD_DIM))
    head_rows = full((HEADS, HEAD_DIM))
    mixed_tile = lambda i: (jnp.minimum(i, n_tiles - 1), 0)
    projected_tile = lambda i: (jnp.maximum(i - 1, 0), 0)
    return pl.pallas_call(
        functools.partial(_mixer_kernel, chunks_per_step=chunks_per_step, gamma_c=gamma_c),
        grid=(n_tiles + 1,),
        in_specs=[
            pl.BlockSpec((rows, IN_WIDTH), mixed_tile),
            full((n_chunks, HEAD_DIM)), full((n_chunks, HEAD_DIM)),
            full((CHUNK, HEAD_DIM)), full((CHUNK, HEAD_DIM)), full((CHUNK, HEAD_DIM)), full((CHUNK, HEAD_DIM)),
            head_rows, head_rows, head_tiles, head_rows, head_rows, head_rows,
            head_tiles, head_tiles, head_tiles,
            pl.BlockSpec(memory_space=pl.ANY),
            pl.BlockSpec((rows, d), projected_tile),
            pl.BlockSpec(memory_space=pl.ANY),
            pl.BlockSpec(memory_space=pl.ANY),
            pl.BlockSpec(memory_space=pl.ANY),
        ],
        out_specs=[pl.BlockSpec((rows, d), projected_tile),
                   pl.BlockSpec((gate_rows, dff), cast_block),
                   pl.BlockSpec((gate_rows, dff), cast_block),
                   pl.BlockSpec((down_rows, d), cast_block)],
        out_shape=[jax.ShapeDtypeStruct((s, d), F32),
                   jax.ShapeDtypeStruct(w_gate.shape, BF16),
                   jax.ShapeDtypeStruct(w_up.shape, BF16),
                   jax.ShapeDtypeStruct(w_down.shape, BF16)],
        scratch_shapes=[pltpu.VMEM((HEADS, HEAD_DIM, HEAD_DIM), F32),
                        pltpu.VMEM((HEADS, CHUNK, CHUNK), BF16),
                        pltpu.VMEM((HEADS, CHUNK, HEAD_DIM), F32),
                        pltpu.VMEM((MIX_WIDTH, d), BF16),
                        pltpu.VMEM((rows, MIX_WIDTH), BF16),
                        pltpu.VMEM((rows, MIX_WIDTH), BF16),
                        pltpu.VMEM((2, W_OUT_STAGE_ROWS, d), F32),
                        pltpu.SemaphoreType.DMA((2,)),
                        pltpu.VMEM((2, gate_rows, dff), F32),
                        pltpu.VMEM((2, gate_rows, dff), F32),
                        pltpu.VMEM((2, down_rows, d), F32),
                        pltpu.SemaphoreType.DMA((3, 2))],
        compiler_params=pltpu.CompilerParams(
            dimension_semantics=("arbitrary",),
            vmem_limit_bytes=V7X_VMEM_LIMIT_BYTES),
        name="mixer_out_proj",
    )(proj, ca, sa, cb, sb, cbs, sbs, lng, lnb, ws, bs, gng, gnb, decay, zeta, xi, w_out, x,
      w_gate, w_up, w_down)


def _ffn_kernel(x_hbm, g2_ref, wg_ref, wu_ref, wd_ref, gf_ref, o_ref, h_ref, x_ref, x_sem, a_ref, *,
                n_tiles, final_norm):
    f = pl.program_id(1)
    last = n_tiles

    def activations(h, slot):
        gate = jnp.dot(h, wg_ref[...], preferred_element_type=F32)
        up = jnp.dot(h, wu_ref[...], preferred_element_type=F32)
        a_ref[slot] = (gate * jax.nn.sigmoid(gate) * up).astype(BF16)

    def first_step():
        base = x_ref[...]
        h = _rms_rows(base, g2_ref[...]).astype(BF16)
        h_ref[...] = h
        o_ref[...] = base
        activations(h, 0)

    def middle_step(slot):
        activations(h_ref[...], slot)
        o_ref[...] += jnp.dot(a_ref[1 - slot], wd_ref[...], preferred_element_type=F32)

    def last_step():
        y = o_ref[...] + jnp.dot(a_ref[(last - 1) % 2], wd_ref[...], preferred_element_type=F32)
        o_ref[...] = _rms_rows(y, gf_ref[...]) if final_norm else y

    _prefetched_row_tile(x_hbm, x_ref, x_sem, first_step)
    for slot in range(2):
        pl.when((f > 0) & (f < last) & (f % 2 == slot))(functools.partial(middle_step, slot))
    pl.when(f == last)(last_step)


def _ffn(x1, g2, wg, wu, wd, gf, *, tm, tf, final_norm):
    s, d = x1.shape
    dff = wg.shape[1]
    n_tiles = dff // tf
    return pl.pallas_call(
        functools.partial(_ffn_kernel, n_tiles=n_tiles, final_norm=final_norm),
        grid=(s // tm, n_tiles + 1),
        in_specs=[
            pl.BlockSpec(memory_space=pl.ANY),
            pl.BlockSpec((1, d), lambda i, f: (0, 0)),
            pl.BlockSpec((d, tf), lambda i, f: (0, jnp.minimum(f, n_tiles - 1))),
            pl.BlockSpec((d, tf), lambda i, f: (0, jnp.minimum(f, n_tiles - 1))),
            pl.BlockSpec((tf, d), lambda i, f: (jnp.maximum(f - 1, 0), 0)),
            pl.BlockSpec((1, d), lambda i, f: (0, 0)),
        ],
        out_specs=pl.BlockSpec((tm, d), lambda i, f: (i, 0)),
        out_shape=jax.ShapeDtypeStruct((s, d), F32),
        scratch_shapes=[pltpu.VMEM((tm, d), BF16), pltpu.VMEM((tm, d), F32), pltpu.SemaphoreType.DMA(()),
                        pltpu.VMEM((2, tm, tf), BF16)],
        compiler_params=pltpu.CompilerParams(
            dimension_semantics=("arbitrary", "arbitrary"),
            vmem_limit_bytes=V7X_VMEM_LIMIT_BYTES),
        name="ffn",
    )(x1, g2, wg, wu, wd, gf)


def kernel(x, norm1_g, w_in, sgu_ln_g, sgu_ln_b, w_spatial, b_spatial, ret_gn_g, ret_gn_b,
           w_out, norm2_g, w_gate, w_up, w_down, final_norm_g):
    batch, seq, d = x.shape
    depth = w_in.shape[0]
    outs = []
    for b in range(batch):
        xb = x[b]
        for l in range(depth):
            proj = _in_proj(xb, norm1_g[l][None, :], w_in[l],
                            tm=IN_PROJ_ROWS, tn=IN_PROJ_COLS, n_split=IN_PROJ_DOTS_PER_STEP)
            x1, wg, wu, wd = _mixer_out_proj(proj, sgu_ln_g[l], sgu_ln_b[l], w_spatial[l], b_spatial[l],
                                             ret_gn_g[l], ret_gn_b[l], w_out[l], xb,
                                             w_gate[l], w_up[l], w_down[l],
                                             chunks_per_step=MIXER_CHUNKS_PER_STEP)
            xb = _ffn(x1, norm2_g[l][None, :], wg, wu, wd, final_norm_g[None, :],
                      tm=FFN_ROWS, tf=FFN_COLS, final_norm=(l == depth - 1))
        outs.append(xb)
    return outs[0][None] if batch == 1 else jnp.stack(outs)
```

```python
import functools

import numpy as np

import jax
import jax.numpy as jnp
from jax import lax
from jax.experimental import pallas as pl
from jax.experimental.pallas import tpu as pltpu

CHUNK = 128
HEADS = 8
HEAD_DIM = 128
SGU_WIDTH = HEADS * HEAD_DIM
RET_WIDTH = HEADS * HEAD_DIM
MIX_WIDTH = SGU_WIDTH + RET_WIDTH
IN_WIDTH = 2 * SGU_WIDTH + 4 * RET_WIDTH
ROPE_BASE = 10000.0
EPS = 1e-6

OFF_U, OFF_VS, OFF_Q, OFF_K, OFF_VR, OFF_G = (i * SGU_WIDTH for i in range(6))

V7X_VMEM_LIMIT_BYTES = 60 * 1024 * 1024
FFN_VMEM_LIMIT_BYTES = 127 * 512 * 1024

N_PROJ_PIECES = 8
W_OUT_STAGE_ROWS = 256

IN_PROJ_ROWS, IN_PROJ_COLS, IN_PROJ_DOTS_PER_STEP = 2048, 1024, 2
MIXER_CHUNKS_PER_STEP = 2
FFN_ROWS, FFN_COLS = 1024, 512

F32 = jnp.float32
BF16 = jnp.bfloat16


def _rms_rows(x, g):
    ms = jnp.mean(x * x, axis=-1, keepdims=True)
    return x * lax.rsqrt(ms + EPS) * g


def _norm_rows(x, g, b):
    mu = jnp.mean(x, axis=-1, keepdims=True)
    d = x - mu
    var = jnp.mean(d * d, axis=-1, keepdims=True)
    return d * lax.rsqrt(var + EPS) * g + b


def _row_tile_copy(x_hbm, x_ref, sem, tile):
    tm = x_ref.shape[0]
    start = pl.multiple_of(tile * tm, tm)
    return pltpu.make_async_copy(x_hbm.at[pl.ds(start, tm), :], x_ref, sem)


def _prefetched_row_tile(x_hbm, x_ref, sem, consume):
    i, j = pl.program_id(0), pl.program_id(1)

    @pl.when((i == 0) & (j == 0))
    def _():
        _row_tile_copy(x_hbm, x_ref, sem, 0).start()

    @pl.when(j == 0)
    def _():
        _row_tile_copy(x_hbm, x_ref, sem, i).wait()
        consume()

    @pl.when((j == 1) & (i + 1 < pl.num_programs(0)))
    def _():
        _row_tile_copy(x_hbm, x_ref, sem, i + 1).start()


def _in_proj_kernel(x_hbm, g_ref, w_ref, o_ref, h_ref, x_ref, x_sem, *, n_split):
    def normalise():
        h_ref[...] = _rms_rows(x_ref[...], g_ref[...]).astype(BF16)

    _prefetched_row_tile(x_hbm, x_ref, x_sem, normalise)
    cols = o_ref.shape[1] // n_split
    for s in range(n_split):
        n = slice(s * cols, (s + 1) * cols)
        o_ref[:, n] = jnp.dot(h_ref[...], w_ref[:, n].astype(BF16),
                              preferred_element_type=F32).astype(o_ref.dtype)


def _in_proj(x, g, w, *, tm, tn, n_split):
    s, d = x.shape
    n = w.shape[1]
    assert n // tn >= 2, "the row-tile prefetch starts at inner step 1"
    return pl.pallas_call(
        functools.partial(_in_proj_kernel, n_split=n_split),
        grid=(s // tm, n // tn),
        in_specs=[
            pl.BlockSpec(memory_space=pl.ANY),
            pl.BlockSpec((1, d), lambda i, j: (0, 0)),
            pl.BlockSpec((d, tn), lambda i, j: (0, j)),
        ],
        out_specs=pl.BlockSpec((tm, tn), lambda i, j: (i, j)),
        out_shape=jax.ShapeDtypeStruct((s, n), BF16),
        scratch_shapes=[pltpu.VMEM((tm, d), BF16), pltpu.VMEM((tm, d), F32), pltpu.SemaphoreType.DMA(())],
        compiler_params=pltpu.CompilerParams(
            dimension_semantics=("arbitrary", "arbitrary"),
            vmem_limit_bytes=V7X_VMEM_LIMIT_BYTES),
        name="in_proj",
    )(x, g, w)


def _mixer_kernel(p_ref, ca_ref, sa_ref, cb_ref, sb_ref, cbs_ref, sbs_ref,
                  lng_ref, lnb_ref, ws_ref, bs_ref, gng_ref, gnb_ref,
                  decay_ref, zeta_ref, xi_ref, wout_hbm, x_ref, wg_hbm, wu_hbm, wd_hbm,
                  x1_ref, wgb_ref, wub_ref, wdb_ref,
                  state_ref, wc_ref, bias_ref, woutb_ref, o_ref, prev_ref, stage_ref, stage_sem,
                  wg_stage, wu_stage, wd_stage, ffn_sem, *, chunks_per_step, gamma_c):
    step = pl.program_id(0)
    last_block = pl.num_programs(0) - 2
    ffn_weights = ((wg_hbm, wg_stage, wgb_ref), (wu_hbm, wu_stage, wub_ref), (wd_hbm, wd_stage, wdb_ref))

    def ffn_weight_copies(at_step):
        block = jnp.minimum(at_step, last_block)
        slot = at_step % 2
        copies = []
        for k, (w_hbm, stage, _) in enumerate(ffn_weights):
            rows = stage.shape[1]
            start = pl.multiple_of(block * rows, 8)
            copies.append(pltpu.make_async_copy(w_hbm.at[pl.ds(start, rows), :], stage.at[slot], ffn_sem.at[k, slot]))
        return copies

    row_id = lax.broadcasted_iota(jnp.int32, (CHUNK, CHUNK), 0)
    col_id = lax.broadcasted_iota(jnp.int32, (CHUNK, CHUNK), 1)

    @pl.when(step == 0)
    def _():
        for copy in ffn_weight_copies(0):
            copy.start()
        state_ref[...] = jnp.zeros_like(state_ref)
        o_ref[...] = jnp.zeros_like(o_ref)
        stage_rows = stage_ref.shape[1]
        n_stage = woutb_ref.shape[0] // stage_rows

        def stage_copy(r):
            return pltpu.make_async_copy(wout_hbm.at[pl.ds(r * stage_rows, stage_rows), :],
                                         stage_ref.at[r % 2], stage_sem.at[r % 2])

        stage_copy(0).start()
        for r in range(n_stage):
            if r + 1 < n_stage:
                stage_copy(r + 1).start()
            stage_copy(r).wait()
            woutb_ref[r * stage_rows:(r + 1) * stage_rows, :] = stage_ref[r % 2].astype(BF16)
        for h in range(HEADS):
            wc_ref[h] = jnp.where(row_id >= col_id, ws_ref[h], 0.0).astype(BF16)
            b_col = jnp.sum(jnp.where(row_id == col_id, bs_ref[h:h + 1, :], 0.0), axis=1, keepdims=True)
            bias_ref[h] = jnp.broadcast_to(b_col, (CHUNK, HEAD_DIM))

    @pl.when(step + 1 < pl.num_programs(0))
    def _():
        for copy in ffn_weight_copies(step + 1):
            copy.start()

    k_scale = HEAD_DIM ** -0.5
    chunk0 = jnp.minimum(step, last_block) * chunks_per_step
    heads = range(HEADS)

    prev_ref[...] = o_ref[...]
    piece_cols = x1_ref.shape[1] // N_PROJ_PIECES
    issued = []

    def project_piece():
        n = slice(len(issued) * piece_cols, (len(issued) + 1) * piece_cols)
        issued.append(n)
        x1_ref[:, n] = x_ref[:, n] + jnp.dot(prev_ref[...], woutb_ref[:, n], preferred_element_type=F32)

    def cols(off, h):
        return slice(off + h * HEAD_DIM, off + (h + 1) * HEAD_DIM)

    def chunk_rows(c):
        return slice(c * CHUNK, (c + 1) * CHUNK)

    def spatial_gate(c):
        rows = chunk_rows(c)
        vn = [_norm_rows(p_ref[rows, cols(OFF_VS, h)].astype(F32), lng_ref[h:h + 1, :], lnb_ref[h:h + 1, :])
              .astype(BF16) for h in heads]
        mixed = [jnp.dot(wc_ref[h], vn[h], preferred_element_type=F32) for h in heads]
        for h in heads:
            u = p_ref[rows, cols(OFF_U, h)].astype(F32)
            o_ref[rows, cols(0, h)] = (u * (mixed[h] + bias_ref[h])).astype(o_ref.dtype)

    def retention_scores(c):
        rows = chunk_rows(c)
        ca = ca_ref[pl.ds(chunk0 + c, 1), :]
        sa = sa_ref[pl.ds(chunk0 + c, 1), :]
        cos2 = ca * cb_ref[...] - sa * sb_ref[...]
        sin2 = sa * cbs_ref[...] + ca * sbs_ref[...]
        cos2k = cos2 * k_scale
        sin2k = sin2 * k_scale
        qb, kr = [], []
        for h in heads:
            q = p_ref[rows, cols(OFF_Q, h)].astype(F32)
            k = p_ref[rows, cols(OFF_K, h)].astype(F32)
            qb.append((q * cos2 + pltpu.roll(q, HEAD_DIM // 2, axis=1) * sin2).astype(BF16))
            kr.append(k * cos2k + pltpu.roll(k, HEAD_DIM // 2, axis=1) * sin2k)
        scores = [lax.dot_general(qb[h], kr[h].astype(BF16), (((1,), (1,)), ((), ())),
                                  preferred_element_type=F32) for h in heads]
        return qb, kr, scores

    def retention_values(c, qb, kr, scores):
        rows = chunk_rows(c)
        vr = [p_ref[rows, cols(OFF_VR, h)] for h in heads]
        scores_b = [(scores[h] * decay_ref[h]).astype(BF16) for h in heads]
        kz = [(kr[h] * zeta_ref[h]).astype(BF16) for h in heads]
        state = [state_ref[h] for h in heads]
        intra = [jnp.dot(scores_b[h], vr[h], preferred_element_type=F32) for h in heads]
        inter = [jnp.dot(qb[h], state[h].astype(BF16), preferred_element_type=F32) for h in heads]
        kv = [lax.dot_general(kz[h], vr[h], (((0,), (0,)), ((), ())), preferred_element_type=F32) for h in heads]
        return intra, inter, kv, state

    def retention_gate(c, intra, inter, kv, state):
        rows = chunk_rows(c)
        for h in heads:
            state_ref[h] = gamma_c[h] * state[h] + kv[h]
            on = _norm_rows(intra[h] + inter[h] * xi_ref[h], gng_ref[h:h + 1, :], gnb_ref[h:h + 1, :])
            g = p_ref[rows, cols(OFF_G, h)].astype(F32)
            o_ref[rows, cols(SGU_WIDTH, h)] = (g * jax.nn.sigmoid(g) * on).astype(o_ref.dtype)

    for c in range(chunks_per_step):
        project_piece()
        spatial_gate(c)
        project_piece()
        scored = retention_scores(c)
        project_piece()
        values = retention_values(c, *scored)
        project_piece()
        retention_gate(c, *values)
    assert len(issued) == N_PROJ_PIECES

    for copy, (_, stage, out_ref) in zip(ffn_weight_copies(step), ffn_weights):
        copy.wait()
        out_ref[...] = stage[step % 2].astype(BF16)


def _mixer_constants(seq):
    half = HEAD_DIM // 2
    inv = 1.0 / (ROPE_BASE ** (np.arange(half, dtype=np.float64) / half))
    inv2 = np.concatenate([inv, inv])
    sign = np.concatenate([-np.ones(half), np.ones(half)])
    ang_a = (CHUNK * np.arange(seq // CHUNK, dtype=np.float64))[:, None] * inv2[None, :]
    ang_b = np.arange(CHUNK, dtype=np.float64)[:, None] * inv2[None, :]
    rot = [np.cos(ang_a), np.sin(ang_a), np.cos(ang_b), np.sin(ang_b),
           sign * np.cos(ang_b), sign * np.sin(ang_b)]

    log_gamma = np.log(1.0 - np.exp2(-5.0 - np.arange(HEADS, dtype=np.float64)))
    idx = np.arange(CHUNK, dtype=np.float64)
    diff = idx[:, None] - idx[None, :]
    decay = np.where(diff[None] >= 0, np.exp(np.maximum(diff, 0.0)[None] * log_gamma[:, None, None]), 0.0)
    zeta = np.exp((CHUNK - 1.0 - idx)[None, :] * log_gamma[:, None])
    xi = np.exp((idx + 1.0)[None, :] * log_gamma[:, None])
    bcast = lambda v: np.broadcast_to(v[:, :, None], (HEADS, CHUNK, HEAD_DIM))
    gamma_c = tuple(float(np.float32(v)) for v in np.exp(CHUNK * log_gamma))
    tables = [jnp.asarray(np.ascontiguousarray(t), dtype=F32) for t in rot + [decay, bcast(zeta), bcast(xi)]]
    return tables, gamma_c


def _mixer_out_proj(proj, lng, lnb, ws, bs, gng, gnb, w_out, x, w_gate, w_up, w_down, *, chunks_per_step):
    s, d = x.shape
    dff = w_gate.shape[1]
    n_chunks = s // CHUNK
    rows = chunks_per_step * CHUNK
    n_tiles = s // rows
    gate_rows, down_rows = d // n_tiles, dff // n_tiles
    cast_block = lambda i: (jnp.minimum(i, n_tiles - 1), 0)
    (ca, sa, cb, sb, cbs, sbs, decay, zeta, xi), gamma_c = _mixer_constants(s)
    full = lambda shape, **kw: pl.BlockSpec(shape, lambda i: (0,) * len(shape), **kw)
    head_tiles = full((HEADS, CHUNK, HEAD_DIM))
    head_rows = full((HEADS, HEAD_DIM))
    mixed_tile = lambda i: (jnp.minimum(i, n_tiles - 1), 0)
    projected_tile = lambda i: (jnp.maximum(i - 1, 0), 0)
    return pl.pallas_call(
        functools.partial(_mixer_kernel, chunks_per_step=chunks_per_step, gamma_c=gamma_c),
        grid=(n_tiles + 1,),
        in_specs=[
            pl.BlockSpec((rows, IN_WIDTH), mixed_tile),
            full((n_chunks, HEAD_DIM)), full((n_chunks, HEAD_DIM)),
            full((CHUNK, HEAD_DIM)), full((CHUNK, HEAD_DIM)), full((CHUNK, HEAD_DIM)), full((CHUNK, HEAD_DIM)),
            head_rows, head_rows, head_tiles, head_rows, head_rows, head_rows,
            head_tiles, head_tiles, head_tiles,
            pl.BlockSpec(memory_space=pl.ANY),
            pl.BlockSpec((rows, d), projected_tile),
            pl.BlockSpec(memory_space=pl.ANY),
            pl.BlockSpec(memory_space=pl.ANY),
            pl.BlockSpec(memory_space=pl.ANY),
        ],
        out_specs=[pl.BlockSpec((rows, d), projected_tile),
                   pl.BlockSpec((gate_rows, dff), cast_block),
                   pl.BlockSpec((gate_rows, dff), cast_block),
                   pl.BlockSpec((down_rows, d), cast_block)],
        out_shape=[jax.ShapeDtypeStruct((s, d), F32),
                   jax.ShapeDtypeStruct(w_gate.shape, BF16),
                   jax.ShapeDtypeStruct(w_up.shape, BF16),
                   jax.ShapeDtypeStruct(w_down.shape, BF16)],
        scratch_shapes=[pltpu.VMEM((HEADS, HEAD_DIM, HEAD_DIM), F32),
                        pltpu.VMEM((HEADS, CHUNK, CHUNK), BF16),
                        pltpu.VMEM((HEADS, CHUNK, HEAD_DIM), F32),
                        pltpu.VMEM((MIX_WIDTH, d), BF16),
                        pltpu.VMEM((rows, MIX_WIDTH), BF16),
                        pltpu.VMEM((rows, MIX_WIDTH), BF16),
                        pltpu.VMEM((2, W_OUT_STAGE_ROWS, d), F32),
                        pltpu.SemaphoreType.DMA((2,)),
                        pltpu.VMEM((2, gate_rows, dff), F32),
                        pltpu.VMEM((2, gate_rows, dff), F32),
                        pltpu.VMEM((2, down_rows, d), F32),
                        pltpu.SemaphoreType.DMA((3, 2))],
        compiler_params=pltpu.CompilerParams(
            dimension_semantics=("arbitrary",),
            vmem_limit_bytes=V7X_VMEM_LIMIT_BYTES),
        name="mixer_out_proj",
    )(proj, ca, sa, cb, sb, cbs, sbs, lng, lnb, ws, bs, gng, gnb, decay, zeta, xi, w_out, x,
      w_gate, w_up, w_down)


def _ffn_kernel(x_hbm, g2_ref, wga_ref, wua_ref, wda_ref, wgb_ref, wub_ref, wdb_ref, gf_ref,
                o_ref, h_ref, x_ref, x_sem, *, n_tiles, final_norm):
    f = pl.program_id(1)
    last = pl.num_programs(1) - 1
    tiles_in_last = n_tiles - 2 * (pl.cdiv(n_tiles, 2) - 1)

    def ffn_step(is_first, is_last):
        if is_first:
            base = x_ref[...]
            h = _rms_rows(base, g2_ref[...]).astype(BF16)
            h_ref[...] = h
        else:
            base = o_ref[...]
            h = h_ref[...]
        windows = [(wga_ref, wua_ref, wda_ref), (wgb_ref, wub_ref, wdb_ref)]
        if is_last:
            windows = windows[:tiles_in_last]
        for k, (wg_ref, wu_ref, wd_ref) in enumerate(windows):
            gate = jnp.dot(h, wg_ref[...], preferred_element_type=F32)
            up = jnp.dot(h, wu_ref[...], preferred_element_type=F32)
            a = (gate * jax.nn.sigmoid(gate) * up).astype(BF16)
            y = (base if k == 0 else o_ref[...]) + jnp.dot(a, wd_ref[...], preferred_element_type=F32)
            finish = is_last and final_norm and k == len(windows) - 1
            o_ref[...] = _rms_rows(y, gf_ref[...]) if finish else y

    _prefetched_row_tile(x_hbm, x_ref, x_sem, functools.partial(ffn_step, True, False))
    pl.when((f > 0) & (f < last))(functools.partial(ffn_step, False, False))
    pl.when(f == last)(functools.partial(ffn_step, False, True))


def _ffn(x1, g2, wg, wu, wd, gf, *, tm, tf, final_norm):
    s, d = x1.shape
    dff = wg.shape[1]
    n_tiles = dff // tf
    n_steps = pl.cdiv(n_tiles, 2)
    assert n_steps >= 2, "first and last inner steps are distinct code paths"

    def tile(k):
        return lambda i, f: jnp.minimum(2 * f + k, n_tiles - 1)

    def cols(k):
        return pl.BlockSpec((d, tf), lambda i, f: (0, tile(k)(i, f)))

    def rows(k):
        return pl.BlockSpec((tf, d), lambda i, f: (tile(k)(i, f), 0))

    return pl.pallas_call(
        functools.partial(_ffn_kernel, n_tiles=n_tiles, final_norm=final_norm),
        grid=(s // tm, n_steps),
        in_specs=[
            pl.BlockSpec(memory_space=pl.ANY),
            pl.BlockSpec((1, d), lambda i, f: (0, 0)),
            cols(0), cols(0), rows(0), cols(1), cols(1), rows(1),
            pl.BlockSpec((1, d), lambda i, f: (0, 0)),
        ],
        out_specs=pl.BlockSpec((tm, d), lambda i, f: (i, 0)),
        out_shape=jax.ShapeDtypeStruct((s, d), F32),
        scratch_shapes=[pltpu.VMEM((tm, d), BF16), pltpu.VMEM((tm, d), F32), pltpu.SemaphoreType.DMA(())],
        compiler_params=pltpu.CompilerParams(
            dimension_semantics=("arbitrary", "arbitrary"),
            vmem_limit_bytes=FFN_VMEM_LIMIT_BYTES),
        name="ffn",
    )(x1, g2, wg, wu, wd, wg, wu, wd, gf)


def kernel(x, norm1_g, w_in, sgu_ln_g, sgu_ln_b, w_spatial, b_spatial, ret_gn_g, ret_gn_b,
           w_out, norm2_g, w_gate, w_up, w_down, final_norm_g):
    batch, seq, d = x.shape
    depth = w_in.shape[0]
    outs = []
    for b in range(batch):
        xb = x[b]
        for l in range(depth):
            proj = _in_proj(xb, norm1_g[l][None, :], w_in[l],
                            tm=IN_PROJ_ROWS, tn=IN_PROJ_COLS, n_split=IN_PROJ_DOTS_PER_STEP)
            x1, wg, wu, wd = _mixer_out_proj(proj, sgu_ln_g[l], sgu_ln_b[l], w_spatial[l], b_spatial[l],
                                             ret_gn_g[l], ret_gn_b[l], w_out[l], xb,
                                             w_gate[l], w_up[l], w_down[l],
                                             chunks_per_step=MIXER_CHUNKS_PER_STEP)
            xb = _ffn(x1, norm2_g[l][None, :], wg, wu, wd, final_norm_g[None, :],
                      tm=FFN_ROWS, tf=FFN_COLS, final_norm=(l == depth - 1))
        outs.append(xb)
    return outs[0][None] if batch == 1 else jnp.stack(outs)
```

```python
import functools

import numpy as np

import jax
import jax.numpy as jnp
from jax import lax
from jax.experimental import pallas as pl
from jax.experimental.pallas import tpu as pltpu

CHUNK = 128
HEADS = 8
HEAD_DIM = 128
SGU_WIDTH = HEADS * HEAD_DIM
RET_WIDTH = HEADS * HEAD_DIM
MIX_WIDTH = SGU_WIDTH + RET_WIDTH
IN_WIDTH = 2 * SGU_WIDTH + 4 * RET_WIDTH
ROPE_BASE = 10000.0
EPS = 1e-6

OFF_U, OFF_VS, OFF_Q, OFF_K, OFF_VR, OFF_G = (i * SGU_WIDTH for i in range(6))

V7X_VMEM_LIMIT_BYTES = 60 * 1024 * 1024
FFN_WEIGHT_COPY_PRIORITY = 1

N_PROJ_PIECES = 8
W_OUT_STAGE_ROWS = 256

IN_PROJ_ROWS, IN_PROJ_COLS, IN_PROJ_DOTS_PER_STEP = 2048, 1024, 2
MIXER_CHUNKS_PER_STEP = 2
FFN_ROWS, FFN_COLS = 1024, 512

F32 = jnp.float32
BF16 = jnp.bfloat16


def _rms_rows(x, g):
    ms = jnp.mean(x * x, axis=-1, keepdims=True)
    return x * lax.rsqrt(ms + EPS) * g


def _norm_rows(x, g, b):
    mu = jnp.mean(x, axis=-1, keepdims=True)
    d = x - mu
    var = jnp.mean(d * d, axis=-1, keepdims=True)
    return d * lax.rsqrt(var + EPS) * g + b


def _row_tile_copy(x_hbm, x_ref, sem, tile):
    tm = x_ref.shape[0]
    start = pl.multiple_of(tile * tm, tm)
    return pltpu.make_async_copy(x_hbm.at[pl.ds(start, tm), :], x_ref, sem)


def _prefetched_row_tile(x_hbm, x_ref, sem, consume):
    i, j = pl.program_id(0), pl.program_id(1)

    @pl.when((i == 0) & (j == 0))
    def _():
        _row_tile_copy(x_hbm, x_ref, sem, 0).start()

    @pl.when(j == 0)
    def _():
        _row_tile_copy(x_hbm, x_ref, sem, i).wait()
        consume()

    @pl.when((j == 1) & (i + 1 < pl.num_programs(0)))
    def _():
        _row_tile_copy(x_hbm, x_ref, sem, i + 1).start()


def _in_proj_kernel(x_hbm, g_ref, w_ref, o_ref, h_ref, x_ref, x_sem, *, n_split):
    def normalise():
        h_ref[...] = _rms_rows(x_ref[...], g_ref[...]).astype(BF16)

    _prefetched_row_tile(x_hbm, x_ref, x_sem, normalise)
    cols = o_ref.shape[1] // n_split
    for s in range(n_split):
        n = slice(s * cols, (s + 1) * cols)
        o_ref[:, n] = jnp.dot(h_ref[...], w_ref[:, n].astype(BF16),
                              preferred_element_type=F32).astype(o_ref.dtype)


def _in_proj(x, g, w, *, tm, tn, n_split):
    s, d = x.shape
    n = w.shape[1]
    assert n // tn >= 2, "the row-tile prefetch starts at inner step 1"
    return pl.pallas_call(
        functools.partial(_in_proj_kernel, n_split=n_split),
        grid=(s // tm, n // tn),
        in_specs=[
            pl.BlockSpec(memory_space=pl.ANY),
            pl.BlockSpec((1, d), lambda i, j: (0, 0)),
            pl.BlockSpec((d, tn), lambda i, j: (0, j)),
        ],
        out_specs=pl.BlockSpec((tm, tn), lambda i, j: (i, j)),
        out_shape=jax.ShapeDtypeStruct((s, n), BF16),
        scratch_shapes=[pltpu.VMEM((tm, d), BF16), pltpu.VMEM((tm, d), F32), pltpu.SemaphoreType.DMA(())],
        compiler_params=pltpu.CompilerParams(
            dimension_semantics=("arbitrary", "arbitrary"),
            vmem_limit_bytes=V7X_VMEM_LIMIT_BYTES),
        name="in_proj",
    )(x, g, w)


def _mixer_kernel(p_ref, ca_ref, sa_ref, cb_ref, sb_ref, cbs_ref, sbs_ref,
                  lng_ref, lnb_ref, ws_ref, bs_ref, gng_ref, gnb_ref,
                  decay_ref, zeta_ref, xi_ref, wout_hbm, x_ref, wg_hbm, wu_hbm, wd_hbm,
                  x1_ref, wgb_ref, wub_ref, wdb_ref,
                  state_ref, wc_ref, bias_ref, woutb_ref, o_ref, prev_ref, stage_ref, stage_sem,
                  wg_stage, wu_stage, wd_stage, ffn_sem, *, chunks_per_step, gamma_c):
    step = pl.program_id(0)
    last_block = pl.num_programs(0) - 2
    ffn_weights = ((wg_hbm, wg_stage, wgb_ref), (wu_hbm, wu_stage, wub_ref), (wd_hbm, wd_stage, wdb_ref))

    def ffn_weight_copies(at_step):
        block = jnp.minimum(at_step, last_block)
        slot = at_step % 2
        copies = []
        for k, (w_hbm, stage, _) in enumerate(ffn_weights):
            rows = stage.shape[1]
            start = pl.multiple_of(block * rows, 8)
            copies.append(pltpu.make_async_copy(w_hbm.at[pl.ds(start, rows), :], stage.at[slot], ffn_sem.at[k, slot]))
        return copies

    row_id = lax.broadcasted_iota(jnp.int32, (CHUNK, CHUNK), 0)
    col_id = lax.broadcasted_iota(jnp.int32, (CHUNK, CHUNK), 1)

    @pl.when(step == 0)
    def _():
        for copy in ffn_weight_copies(0):
            copy.start(priority=FFN_WEIGHT_COPY_PRIORITY)
        state_ref[...] = jnp.zeros_like(state_ref)
        o_ref[...] = jnp.zeros_like(o_ref)
        stage_rows = stage_ref.shape[1]
        n_stage = woutb_ref.shape[0] // stage_rows

        def stage_copy(r):
            return pltpu.make_async_copy(wout_hbm.at[pl.ds(r * stage_rows, stage_rows), :],
                                         stage_ref.at[r % 2], stage_sem.at[r % 2])

        stage_copy(0).start()
        for r in range(n_stage):
            if r + 1 < n_stage:
                stage_copy(r + 1).start()
            stage_copy(r).wait()
            woutb_ref[r * stage_rows:(r + 1) * stage_rows, :] = stage_ref[r % 2].astype(BF16)
        for h in range(HEADS):
            wc_ref[h] = jnp.where(row_id >= col_id, ws_ref[h], 0.0).astype(BF16)
            b_col = jnp.sum(jnp.where(row_id == col_id, bs_ref[h:h + 1, :], 0.0), axis=1, keepdims=True)
            bias_ref[h] = jnp.broadcast_to(b_col, (CHUNK, HEAD_DIM))

    @pl.when(step + 1 < pl.num_programs(0))
    def _():
        for copy in ffn_weight_copies(step + 1):
            copy.start(priority=FFN_WEIGHT_COPY_PRIORITY)

    k_scale = HEAD_DIM ** -0.5
    chunk0 = jnp.minimum(step, last_block) * chunks_per_step
    heads = range(HEADS)

    prev_ref[...] = o_ref[...]
    piece_cols = x1_ref.shape[1] // N_PROJ_PIECES
    issued = []

    def project_piece():
        n = slice(len(issued) * piece_cols, (len(issued) + 1) * piece_cols)
        issued.append(n)
        x1_ref[:, n] = x_ref[:, n] + jnp.dot(prev_ref[...], woutb_ref[:, n], preferred_element_type=F32)

    def cols(off, h):
        return slice(off + h * HEAD_DIM, off + (h + 1) * HEAD_DIM)

    def chunk_rows(c):
        return slice(c * CHUNK, (c + 1) * CHUNK)

    def spatial_gate(c):
        rows = chunk_rows(c)
        vn = [_norm_rows(p_ref[rows, cols(OFF_VS, h)].astype(F32), lng_ref[h:h + 1, :], lnb_ref[h:h + 1, :])
              .astype(BF16) for h in heads]
        mixed = [jnp.dot(wc_ref[h], vn[h], preferred_element_type=F32) for h in heads]
        for h in heads:
            u = p_ref[rows, cols(OFF_U, h)].astype(F32)
            o_ref[rows, cols(0, h)] = (u * (mixed[h] + bias_ref[h])).astype(o_ref.dtype)

    def retention_scores(c):
        rows = chunk_rows(c)
        ca = ca_ref[pl.ds(chunk0 + c, 1), :]
        sa = sa_ref[pl.ds(chunk0 + c, 1), :]
        cos2 = ca * cb_ref[...] - sa * sb_ref[...]
        sin2 = sa * cbs_ref[...] + ca * sbs_ref[...]
        cos2k = cos2 * k_scale
        sin2k = sin2 * k_scale
        qb, kr = [], []
        for h in heads:
            q = p_ref[rows, cols(OFF_Q, h)].astype(F32)
            k = p_ref[rows, cols(OFF_K, h)].astype(F32)
            qb.append((q * cos2 + pltpu.roll(q, HEAD_DIM // 2, axis=1) * sin2).astype(BF16))
            kr.append(k * cos2k + pltpu.roll(k, HEAD_DIM // 2, axis=1) * sin2k)
        scores = [lax.dot_general(qb[h], kr[h].astype(BF16), (((1,), (1,)), ((), ())),
                                  preferred_element_type=F32) for h in heads]
        return qb, kr, scores

    def retention_values(c, qb, kr, scores):
        rows = chunk_rows(c)
        vr = [p_ref[rows, cols(OFF_VR, h)] for h in heads]
        scores_b = [(scores[h] * decay_ref[h]).astype(BF16) for h in heads]
        kz = [(kr[h] * zeta_ref[h]).astype(BF16) for h in heads]
        state = [state_ref[h] for h in heads]
        intra = [jnp.dot(scores_b[h], vr[h], preferred_element_type=F32) for h in heads]
        inter = [jnp.dot(qb[h], state[h].astype(BF16), preferred_element_type=F32) for h in heads]
        kv = [lax.dot_general(kz[h], vr[h], (((0,), (0,)), ((), ())), preferred_element_type=F32) for h in heads]
        return intra, inter, kv, state

    def retention_gate(c, intra, inter, kv, state):
        rows = chunk_rows(c)
        for h in heads:
            state_ref[h] = gamma_c[h] * state[h] + kv[h]
            on = _norm_rows(intra[h] + inter[h] * xi_ref[h], gng_ref[h:h + 1, :], gnb_ref[h:h + 1, :])
            g = p_ref[rows, cols(OFF_G, h)].astype(F32)
            o_ref[rows, cols(SGU_WIDTH, h)] = (g * jax.nn.sigmoid(g) * on).astype(o_ref.dtype)

    for c in range(chunks_per_step):
        project_piece()
        spatial_gate(c)
        project_piece()
        scored = retention_scores(c)
        project_piece()
        values = retention_values(c, *scored)
        project_piece()
        retention_gate(c, *values)
    assert len(issued) == N_PROJ_PIECES

    for copy, (_, stage, out_ref) in zip(ffn_weight_copies(step), ffn_weights):
        copy.wait()
        out_ref[...] = stage[step % 2].astype(BF16)


def _mixer_constants(seq):
    half = HEAD_DIM // 2
    inv = 1.0 / (ROPE_BASE ** (np.arange(half, dtype=np.float64) / half))
    inv2 = np.concatenate([inv, inv])
    sign = np.concatenate([-np.ones(half), np.ones(half)])
    ang_a = (CHUNK * np.arange(seq // CHUNK, dtype=np.float64))[:, None] * inv2[None, :]
    ang_b = np.arange(CHUNK, dtype=np.float64)[:, None] * inv2[None, :]
    rot = [np.cos(ang_a), np.sin(ang_a), np.cos(ang_b), np.sin(ang_b),
           sign * np.cos(ang_b), sign * np.sin(ang_b)]

    log_gamma = np.log(1.0 - np.exp2(-5.0 - np.arange(HEADS, dtype=np.float64)))
    idx = np.arange(CHUNK, dtype=np.float64)
    diff = idx[:, None] - idx[None, :]
    decay = np.where(diff[None] >= 0, np.exp(np.maximum(diff, 0.0)[None] * log_gamma[:, None, None]), 0.0)
    zeta = np.exp((CHUNK - 1.0 - idx)[None, :] * log_gamma[:, None])
    xi = np.exp((idx + 1.0)[None, :] * log_gamma[:, None])
    bcast = lambda v: np.broadcast_to(v[:, :, None], (HEADS, CHUNK, HEAD_DIM))
    gamma_c = tuple(float(np.float32(v)) for v in np.exp(CHUNK * log_gamma))
    tables = [jnp.asarray(np.ascontiguousarray(t), dtype=F32) for t in rot + [decay, bcast(zeta), bcast(xi)]]
    return tables, gamma_c


def _mixer_out_proj(proj, lng, lnb, ws, bs, gng, gnb, w_out, x, w_gate, w_up, w_down, *, chunks_per_step):
    s, d = x.shape
    dff = w_gate.shape[1]
    n_chunks = s // CHUNK
    rows = chunks_per_step * CHUNK
    n_tiles = s // rows
    gate_rows, down_rows = d // n_tiles, dff // n_tiles
    cast_block = lambda i: (jnp.minimum(i, n_tiles - 1), 0)
    (ca, sa, cb, sb, cbs, sbs, decay, zeta, xi), gamma_c = _mixer_constants(s)
    full = lambda shape, **kw: pl.BlockSpec(shape, lambda i: (0,) * len(shape), **kw)
    head_tiles = full((HEADS, CHUNK, HEAD_DIM))
    head_rows = full((HEADS, HEAD_DIM))
    mixed_tile = lambda i: (jnp.minimum(i, n_tiles - 1), 0)
    projected_tile = lambda i: (jnp.maximum(i - 1, 0), 0)
    return pl.pallas_call(
        functools.partial(_mixer_kernel, chunks_per_step=chunks_per_step, gamma_c=gamma_c),
        grid=(n_tiles + 1,),
        in_specs=[
            pl.BlockSpec((rows, IN_WIDTH), mixed_tile),
            full((n_chunks, HEAD_DIM)), full((n_chunks, HEAD_DIM)),
            full((CHUNK, HEAD_DIM)), full((CHUNK, HEAD_DIM)), full((CHUNK, HEAD_DIM)), full((CHUNK, HEAD_DIM)),
            head_rows, head_rows, head_tiles, head_rows, head_rows, head_rows,
            head_tiles, head_tiles, head_tiles,
            pl.BlockSpec(memory_space=pl.ANY),
            pl.BlockSpec((rows, d), projected_tile),
            pl.BlockSpec(memory_space=pl.ANY),
            pl.BlockSpec(memory_space=pl.ANY),
            pl.BlockSpec(memory_space=pl.ANY),
        ],
        out_specs=[pl.BlockSpec((rows, d), projected_tile),
                   pl.BlockSpec((gate_rows, dff), cast_block),
                   pl.BlockSpec((gate_rows, dff), cast_block),
                   pl.BlockSpec((down_rows, d), cast_block)],
        out_shape=[jax.ShapeDtypeStruct((s, d), F32),
                   jax.ShapeDtypeStruct(w_gate.shape, BF16),
                   jax.ShapeDtypeStruct(w_up.shape, BF16),
                   jax.ShapeDtypeStruct(w_down.shape, BF16)],
        scratch_shapes=[pltpu.VMEM((HEADS, HEAD_DIM, HEAD_DIM), F32),
                        pltpu.VMEM((HEADS, CHUNK, CHUNK), BF16),
                        pltpu.VMEM((HEADS, CHUNK, HEAD_DIM), F32),
                        pltpu.VMEM((MIX_WIDTH, d), BF16),
                        pltpu.VMEM((rows, MIX_WIDTH), BF16),
                        pltpu.VMEM((rows, MIX_WIDTH), BF16),
                        pltpu.VMEM((2, W_OUT_STAGE_ROWS, d), F32),
                        pltpu.SemaphoreType.DMA((2,)),
                        pltpu.VMEM((2, gate_rows, dff), F32),
                        pltpu.VMEM((2, gate_rows, dff), F32),
                        pltpu.VMEM((2, down_rows, d), F32),
                        pltpu.SemaphoreType.DMA((3, 2))],
        compiler_params=pltpu.CompilerParams(
            dimension_semantics=("arbitrary",),
            vmem_limit_bytes=V7X_VMEM_LIMIT_BYTES),
        name="mixer_out_proj",
    )(proj, ca, sa, cb, sb, cbs, sbs, lng, lnb, ws, bs, gng, gnb, decay, zeta, xi, w_out, x,
      w_gate, w_up, w_down)


def _ffn_kernel(x_hbm, g2_ref, wg_ref, wu_ref, wd_ref, gf_ref, o_ref, h_ref, x_ref, x_sem, *, final_norm):
    f = pl.program_id(1)
    last = pl.num_programs(1) - 1

    def ffn_step(is_first, is_last):
        if is_first:
            base = x_ref[...]
            h = _rms_rows(base, g2_ref[...]).astype(BF16)
            h_ref[...] = h
        else:
            base = o_ref[...]
            h = h_ref[...]
        gate = jnp.dot(h, wg_ref[...], preferred_element_type=F32)
        up = jnp.dot(h, wu_ref[...], preferred_element_type=F32)
        a = (gate * jax.nn.sigmoid(gate) * up).astype(BF16)
        y = base + jnp.dot(a, wd_ref[...], preferred_element_type=F32)
        o_ref[...] = _rms_rows(y, gf_ref[...]) if (is_last and final_norm) else y

    _prefetched_row_tile(x_hbm, x_ref, x_sem, functools.partial(ffn_step, True, False))
    pl.when((f > 0) & (f < last))(functools.partial(ffn_step, False, False))
    pl.when(f == last)(functools.partial(ffn_step, False, True))


def _ffn(x1, g2, wg, wu, wd, gf, *, tm, tf, final_norm):
    s, d = x1.shape
    dff = wg.shape[1]
    assert dff // tf >= 2, "first and last inner steps are distinct code paths"
    return pl.pallas_call(
        functools.partial(_ffn_kernel, final_norm=final_norm),
        grid=(s // tm, dff // tf),
        in_specs=[
            pl.BlockSpec(memory_space=pl.ANY),
            pl.BlockSpec((1, d), lambda i, f: (0, 0)),
            pl.BlockSpec((d, tf), lambda i, f: (0, f)),
            pl.BlockSpec((d, tf), lambda i, f: (0, f)),
            pl.BlockSpec((tf, d), lambda i, f: (f, 0)),
            pl.BlockSpec((1, d), lambda i, f: (0, 0)),
        ],
        out_specs=pl.BlockSpec((tm, d), lambda i, f: (i, 0)),
        out_shape=jax.ShapeDtypeStruct((s, d), F32),
        scratch_shapes=[pltpu.VMEM((tm, d), BF16), pltpu.VMEM((tm, d), F32), pltpu.SemaphoreType.DMA(())],
        compiler_params=pltpu.CompilerParams(
            dimension_semantics=("arbitrary", "arbitrary"),
            vmem_limit_bytes=V7X_VMEM_LIMIT_BYTES),
        name="ffn",
    )(x1, g2, wg, wu, wd, gf)


def kernel(x, norm1_g, w_in, sgu_ln_g, sgu_ln_b, w_spatial, b_spatial, ret_gn_g, ret_gn_b,
           w_out, norm2_g, w_gate, w_up, w_down, final_norm_g):
    batch, seq, d = x.shape
    depth = w_in.shape[0]
    outs = []
    for b in range(batch):
        xb = x[b]
        for l in range(depth):
            proj = _in_proj(xb, norm1_g[l][None, :], w_in[l],
                            tm=IN_PROJ_ROWS, tn=IN_PROJ_COLS, n_split=IN_PROJ_DOTS_PER_STEP)
            x1, wg, wu, wd = _mixer_out_proj(proj, sgu_ln_g[l], sgu_ln_b[l], w_spatial[l], b_spatial[l],
                                             ret_gn_g[l], ret_gn_b[l], w_out[l], xb,
                                             w_gate[l], w_up[l], w_down[l],
                                             chunks_per_step=MIXER_CHUNKS_PER_STEP)
            xb = _ffn(x1, norm2_g[l][None, :], wg, wu, wd, final_norm_g[None, :],
                      tm=FFN_ROWS, tf=FFN_COLS, final_norm=(l == depth - 1))
        outs.append(xb)
    return outs[0][None] if batch == 1 else jnp.stack(outs)
```
